```python
import jax
import jax.numpy as jnp
from jax import lax
import numpy as np

D_MODEL = 2048
BATCH = 4
SEQ = 2048
DEPTH = 2
DEC_BATCH = 128
DEC_SEQ = 4
PAST_LEN = 16384
PAGE_SIZE = 128

N_EVEN = (DEPTH + 1) // 2
N_ODD = DEPTH // 2
D_A = D_MODEL // 2
HGRN_EXPAND = 128
H_A = D_A // HGRN_EXPAND
DK_A = HGRN_EXPAND
DV_A = D_A // H_A
CHUNK_A = 16
D_B = D_MODEL - D_A
HEAD_B = 64
H_B = D_B // HEAD_B
LORA_W = 64
LORA_A = 64
LORA_G = 128
D_SHIFT = 3 * D_B + LORA_W + LORA_A + LORA_G
D_IN_AB = 4 * D_A + D_SHIFT
D_CONV = D_MODEL
CONV_W = 3
D_FF = 5632
N_EXPERTS = 8
TOP_K = 2
D_FF_EXPERT = 7168
NORM_EPS = 1e-6
RWKV_GN_EPS = 64e-5

kernel_name = 'hgrn2_rwkv7_shortconv_moe_decode_step'


def rms_norm(x, gain):
    xf = x.astype(jnp.float32)
    y = xf * lax.rsqrt(jnp.mean(xf * xf, axis=-1, keepdims=True) + NORM_EPS)
    return (y * gain.astype(jnp.float32)).astype(x.dtype)


def swiglu(x, w1, w3, w2):
    return (jax.nn.silu(x @ w1) * (x @ w3)) @ w2


def _chunk_len(length):
    return max(d for d in range(1, min(CHUNK_A, length) + 1) if length % d == 0)


def hgrn2_chunked(q, k, g, v, s0):
    B, L, H, DK = q.shape
    DV = v.shape[-1]
    c = _chunk_len(L)
    n = L // c

    def to_chunks(t):
        return t.astype(jnp.float32).reshape(B, n, c, H, t.shape[-1]).transpose(1, 0, 3, 2, 4)

    qc, kc, gc, vc = to_chunks(q), to_chunks(k), to_chunks(g), to_chunks(v)
    causal = jnp.tril(jnp.ones((c, c), dtype=bool))[:, :, None]

    def step(S, inp):
        qi, ki, gi, vi = inp
        b = jnp.cumsum(gi, axis=2)
        diff = b[:, :, :, None, :] - b[:, :, None, :, :]
        dec = jnp.exp(jnp.where(causal, diff, -jnp.inf))
        scores = jnp.einsum('bhtd,bhsd,bhtsd->bhts', qi, ki, dec)
        o = jnp.einsum('bhts,bhsv->bhtv', scores, vi) + jnp.einsum('bhtd,bhdv->bhtv', qi * jnp.exp(b), S)
        b_last = b[:, :, -1:, :]
        S = jnp.exp(b_last[:, :, 0, :, None]) * S + jnp.einsum('bhsd,bhsv->bhdv', ki * jnp.exp(b_last - b), vi)
        return S, o

    S, o = lax.scan(step, s0.astype(jnp.float32), (qc, kc, gc, vc))
    o = o.transpose(1, 0, 3, 2, 4).reshape(B, L, H, DV)
    return o, S


def rwkv7_scan(r, w, k, v, kk, kka, s0):
    def step(S, inp):
        rt, wt, kt, vt, kkt, kat = inp
        sa = jnp.einsum('bhvk,bhk->bhv', S, kkt)
        S = S * wt[:, :, None, :] - sa[..., None] * kat[:, :, None, :] + vt[..., None] * kt[:, :, None, :]
        return S, jnp.einsum('bhvk,bhk->bhv', S, rt)

    xs = (jnp.moveaxis(r, 1, 0), jnp.moveaxis(w, 1, 0), jnp.moveaxis(k, 1, 0),
          jnp.moveaxis(v, 1, 0), jnp.moveaxis(kk, 1, 0), jnp.moveaxis(kka, 1, 0))
    S, y = lax.scan(step, s0.astype(jnp.float32), xs)
    return jnp.moveaxis(y, 0, 1), S


def even_mixer(xn, s_hgrn, s_rwkv, s_shift, lb, p, j):
    B, L, _ = xn.shape
    f32 = jnp.float32
    proj = xn @ p['w_in_ab'][j]
    q_a, f_a, i_a, go_a, u = jnp.split(proj, [D_A, 2 * D_A, 3 * D_A, 4 * D_A], axis=-1)
    f = lb + (1.0 - lb) * jax.nn.sigmoid(f_a.astype(f32))
    o_a, s_hgrn_new = hgrn2_chunked(jax.nn.silu(q_a).reshape(B, L, H_A, DK_A),
                                    (1.0 - f).reshape(B, L, H_A, DK_A),
                                    jnp.log(f).reshape(B, L, H_A, DK_A),
                                    i_a.reshape(B, L, H_A, DV_A), s_hgrn)
    o_a = rms_norm(o_a, p['hgrn_norm'][j]).reshape(B, L, D_A) * jax.nn.silu(go_a).astype(f32)
    u_prev = jnp.concatenate([s_shift[:, None, :].astype(u.dtype), u[:, :-1]], axis=1)
    us = u + (u_prev - u) * p['rwkv_mu'][j]
    r, k, v, wd, ad, gd = jnp.split(us, [D_B, 2 * D_B, 3 * D_B, 3 * D_B + LORA_W, 3 * D_B + LORA_W + LORA_A], axis=-1)
    w_log = -jax.nn.softplus(-(p['rwkv_w0'][j] + jnp.tanh(wd) @ p['rwkv_w2'][j]).astype(f32)) - 0.5
    decay = jnp.exp(-jnp.exp(w_log))
    a = jax.nn.sigmoid((p['rwkv_a0'][j] + ad @ p['rwkv_a2'][j]).astype(f32))
    g = (jax.nn.sigmoid(gd) @ p['rwkv_g2'][j]).astype(f32)
    kk = (k * p['rwkv_k_k'][j]).astype(f32).reshape(B, L, H_B, HEAD_B)
    kk = kk / jnp.maximum(jnp.sqrt(jnp.sum(kk * kk, axis=-1, keepdims=True)), 1e-12)
    kmod = k.astype(f32) * (1.0 + (a - 1.0) * p['rwkv_k_a'][j].astype(f32))
    rh = r.astype(f32).reshape(B, L, H_B, HEAD_B)
    kh = kmod.reshape(B, L, H_B, HEAD_B)
    vh = v.astype(f32).reshape(B, L, H_B, HEAD_B)
    ah = a.reshape(B, L, H_B, HEAD_B)
    y, s_rwkv_new = rwkv7_scan(rh, decay.reshape(B, L, H_B, HEAD_B), kh, vh, kk, kk * ah, s_rwkv)
    mean = jnp.mean(y, axis=-1, keepdims=True)
    var = jnp.mean(jnp.square(y - mean), axis=-1, keepdims=True)
    yn = ((y - mean) * lax.rsqrt(var + RWKV_GN_EPS)).reshape(B, L, D_B) * p['rwkv_ln_w'][j] + p['rwkv_ln_b'][j]
    bonus = jnp.sum(rh * kh * p['rwkv_r_k'][j].astype(f32), axis=-1, keepdims=True) * vh
    o_b = (yn + bonus.reshape(B, L, D_B)) * g
    o = jnp.concatenate([o_a, o_b], axis=-1).astype(xn.dtype)
    return o @ p['w_out_ab'][j], s_hgrn_new, s_rwkv_new, u[:, -1]


def conv_mixer(xn, s_conv, w_in, conv_k, w_out):
    L = xn.shape[1]
    gb, gc, h = jnp.split(xn @ w_in, 3, axis=-1)
    u = gc * h
    pad = jnp.concatenate([s_conv.astype(u.dtype), u], axis=1)
    y = conv_k[0] * pad[:, 0:L]
    for tap in range(1, CONV_W):
        y = y + conv_k[tap] * pad[:, tap:tap + L]
    return (gb * y) @ w_out, pad[:, L:]


def moe_swiglu(xn, router, w1, w3, w2):
    B, L, D = xn.shape
    t = xn.reshape(B * L, D)
    logits = (t @ router).astype(jnp.float32)
    top_val, top_idx = lax.top_k(logits, TOP_K)
    gate_top = jax.nn.softmax(top_val, axis=-1)
    gates = jnp.sum(jax.nn.one_hot(top_idx, N_EXPERTS, dtype=jnp.float32) * gate_top[..., None], axis=1)
    gates = gates.astype(t.dtype)
    out = jnp.zeros_like(t)
    for e in range(N_EXPERTS):
        out = out + gates[:, e:e + 1] * swiglu(t, w1[e], w3[e], w2[e])
    return out.reshape(B, L, D)


def run_group(x, st_hgrn, st_rwkv, st_shift, st_conv, p):
    lb_all = jnp.cumsum(jax.nn.softmax(p['hgrn_lb_logits'].astype(jnp.float32), axis=0), axis=0)
    new_h, new_r, new_s, new_c = [], [], [], []
    for layer in range(DEPTH):
        j = layer // 2
        xn = rms_norm(x, p['norm_mix'][layer])
        if layer % 2 == 0:
            mix, sh, sr, ss = even_mixer(xn, st_hgrn[j], st_rwkv[j], st_shift[j], lb_all[j], p, j)
            new_h.append(sh.astype(st_hgrn.dtype))
            new_r.append(sr.astype(st_rwkv.dtype))
            new_s.append(ss.astype(st_shift.dtype))
            x = x + mix
            x = x + swiglu(rms_norm(x, p['norm_ffn'][layer]), p['ffn_w1'][j], p['ffn_w3'][j], p['ffn_w2'][j])
        else:
            mix, sc = conv_mixer(xn, st_conv[j], p['conv_w_in'][j], p['conv_k'][j], p['conv_w_out'][j])
            new_c.append(sc.astype(st_conv.dtype))
            x = x + mix
            x = x + moe_swiglu(rms_norm(x, p['norm_ffn'][layer]), p['moe_router'][j],
                               p['moe_w1'][j], p['moe_w3'][j], p['moe_w2'][j])
    return rms_norm(x, p['norm_final']), jnp.stack(new_h), jnp.stack(new_r), jnp.stack(new_s), jnp.stack(new_c)


def setup_inputs(seed: int = 0) -> dict:
    key = jax.random.key(seed)
    ks = iter(jax.random.split(key, 40))

    def nrm(shape, scale):
        return scale * jax.random.normal(next(ks), shape, jnp.float32)

    def gain(shape):
        return 1.0 + 0.02 * jax.random.normal(next(ks), shape, jnp.float32)

    inp = {}
    inp['x_prompt'] = nrm((BATCH, SEQ, D_MODEL), 1.0)
    inp['x_sample'] = nrm((DEC_BATCH, DEC_SEQ, D_MODEL), 1.0)
    inp['state_hgrn'] = nrm((N_EVEN, DEC_BATCH, H_A, DK_A, DV_A), 0.5)
    inp['state_rwkv'] = nrm((N_EVEN, DEC_BATCH, H_B, HEAD_B, HEAD_B), 0.3)
    inp['state_shift'] = nrm((N_EVEN, DEC_BATCH, D_SHIFT), 1.0)
    inp['state_conv'] = nrm((N_ODD, DEC_BATCH, CONV_W - 1, D_CONV), 1.0)
    inp['norm_mix'] = gain((DEPTH, D_MODEL))
    inp['w_in_ab'] = nrm((N_EVEN, D_MODEL, D_IN_AB), D_MODEL ** -0.5)
    inp['hgrn_lb_logits'] = nrm((N_EVEN + 1, D_A), 0.1)
    inp['hgrn_norm'] = gain((N_EVEN, DV_A))
    inp['rwkv_mu'] = jax.random.uniform(next(ks), (N_EVEN, D_SHIFT), jnp.float32)
    inp['rwkv_w0'] = -1.0 + nrm((N_EVEN, D_B), 0.3)
    inp['rwkv_w2'] = nrm((N_EVEN, LORA_W, D_B), 0.5 * LORA_W ** -0.5)
    inp['rwkv_a0'] = nrm((N_EVEN, D_B), 0.1)
    inp['rwkv_a2'] = nrm((N_EVEN, LORA_A, D_B), 0.5 * LORA_A ** -0.5)
    inp['rwkv_g2'] = nrm((N_EVEN, LORA_G, D_B), LORA_G ** -0.5)
    inp['rwkv_k_k'] = 0.85 + nrm((N_EVEN, D_B), 0.05)
    inp['rwkv_k_a'] = 1.0 + nrm((N_EVEN, D_B), 0.05)
    inp['rwkv_r_k'] = nrm((N_EVEN, H_B, HEAD_B), 0.1)
    inp['rwkv_ln_w'] = gain((N_EVEN, D_B))
    inp['rwkv_ln_b'] = nrm((N_EVEN, D_B), 0.02)
    inp['w_out_ab'] = nrm((N_EVEN, D_A + D_B, D_MODEL), (D_A + D_B) ** -0.5)
    inp['norm_ffn'] = gain((DEPTH, D_MODEL))
    inp['ffn_w1'] = nrm((N_EVEN, D_MODEL, D_FF), D_MODEL ** -0.5)
    inp['ffn_w3'] = nrm((N_EVEN, D_MODEL, D_FF), D_MODEL ** -0.5)
    inp['ffn_w2'] = nrm((N_EVEN, D_FF, D_MODEL), D_FF ** -0.5)
    inp['conv_w_in'] = nrm((N_ODD, D_MODEL, 3 * D_CONV), D_MODEL ** -0.5)
    inp['conv_k'] = nrm((N_ODD, CONV_W, D_CONV), CONV_W ** -0.5)
    inp['conv_w_out'] = nrm((N_ODD, D_CONV, D_MODEL), D_CONV ** -0.5)
    inp['moe_router'] = nrm((N_ODD, D_MODEL, N_EXPERTS), D_MODEL ** -0.5)
    inp['moe_w1'] = nrm((N_ODD, N_EXPERTS, D_MODEL, D_FF_EXPERT), D_MODEL ** -0.5)
    inp['moe_w3'] = nrm((N_ODD, N_EXPERTS, D_MODEL, D_FF_EXPERT), D_MODEL ** -0.5)
    inp['moe_w2'] = nrm((N_ODD, N_EXPERTS, D_FF_EXPERT, D_MODEL), D_FF_EXPERT ** -0.5)
    inp['norm_final'] = gain((D_MODEL,))
    return inp


def reference(x_prompt, x_sample, state_hgrn, state_rwkv, state_shift, state_conv, norm_mix, w_in_ab,
              hgrn_lb_logits, hgrn_norm, rwkv_mu, rwkv_w0, rwkv_w2, rwkv_a0, rwkv_a2, rwkv_g2, rwkv_k_k,
              rwkv_k_a, rwkv_r_k, rwkv_ln_w, rwkv_ln_b, w_out_ab, norm_ffn, ffn_w1, ffn_w3, ffn_w2,
              conv_w_in, conv_k, conv_w_out, moe_router, moe_w1, moe_w3, moe_w2, norm_final):
    p = dict(norm_mix=norm_mix, w_in_ab=w_in_ab, hgrn_lb_logits=hgrn_lb_logits, hgrn_norm=hgrn_norm,
             rwkv_mu=rwkv_mu, rwkv_w0=rwkv_w0, rwkv_w2=rwkv_w2, rwkv_a0=rwkv_a0, rwkv_a2=rwkv_a2,
             rwkv_g2=rwkv_g2, rwkv_k_k=rwkv_k_k, rwkv_k_a=rwkv_k_a, rwkv_r_k=rwkv_r_k, rwkv_ln_w=rwkv_ln_w,
             rwkv_ln_b=rwkv_ln_b, w_out_ab=w_out_ab, norm_ffn=norm_ffn, ffn_w1=ffn_w1, ffn_w3=ffn_w3,
             ffn_w2=ffn_w2, conv_w_in=conv_w_in, conv_k=conv_k, conv_w_out=conv_w_out, moe_router=moe_router,
             moe_w1=moe_w1, moe_w3=moe_w3, moe_w2=moe_w2, norm_final=norm_final)
    bp = x_prompt.shape[0]
    h0 = jnp.zeros((N_EVEN, bp) + state_hgrn.shape[2:], state_hgrn.dtype)
    r0 = jnp.zeros((N_EVEN, bp) + state_rwkv.shape[2:], state_rwkv.dtype)
    s0 = jnp.zeros((N_EVEN, bp) + state_shift.shape[2:], state_shift.dtype)
    c0 = jnp.zeros((N_ODD, bp) + state_conv.shape[2:], state_conv.dtype)
    y_prompt, hgrn_p, rwkv_p, shift_p, conv_p = run_group(x_prompt, h0, r0, s0, c0, p)
    y_sample, hgrn_s, rwkv_s, shift_s, conv_s = run_group(x_sample, state_hgrn, state_rwkv, state_shift, state_conv, p)
    return (y_prompt, y_sample, hgrn_p, rwkv_p, shift_p, conv_p, hgrn_s, rwkv_s, shift_s, conv_s)
```

```python
import functools

import jax
import jax.numpy as jnp
from jax import lax
from jax.experimental import pallas as pl
from jax.experimental.pallas import tpu as pltpu

NORM_EPS = 1e-6
RWKV_GN_EPS = 64e-5
HEAD_B = 64
DK_A = 128
LANES = 128
SUBLANES = 8
VMEM_LIMIT = 56 * 1024 * 1024


def _cparams(*sem):
    return pltpu.CompilerParams(dimension_semantics=sem, vmem_limit_bytes=VMEM_LIMIT)


def _tile(n, cap, mult=SUBLANES):
    best = None
    for d in range(mult, min(n, cap) + 1, mult):
        if n % d == 0:
            best = d
    return n if best is None else best


def _rms(x, gain):
    return x * lax.rsqrt(jnp.mean(x * x, axis=-1, keepdims=True) + NORM_EPS) * gain


def _silu(x):
    return x * jax.nn.sigmoid(x)


def _split_bf16(x):
    hi = x.astype(jnp.bfloat16).astype(jnp.float32)
    return hi, x - hi


def _rms_mm_kernel(x_ref, g_ref, w_ref, o_ref, xn_ref):
    @pl.when(pl.program_id(1) == 0)
    def _():
        xn_ref[...] = _rms(x_ref[...], g_ref[...])

    o_ref[...] = jnp.dot(xn_ref[...], w_ref[...], preferred_element_type=jnp.float32)


def rms_mm(x, gain, w, tn):
    T, K = x.shape
    N = w.shape[1]
    tm = _tile(T, 1088)
    return pl.pallas_call(
        _rms_mm_kernel,
        grid=(T // tm, N // tn),
        in_specs=[pl.BlockSpec((tm, K), lambda i, j: (i, 0)),
                  pl.BlockSpec((1, K), lambda i, j: (0, 0)),
                  pl.BlockSpec((K, tn), lambda i, j: (0, j))],
        out_specs=pl.BlockSpec((tm, tn), lambda i, j: (i, j)),
        out_shape=jax.ShapeDtypeStruct((T, N), jnp.float32),
        scratch_shapes=[pltpu.VMEM((tm, K), jnp.float32)],
        compiler_params=_cparams("parallel", "arbitrary"),
        name="rms_mm",
    )(x, gain.reshape(1, K), w)


def _mm_res_kernel(a_ref, w_ref, r_ref, o_ref):
    o_ref[...] = r_ref[...] + jnp.dot(a_ref[...], w_ref[...], preferred_element_type=jnp.float32)


def mm_res(a, w, res, tn=512):
    T, K = a.shape
    N = w.shape[1]
    tm = _tile(T, 1088)
    tn = _tile(N, tn, LANES)
    return pl.pallas_call(
        _mm_res_kernel,
        grid=(T // tm, N // tn),
        in_specs=[pl.BlockSpec((tm, K), lambda i, j: (i, 0)),
                  pl.BlockSpec((K, tn), lambda i, j: (0, j)),
                  pl.BlockSpec((tm, tn), lambda i, j: (i, j))],
        out_specs=pl.BlockSpec((tm, tn), lambda i, j: (i, j)),
        out_shape=jax.ShapeDtypeStruct((T, N), jnp.float32),
        compiler_params=_cparams("parallel", "parallel"),
        name="mm_res",
    )(a, w, res)


def _rms_only_kernel(x_ref, g_ref, o_ref):
    o_ref[...] = _rms(x_ref[...], g_ref[...])


def rms_only(x, gain):
    T, K = x.shape
    tm = _tile(T, 544)
    return pl.pallas_call(
        _rms_only_kernel,
        grid=(T // tm,),
        in_specs=[pl.BlockSpec((tm, K), lambda i: (i, 0)),
                  pl.BlockSpec((1, K), lambda i: (0, 0))],
        out_specs=pl.BlockSpec((tm, K), lambda i: (i, 0)),
        out_shape=jax.ShapeDtypeStruct((T, K), jnp.float32),
        compiler_params=_cparams("parallel"),
        name="rms_final",
    )(x, gain.reshape(1, K))


def _ffn_kernel(x_ref, g_ref, w1_ref, w3_ref, w2_ref, o_ref, xn_ref, acc_ref):
    j = pl.program_id(1)

    @pl.when(j == 0)
    def _():
        xn_ref[...] = _rms(x_ref[...], g_ref[...])
        acc_ref[...] = jnp.zeros_like(acc_ref)

    xn = xn_ref[...]
    h1 = jnp.dot(xn, w1_ref[...], preferred_element_type=jnp.float32)
    h3 = jnp.dot(xn, w3_ref[...], preferred_element_type=jnp.float32)
    acc_ref[...] += jnp.dot(_silu(h1) * h3, w2_ref[...], preferred_element_type=jnp.float32)

    @pl.when(j == pl.num_programs(1) - 1)
    def _():
        o_ref[...] = x_ref[...] + acc_ref[...]


def ffn(x, gain, w1, w3, w2, tf=256):
    T, D = x.shape
    F = w1.shape[1]
    tm = _tile(T, 544)
    return pl.pallas_call(
        _ffn_kernel,
        grid=(T // tm, F // tf),
        in_specs=[pl.BlockSpec((tm, D), lambda i, j: (i, 0)),
                  pl.BlockSpec((1, D), lambda i, j: (0, 0)),
                  pl.BlockSpec((D, tf), lambda i, j: (0, j)),
                  pl.BlockSpec((D, tf), lambda i, j: (0, j)),
                  pl.BlockSpec((tf, D), lambda i, j: (j, 0))],
        out_specs=pl.BlockSpec((tm, D), lambda i, j: (i, 0)),
        out_shape=jax.ShapeDtypeStruct((T, D), jnp.float32),
        scratch_shapes=[pltpu.VMEM((tm, D), jnp.float32), pltpu.VMEM((tm, D), jnp.float32)],
        compiler_params=_cparams("parallel", "arbitrary"),
        name="ffn",
    )(x, gain.reshape(1, D), w1, w3, w2)


def _moe_kernel(x_ref, g_ref, gate_ref, w1_ref, w3_ref, w2_ref, o_ref, xn_ref, acc_ref):
    e = pl.program_id(1)
    j = pl.program_id(2)

    @pl.when((e == 0) & (j == 0))
    def _():
        xn_ref[...] = _rms(x_ref[...], g_ref[...])
        acc_ref[...] = jnp.zeros_like(acc_ref)

    xn = xn_ref[...]
    h1 = jnp.dot(xn, w1_ref[...], preferred_element_type=jnp.float32)
    h3 = jnp.dot(xn, w3_ref[...], preferred_element_type=jnp.float32)
    h = _silu(h1) * h3 * gate_ref[...]
    acc_ref[...] += jnp.dot(h, w2_ref[...], preferred_element_type=jnp.float32)

    @pl.when((e == pl.num_programs(1) - 1) & (j == pl.num_programs(2) - 1))
    def _():
        o_ref[...] = x_ref[...] + acc_ref[...]


def moe_dense(x, gain, gates, w1, w3, w2, tf=256):
    T, D = x.shape
    E, _, F = w1.shape
    tm = _tile(T, 544)
    return pl.pallas_call(
        _moe_kernel,
        grid=(T // tm, E, F // tf),
        in_specs=[pl.BlockSpec((tm, D), lambda i, e, j: (i, 0)),
                  pl.BlockSpec((1, D), lambda i, e, j: (0, 0)),
                  pl.BlockSpec((None, tm, 1), lambda i, e, j: (e, i, 0)),
                  pl.BlockSpec((None, D, tf), lambda i, e, j: (e, 0, j)),
                  pl.BlockSpec((None, D, tf), lambda i, e, j: (e, 0, j)),
                  pl.BlockSpec((None, tf, D), lambda i, e, j: (e, j, 0))],
        out_specs=pl.BlockSpec((tm, D), lambda i, e, j: (i, 0)),
        out_shape=jax.ShapeDtypeStruct((T, D), jnp.float32),
        scratch_shapes=[pltpu.VMEM((tm, D), jnp.float32), pltpu.VMEM((tm, D), jnp.float32)],
        compiler_params=_cparams("parallel", "arbitrary", "arbitrary"),
        name="moe",
    )(x, gain.reshape(1, D), gates, w1, w3, w2)


def _router_kernel(x_ref, g_ref, rt_ref, gate_ref, *, n_exp):
    xn = _rms(x_ref[...], g_ref[...])
    logit = [jnp.sum(xn * rt_ref[e:e + 1, :], axis=-1, keepdims=True) for e in range(n_exp)]
    neg = jnp.float32(-jnp.inf)

    def top(vals):
        m = vals[0]
        for v in vals[1:]:
            m = jnp.maximum(m, v)
        idx = jnp.full(m.shape, n_exp, jnp.int32)
        for e in reversed(range(n_exp)):
            idx = jnp.where(vals[e] == m, e, idx)
        return m, idx

    m1, i1 = top(logit)
    rest = [jnp.where(i1 == e, neg, logit[e]) for e in range(n_exp)]
    m2, i2 = top(rest)
    d = jnp.exp(m2 - m1)
    g1 = 1.0 / (1.0 + d)
    g2 = d / (1.0 + d)
    for e in range(n_exp):
        gate_ref[e] = jnp.where(i1 == e, g1, 0.0) + jnp.where(i2 == e, g2, 0.0)


def router_gates(x, gain, router):
    T, D = x.shape
    E = router.shape[1]
    tm = _tile(T, 544)
    return pl.pallas_call(
        functools.partial(_router_kernel, n_exp=E),
        grid=(T // tm,),
        in_specs=[pl.BlockSpec((tm, D), lambda i: (i, 0)),
                  pl.BlockSpec((1, D), lambda i: (0, 0)),
                  pl.BlockSpec((E, D), lambda i: (0, 0))],
        out_specs=pl.BlockSpec((E, tm, 1), lambda i: (0, i, 0)),
        out_shape=jax.ShapeDtypeStruct((E, T, 1), jnp.float32),
        compiler_params=_cparams("parallel"),
        name="router",
    )(x, gain.reshape(1, D), router.T)


def _cumsum_rows(x):
    n = x.shape[0]
    row = lax.broadcasted_iota(jnp.int32, x.shape, 0)
    s = 1
    while s < n:
        x = x + jnp.where(row >= s, pltpu.roll(x, s, 0), 0.0)
        s *= 2
    return x


def _hgrn_kernel(q_ref, f_ref, i_ref, lbl_ref, s0_ref, o_ref, so_ref, s_scr, *, chunk, layer_j, valid_len):
    c = pl.program_id(2)
    tl = q_ref.shape[-2]
    dk = q_ref.shape[-1]

    @pl.when(c == 0)
    def _():
        s_scr[...] = s0_ref[...]

    lg = lbl_ref[...]
    ex = jnp.exp(lg - jnp.max(lg, axis=0, keepdims=True))
    lb = jnp.sum(ex[0:layer_j + 1, :], axis=0, keepdims=True) / jnp.sum(ex, axis=0, keepdims=True)

    pad = LANES - chunk
    zpad = jnp.zeros((pad, dk), jnp.float32)
    row = lax.broadcasted_iota(jnp.int32, (chunk, LANES), 0)
    col = lax.broadcasted_iota(jnp.int32, (chunk, LANES), 1)
    causal = col <= row
    ones_cd = jnp.ones((chunk, dk), jnp.float32)
    tn_dims = (((0,), (0,)), ((), ()))
    nt_dims = (((1,), (1,)), ((), ()))

    for cc in range(tl // chunk):
        sl = slice(cc * chunk, (cc + 1) * chunk)
        f = lb + (1.0 - lb) * jax.nn.sigmoid(f_ref[sl, :])
        q = _silu(q_ref[sl, :])
        v = i_ref[sl, :]
        if valid_len is not None:
            t_idx = c * tl + cc * chunk + lax.broadcasted_iota(jnp.int32, (chunk, dk), 0)
            ok = t_idx < valid_len
            f = jnp.where(ok, f, 1.0)
            q = jnp.where(ok, q, 0.0)
        g = jnp.log(f)
        k = 1.0 - f
        b = _cumsum_rows(g)
        mid = chunk // 2 - 1
        bm = b[mid:mid + 1, :]
        bl = b[chunk - 1:chunk, :]
        qe = q * jnp.exp(b - bm)
        ke = jnp.concatenate([k * jnp.exp(bm - b), zpad], axis=0)
        vp = jnp.concatenate([v, zpad], axis=0)
        a = lax.dot_general(qe, ke, nt_dims, preferred_element_type=jnp.float32)
        a = jnp.where(causal, a, 0.0)
        s = s_scr[...]
        o = jnp.dot(a, vp, preferred_element_type=jnp.float32)
        o = o + jnp.dot(q * jnp.exp(b), s, preferred_element_type=jnp.float32)
        o_ref[sl, :] = o
        kd = jnp.concatenate([k * jnp.exp(bl - b), zpad], axis=0)
        ghi, glo = _split_bf16(g)
        blc = (lax.dot_general(ghi, ones_cd, tn_dims, preferred_element_type=jnp.float32)
               + lax.dot_general(glo, ones_cd, tn_dims, preferred_element_type=jnp.float32))
        s_scr[...] = jnp.exp(blc) * s + lax.dot_general(kd, vp, tn_dims, preferred_element_type=jnp.float32)

    @pl.when(c == pl.num_programs(2) - 1)
    def _():
        so_ref[...] = s_scr[...]


def hgrn_scan(pa, row0, B, L, tl, chunk, lb_logits, s0, layer_j, valid_len=None):
    H = s0.shape[1]
    dk = s0.shape[2]
    nt = L // tl
    rb0 = row0 // tl
    kern = functools.partial(_hgrn_kernel, chunk=chunk, layer_j=layer_j, valid_len=valid_len)

    def col(off):
        return pl.BlockSpec((tl, dk), lambda b, h, c: (rb0 + b * nt + c, off * H + h))

    return pl.pallas_call(
        kern,
        grid=(B, H, nt),
        in_specs=[col(0), col(1), col(2),
                  pl.BlockSpec((lb_logits.shape[0], dk), lambda b, h, c: (0, h)),
                  pl.BlockSpec((None, None, dk, dk), lambda b, h, c: (b, h, 0, 0))],
        out_specs=[pl.BlockSpec((tl, dk), lambda b, h, c: (b * nt + c, h)),
                   pl.BlockSpec((None, None, dk, dk), lambda b, h, c: (b, h, 0, 0))],
        out_shape=[jax.ShapeDtypeStruct((B * L, H * dk), jnp.float32),
                   jax.ShapeDtypeStruct(s0.shape, jnp.float32)],
        scratch_shapes=[pltpu.VMEM((dk, dk), jnp.float32)],
        compiler_params=_cparams("parallel", "parallel", "arbitrary"),
        name="hgrn_scan",
    )(pa, pa, pa, lb_logits, s0)


def _seg_sum(x, bones):
    outs = []
    for j in range(x.shape[1] // LANES):
        hi, lo = _split_bf16(x[:, j * LANES:(j + 1) * LANES])
        outs.append(jnp.dot(hi, bones, preferred_element_type=jnp.float32)
                    + jnp.dot(lo, bones, preferred_element_type=jnp.float32))
    return jnp.concatenate(outs, axis=1)


def _rwkv_prep_kernel(u_ref, up_ref, mu_ref, w0_ref, a0_ref, kk_ref, ka_ref, rk_ref,
                      w2_ref, a2_ref, g2_ref, bones_ref,
                      kk_o, wr_o, w_o, kka_o, km_o, vhi_o, vlo_o, g_o, c1_o, c2_o, bon_o, *, d_b):
    u = u_ref[...]
    us = u + (up_ref[...] - u) * mu_ref[...]
    r = us[:, 0:d_b]
    k = us[:, d_b:2 * d_b]
    v = us[:, 2 * d_b:3 * d_b]
    wa = us[:, 3 * d_b:3 * d_b + LANES]
    gd = us[:, 3 * d_b + LANES:3 * d_b + 2 * LANES]
    bones = bones_ref[...]
    z = w0_ref[...] + jnp.dot(jnp.tanh(wa), w2_ref[...], preferred_element_type=jnp.float32)
    nz = -z
    softplus = jnp.maximum(nz, 0.0) + jnp.log(1.0 + jnp.exp(-jnp.abs(nz)))
    w_log = -softplus - 0.5
    decay = jnp.exp(-jnp.exp(w_log))
    a = jax.nn.sigmoid(a0_ref[...] + jnp.dot(wa, a2_ref[...], preferred_element_type=jnp.float32))
    g = jnp.dot(jax.nn.sigmoid(gd), g2_ref[...], preferred_element_type=jnp.float32)
    kk = k * kk_ref[...]
    kk = kk / jnp.maximum(jnp.sqrt(_seg_sum(kk * kk, bones)), 1e-12)
    kmod = k * (1.0 + (a - 1.0) * ka_ref[...])
    kka = kk * a
    vhi, vlo = _split_bf16(v)
    kk_o[...] = kk
    wr_o[...] = decay * r
    w_o[...] = decay
    kka_o[...] = kka
    km_o[...] = kmod
    vhi_o[...] = vhi
    vlo_o[...] = vlo
    g_o[...] = g
    c1_o[...] = _seg_sum(kka * r, bones)
    c2_o[...] = _seg_sum(kmod * r, bones)
    bon_o[...] = _seg_sum(r * kmod * rk_ref[...], bones)


def rwkv_prep(u, u_prev, mu, w0, a0, k_k, k_a, r_k, w2p, a2p, g2, bones, d_b):
    T, DU = u.shape
    tm = _tile(T, 272)
    row = lambda n: pl.BlockSpec((1, n), lambda i: (0, 0))
    full = lambda a: pl.BlockSpec(a.shape, lambda i: (0, 0))
    tok = lambda n: pl.BlockSpec((tm, n), lambda i: (i, 0))
    return pl.pallas_call(
        functools.partial(_rwkv_prep_kernel, d_b=d_b),
        grid=(T // tm,),
        in_specs=[tok(DU), tok(DU), row(DU), row(d_b), row(d_b), row(d_b), row(d_b), row(d_b),
                  full(w2p), full(a2p), full(g2), full(bones)],
        out_specs=[tok(d_b)] * 11,
        out_shape=[jax.ShapeDtypeStruct((T, d_b), jnp.float32)] * 11,
        compiler_params=_cparams("parallel"),
        name="rwkv_prep",
    )(u, u_prev, mu.reshape(1, DU), w0.reshape(1, d_b), a0.reshape(1, d_b), k_k.reshape(1, d_b),
      k_a.reshape(1, d_b), r_k.reshape(1, d_b), w2p, a2p, g2, bones)


def _rwkv_scan_kernel(kk_ref, wr_ref, w_ref, kka_ref, k_ref, vhi_ref, vlo_ref, s0_ref, rhs_ref, md_ref,
                      r2_o, sa_o, so_ref, s_scr, *, n_pair):
    tb = kk_ref.shape[-2]
    n = HEAD_B

    @pl.when(pl.program_id(1) == 0)
    def _():
        s_scr[...] = s0_ref[...]

    r2_o[...] = jnp.zeros_like(r2_o)
    sa_o[...] = jnp.zeros_like(sa_o)
    rhs = rhs_ref[...]
    md = md_ref[...]
    lane = lax.broadcasted_iota(jnp.int32, (n, LANES), 1) & (n - 1)

    def token_step(row, sel):
        parts = []
        for p in range(n_pair):
            sl = slice(p * LANES, (p + 1) * LANES)
            s = s_scr[p]
            p1 = s * row(kk_ref, sl)
            p2 = s * row(wr_ref, sl)
            dh = md * row(vhi_ref, sl)
            dl = md * row(vlo_ref, sl)
            parts.append(jnp.concatenate([p1, dh], axis=1))
            parts.append(jnp.concatenate([p2, dl], axis=1))
        out = jnp.dot(jnp.concatenate(parts, axis=0), rhs, preferred_element_type=jnp.float32)
        for p in range(n_pair):
            sl = slice(p * LANES, (p + 1) * LANES)
            top = out[2 * p * n:(2 * p + 1) * n]
            bot = out[(2 * p + 1) * n:(2 * p + 2) * n]
            sa_b = top[:, 0:LANES]
            r2_b = bot[:, 0:LANES]
            v_b = top[:, LANES:2 * LANES] + bot[:, LANES:2 * LANES]
            s = s_scr[p]
            s_scr[p] = s * row(w_ref, sl) - sa_b * row(kka_ref, sl) + v_b * row(k_ref, sl)
            r2_o[p] = jnp.where(sel, r2_b, r2_o[p])
            sa_o[p] = jnp.where(sel, sa_b, sa_o[p])

    if tb % SUBLANES == 0:
        def group(gi, carry):
            base = pl.multiple_of(gi * SUBLANES, SUBLANES)
            for jj in range(SUBLANES):
                token_step(lambda ref, sl, jj=jj: ref[pl.ds(base, SUBLANES), sl][jj:jj + 1, :],
                           lane == gi * SUBLANES + jj)
            return carry

        lax.fori_loop(0, tb // SUBLANES, group, 0)
    else:
        for tt in range(tb):
            token_step(lambda ref, sl, tt=tt: ref[tt:tt + 1, sl], lane == tt)

    @pl.when(pl.program_id(1) == pl.num_programs(1) - 1)
    def _():
        so_ref[...] = s_scr[...]


def rwkv_scan(vecs, B, L, tb, s0_pair, rhs, md):
    n_pair = s0_pair.shape[1]
    d_b = n_pair * LANES
    nt = L // tb
    tok = pl.BlockSpec((None, tb, d_b), lambda b, t: (b, t, 0))
    st = pl.BlockSpec((None, n_pair, HEAD_B, LANES), lambda b, t: (b, 0, 0, 0))
    colo = pl.BlockSpec((None, None, n_pair, HEAD_B, LANES), lambda b, t: (b, t, 0, 0, 0))
    col_shape = jax.ShapeDtypeStruct((B, nt, n_pair, HEAD_B, LANES), jnp.float32)
    return pl.pallas_call(
        functools.partial(_rwkv_scan_kernel, n_pair=n_pair),
        grid=(B, nt),
        in_specs=[tok] * 7 + [st, pl.BlockSpec(rhs.shape, lambda b, t: (0, 0)),
                              pl.BlockSpec(md.shape, lambda b, t: (0, 0))],
        out_specs=[colo, colo, st],
        out_shape=[col_shape, col_shape, jax.ShapeDtypeStruct(s0_pair.shape, jnp.float32)],
        scratch_shapes=[pltpu.VMEM((n_pair, HEAD_B, LANES), jnp.float32)],
        compiler_params=_cparams("parallel", "arbitrary"),
        name="rwkv_scan",
    )(*vecs, s0_pair, rhs, md)


def _mix_out_kernel(oa_ref, go_ref, hg_ref, r2_ref, sa_ref, c1_ref, c2_ref, vhi_ref, vlo_ref, g_ref, bon_ref,
                    lnw_ref, lnb_ref, bones_ref, o_ref, *, d_a):
    oa = oa_ref[...]
    outs = []
    for h in range(d_a // DK_A):
        x = oa[:, h * DK_A:(h + 1) * DK_A]
        outs.append(x * lax.rsqrt(jnp.mean(x * x, axis=-1, keepdims=True) + NORM_EPS))
    o_a = jnp.concatenate(outs, axis=1) * hg_ref[...] * _silu(go_ref[...])
    bones = bones_ref[...]
    v = vhi_ref[...] + vlo_ref[...]
    y = r2_ref[...] - sa_ref[...] * c1_ref[...] + v * c2_ref[...]
    inv_n = 1.0 / HEAD_B
    mean = _seg_sum(y, bones) * inv_n
    d = y - mean
    var = _seg_sum(d * d, bones) * inv_n
    yn = d * lax.rsqrt(var + RWKV_GN_EPS) * lnw_ref[...] + lnb_ref[...]
    o_b = (yn + bon_ref[...] * v) * g_ref[...]
    o_ref[:, 0:d_a] = o_a
    o_ref[:, d_a:] = o_b


def mix_out(oa, pa, hg_row, r2, sa, c1, c2, vhi, vlo, g, bon, ln_w, ln_b, bones):
    T, d_a = oa.shape
    d_b = r2.shape[1]
    tm = _tile(T, 272)
    tok = lambda n: pl.BlockSpec((tm, n), lambda i: (i, 0))
    row = lambda n: pl.BlockSpec((1, n), lambda i: (0, 0))
    return pl.pallas_call(
        functools.partial(_mix_out_kernel, d_a=d_a),
        grid=(T // tm,),
        in_specs=[tok(d_a), pl.BlockSpec((tm, d_a), lambda i: (i, 3)), row(d_a)]
                 + [tok(d_b)] * 8 + [row(d_b), row(d_b), pl.BlockSpec(bones.shape, lambda i: (0, 0))],
        out_specs=tok(d_a + d_b),
        out_shape=jax.ShapeDtypeStruct((T, d_a + d_b), jnp.float32),
        compiler_params=_cparams("parallel"),
        name="mix_out",
    )(oa, pa, hg_row, r2, sa, c1, c2, vhi, vlo, g, bon, ln_w.reshape(1, d_b), ln_b.reshape(1, d_b), bones)


def _conv_u_kernel(gc_ref, h_ref, u_ref):
    u_ref[...] = gc_ref[...] * h_ref[...]


def conv_u(cp, d):
    T = cp.shape[0]
    tm = _tile(T, 544)
    return pl.pallas_call(
        _conv_u_kernel,
        grid=(T // tm,),
        in_specs=[pl.BlockSpec((tm, d), lambda i: (i, 1)), pl.BlockSpec((tm, d), lambda i: (i, 2))],
        out_specs=pl.BlockSpec((tm, d), lambda i: (i, 0)),
        out_shape=jax.ShapeDtypeStruct((T, d), jnp.float32),
        compiler_params=_cparams("parallel"),
        name="conv_u",
    )(cp, cp)


def _conv_mm_kernel(gb_ref, u_ref, u1_ref, u2_ref, ck_ref, w_ref, r_ref, o_ref, z_ref):
    @pl.when(pl.program_id(1) == 0)
    def _():
        y = ck_ref[0:1, :] * u2_ref[...]
        y = y + ck_ref[1:2, :] * u1_ref[...]
        y = y + ck_ref[2:3, :] * u_ref[...]
        z_ref[...] = gb_ref[...] * y

    o_ref[...] = r_ref[...] + jnp.dot(z_ref[...], w_ref[...], preferred_element_type=jnp.float32)


def conv_mm(cp, u, u1, u2, conv_k, w, res, tn=512):
    T, d = u.shape
    N = w.shape[1]
    assert conv_k.shape[0] == 3
    tm = _tile(T, 272)
    tn = _tile(N, tn, LANES)
    tok = pl.BlockSpec((tm, d), lambda i, j: (i, 0))
    return pl.pallas_call(
        _conv_mm_kernel,
        grid=(T // tm, N // tn),
        in_specs=[tok, tok, tok, tok,
                  pl.BlockSpec(conv_k.shape, lambda i, j: (0, 0)),
                  pl.BlockSpec((d, tn), lambda i, j: (0, j)),
                  pl.BlockSpec((tm, tn), lambda i, j: (i, j))],
        out_specs=pl.BlockSpec((tm, tn), lambda i, j: (i, j)),
        out_shape=jax.ShapeDtypeStruct((T, N), jnp.float32),
        scratch_shapes=[pltpu.VMEM((tm, d), jnp.float32)],
        compiler_params=_cparams("parallel", "arbitrary"),
        name="conv_mm",
    )(cp, u, u1, u2, conv_k, w, res)


def _shift_rows(x, groups, states, k):
    outs = []
    row = 0
    for (B, L), st in zip(groups, states):
        xg = x[row:row + B * L].reshape(B, L, -1)
        outs.append(jnp.concatenate([st[:, st.shape[1] - k:], xg[:, :L - k]], axis=1).reshape(B * L, -1))
        row += B * L
    return jnp.concatenate(outs, axis=0)


def _last_rows(x, groups, k):
    outs = []
    row = 0
    for B, L in groups:
        outs.append(x[row:row + B * L].reshape(B, L, -1)[:, L - k:])
        row += B * L
    return outs


def _pair_state(s):
    B, H, n, _ = s.shape
    return s.reshape(B, H // 2, 2, n, n).transpose(0, 1, 3, 2, 4).reshape(B, H // 2, n, 2 * n)


def _unpair_state(s):
    B, hp, n, _ = s.shape
    return s.reshape(B, hp, n, 2, n).transpose(0, 1, 3, 2, 4).reshape(B, 2 * hp, n, n)


def _cols_to_rows(c, tb):
    B, nt, n_pair, n, _ = c.shape
    c = c.reshape(B, nt, n_pair, n, 2, n)[..., :tb]
    return c.transpose(0, 1, 5, 2, 4, 3).reshape(B * nt * tb, n_pair * 2 * n)


def kernel(x_prompt, x_sample, state_hgrn, state_rwkv, state_shift, state_conv, norm_mix, w_in_ab,
           hgrn_lb_logits, hgrn_norm, rwkv_mu, rwkv_w0, rwkv_w2, rwkv_a0, rwkv_a2, rwkv_g2, rwkv_k_k,
           rwkv_k_a, rwkv_r_k, rwkv_ln_w, rwkv_ln_b, w_out_ab, norm_ffn, ffn_w1, ffn_w3, ffn_w2,
           conv_w_in, conv_k, conv_w_out, moe_router, moe_w1, moe_w3, moe_w2, norm_final):
    f32 = jnp.float32
    Bp, Lp, D = x_prompt.shape
    Bs, Ls, _ = x_sample.shape
    Tp, Ts = Bp * Lp, Bs * Ls
    groups = ((Bp, Lp), (Bs, Ls))
    n_even = w_in_ab.shape[0]
    n_odd = conv_w_in.shape[0]
    depth = n_even + n_odd
    H_A, dk = state_hgrn.shape[2], state_hgrn.shape[3]
    d_a = H_A * dk
    H_B, hb = state_rwkv.shape[2], state_rwkv.shape[3]
    d_b = H_B * hb
    d_shift = state_shift.shape[-1]
    lora_w = rwkv_w2.shape[1]
    lora_a = rwkv_a2.shape[1]
    assert hb == HEAD_B and dk == DK_A and lora_w + lora_a == LANES and H_B % 2 == 0
    assert w_in_ab.shape[2] == 4 * d_a + d_shift

    seg = jnp.arange(LANES) // HEAD_B
    bones = (seg[:, None] == seg[None, :]).astype(f32)
    seg2 = jnp.arange(2 * LANES) // HEAD_B
    rhs_scan = (seg2[:, None] == seg2[None, :]).astype(f32)
    md = (jnp.arange(LANES)[None, :] % HEAD_B == jnp.arange(HEAD_B)[:, None]).astype(f32)

    x = jnp.concatenate([x_prompt.reshape(Tp, D), x_sample.reshape(Ts, D)], axis=0)
    new_h, new_r, new_s, new_c = ([], []), ([], []), ([], []), ([], [])

    for layer in range(depth):
        j = layer // 2
        if layer % 2 == 0:
            w_in = w_in_ab[j]
            pa = rms_mm(x, norm_mix[layer], w_in[:, :4 * d_a], tn=512)
            u = rms_mm(x, norm_mix[layer], w_in[:, 4 * d_a:], tn=_tile(d_shift, 512, LANES))
            zeros_h = jnp.zeros((Bp,) + state_hgrn.shape[2:], f32)
            tl = _tile(Lp, 256)
            oa_p, sh_p = hgrn_scan(pa, 0, Bp, Lp, tl, min(32, tl), hgrn_lb_logits, zeros_h, j)
            lpad = -(-Ls // SUBLANES) * SUBLANES
            pa_s = jnp.pad(pa[Tp:, :3 * d_a].reshape(Bs, Ls, 3 * d_a), ((0, 0), (0, lpad - Ls), (0, 0)))
            oa_s, sh_s = hgrn_scan(pa_s.reshape(Bs * lpad, 3 * d_a), 0, Bs, lpad, lpad, lpad, hgrn_lb_logits,
                                   state_hgrn[j], j, valid_len=Ls)
            oa = jnp.concatenate([oa_p, oa_s.reshape(Bs, lpad, d_a)[:, :Ls].reshape(Ts, d_a)], axis=0)
            new_h[0].append(sh_p)
            new_h[1].append(sh_s)
            zeros_s = jnp.zeros((Bp, 1, d_shift), f32)
            u_prev = _shift_rows(u, groups, (zeros_s, state_shift[j][:, None, :]), 1)
            w2p = jnp.concatenate([rwkv_w2[j], jnp.zeros((lora_a, d_b), f32)], axis=0)
            a2p = jnp.concatenate([jnp.zeros((lora_w, d_b), f32), rwkv_a2[j]], axis=0)
            prep = rwkv_prep(u, u_prev, rwkv_mu[j], rwkv_w0[j], rwkv_a0[j], rwkv_k_k[j], rwkv_k_a[j],
                             rwkv_r_k[j].reshape(d_b), w2p, a2p, rwkv_g2[j], bones, d_b)
            kk, wr, wdec, kka, kmod, vhi, vlo, gg, c1, c2, bon = prep
            scan_in = (kk, wr, wdec, kka, kmod, vhi, vlo)
            tb_p = _tile(Lp, HEAD_B)
            zeros_r = jnp.zeros((Bp, H_B // 2, hb, 2 * hb), f32)
            r2_p, sa_p, sr_p = rwkv_scan([a[:Tp].reshape(Bp, Lp, d_b) for a in scan_in], Bp, Lp, tb_p,
                                         zeros_r, rhs_scan, md)
            r2_s, sa_s, sr_s = rwkv_scan([a[Tp:].reshape(Bs, Ls, d_b) for a in scan_in], Bs, Ls, Ls,
                                         _pair_state(state_rwkv[j]), rhs_scan, md)
            r2 = jnp.concatenate([_cols_to_rows(r2_p, tb_p), _cols_to_rows(r2_s, Ls)], axis=0)
            sa = jnp.concatenate([_cols_to_rows(sa_p, tb_p), _cols_to_rows(sa_s, Ls)], axis=0)
            new_r[0].append(_unpair_state(sr_p))
            new_r[1].append(_unpair_state(sr_s))
            last_u = _last_rows(u, groups, 1)
            new_s[0].append(last_u[0][:, 0])
            new_s[1].append(last_u[1][:, 0])
            hg_row = jnp.tile(hgrn_norm[j], H_A).reshape(1, d_a)
            o = mix_out(oa, pa, hg_row, r2, sa, c1, c2, vhi, vlo, gg, bon, rwkv_ln_w[j], rwkv_ln_b[j], bones)
            x = mm_res(o, w_out_ab[j], x)
            x = ffn(x, norm_ffn[layer], ffn_w1[j], ffn_w3[j], ffn_w2[j])
        else:
            d_c = state_conv.shape[-1]
            cw = state_conv.shape[2]
            cp = rms_mm(x, norm_mix[layer], conv_w_in[j], tn=512)
            uc = conv_u(cp, d_c)
            conv_states = (jnp.zeros((Bp, cw, d_c), f32), state_conv[j])
            u1 = _shift_rows(uc, groups, conv_states, 1)
            u2 = _shift_rows(uc, groups, conv_states, 2)
            last_c = _last_rows(uc, groups, cw)
            new_c[0].append(last_c[0])
            new_c[1].append(last_c[1])
            x = conv_mm(cp, uc, u1, u2, conv_k[j], conv_w_out[j], x)
            gates = router_gates(x, norm_ffn[layer], moe_router[j])
            x = moe_dense(x, norm_ffn[layer], gates, moe_w1[j], moe_w3[j], moe_w2[j])

    y = rms_only(x, norm_final)
    y_prompt = y[:Tp].reshape(Bp, Lp, D)
    y_sample = y[Tp:].reshape(Bs, Ls, D)
    outs = [y_prompt, y_sample]
    for g in (0, 1):
        outs += [jnp.stack(new_h[g]), jnp.stack(new_r[g]), jnp.stack(new_s[g]), jnp.stack(new_c[g])]
    return tuple(outs)
```

```python
import functools

import jax
import jax.numpy as jnp
from jax import lax
from jax.experimental import pallas as pl
from jax.experimental.pallas import tpu as pltpu

NORM_EPS = 1e-6
RWKV_GN_EPS = 64e-5
HEAD_B = 64
DK_A = 128
LANES = 128
SUBLANES = 8
VMEM_LIMIT = 56 * 1024 * 1024


def _cparams(*sem):
    return pltpu.CompilerParams(dimension_semantics=sem, vmem_limit_bytes=VMEM_LIMIT)


def _tile(n, cap, mult=SUBLANES):
    best = None
    for d in range(mult, min(n, cap) + 1, mult):
        if n % d == 0:
            best = d
    return n if best is None else best


def _rms(x, gain):
    return x * lax.rsqrt(jnp.mean(x * x, axis=-1, keepdims=True) + NORM_EPS) * gain


def _silu(x):
    return x * jax.nn.sigmoid(x)


def _split_bf16(x):
    hi = x.astype(jnp.bfloat16).astype(jnp.float32)
    return hi, x - hi


def _rms_mm_kernel(x_ref, g_ref, w_ref, o_ref, xn_ref):
    @pl.when(pl.program_id(1) == 0)
    def _():
        xn_ref[...] = _rms(x_ref[...], g_ref[...])

    o_ref[...] = jnp.dot(xn_ref[...], w_ref[...], preferred_element_type=jnp.float32)


def rms_mm(x, gain, w, tn):
    T, K = x.shape
    N = w.shape[1]
    tm = _tile(T, 1088)
    return pl.pallas_call(
        _rms_mm_kernel,
        grid=(T // tm, N // tn),
        in_specs=[pl.BlockSpec((tm, K), lambda i, j: (i, 0)),
                  pl.BlockSpec((1, K), lambda i, j: (0, 0)),
                  pl.BlockSpec((K, tn), lambda i, j: (0, j))],
        out_specs=pl.BlockSpec((tm, tn), lambda i, j: (i, j)),
        out_shape=jax.ShapeDtypeStruct((T, N), jnp.float32),
        scratch_shapes=[pltpu.VMEM((tm, K), jnp.float32)],
        compiler_params=_cparams("parallel", "arbitrary"),
        name="rms_mm",
    )(x, gain.reshape(1, K), w)


def _mm_res_kernel(a_ref, w_ref, r_ref, o_ref):
    o_ref[...] = r_ref[...] + jnp.dot(a_ref[...], w_ref[...], preferred_element_type=jnp.float32)


def mm_res(a, w, res, tn=512):
    T, K = a.shape
    N = w.shape[1]
    tm = _tile(T, 1088)
    tn = _tile(N, tn, LANES)
    return pl.pallas_call(
        _mm_res_kernel,
        grid=(T // tm, N // tn),
        in_specs=[pl.BlockSpec((tm, K), lambda i, j: (i, 0)),
                  pl.BlockSpec((K, tn), lambda i, j: (0, j)),
                  pl.BlockSpec((tm, tn), lambda i, j: (i, j))],
        out_specs=pl.BlockSpec((tm, tn), lambda i, j: (i, j)),
        out_shape=jax.ShapeDtypeStruct((T, N), jnp.float32),
        compiler_params=_cparams("parallel", "parallel"),
        name="mm_res",
    )(a, w, res)


def _rms_only_kernel(x_ref, g_ref, o_ref):
    o_ref[...] = _rms(x_ref[...], g_ref[...])


def rms_only(x, gain):
    T, K = x.shape
    tm = _tile(T, 544)
    return pl.pallas_call(
        _rms_only_kernel,
        grid=(T // tm,),
        in_specs=[pl.BlockSpec((tm, K), lambda i: (i, 0)),
                  pl.BlockSpec((1, K), lambda i: (0, 0))],
        out_specs=pl.BlockSpec((tm, K), lambda i: (i, 0)),
        out_shape=jax.ShapeDtypeStruct((T, K), jnp.float32),
        compiler_params=_cparams("parallel"),
        name="rms_final",
    )(x, gain.reshape(1, K))


def _ffn_kernel(x_ref, g_ref, w1_ref, w3_ref, w2_ref, o_ref, xn_ref, acc_ref):
    j = pl.program_id(1)

    @pl.when(j == 0)
    def _():
        xn_ref[...] = _rms(x_ref[...], g_ref[...])
        acc_ref[...] = jnp.zeros_like(acc_ref)

    xn = xn_ref[...]
    h1 = jnp.dot(xn, w1_ref[...], preferred_element_type=jnp.float32)
    h3 = jnp.dot(xn, w3_ref[...], preferred_element_type=jnp.float32)
    acc_ref[...] += jnp.dot(_silu(h1) * h3, w2_ref[...], preferred_element_type=jnp.float32)

    @pl.when(j == pl.num_programs(1) - 1)
    def _():
        o_ref[...] = x_ref[...] + acc_ref[...]


def ffn(x, gain, w1, w3, w2, tf=256):
    T, D = x.shape
    F = w1.shape[1]
    tm = _tile(T, 544)
    return pl.pallas_call(
        _ffn_kernel,
        grid=(T // tm, F // tf),
        in_specs=[pl.BlockSpec((tm, D), lambda i, j: (i, 0)),
                  pl.BlockSpec((1, D), lambda i, j: (0, 0)),
                  pl.BlockSpec((D, tf), lambda i, j: (0, j)),
                  pl.BlockSpec((D, tf), lambda i, j: (0, j)),
                  pl.BlockSpec((tf, D), lambda i, j: (j, 0))],
        out_specs=pl.BlockSpec((tm, D), lambda i, j: (i, 0)),
        out_shape=jax.ShapeDtypeStruct((T, D), jnp.float32),
        scratch_shapes=[pltpu.VMEM((tm, D), jnp.float32), pltpu.VMEM((tm, D), jnp.float32)],
        compiler_params=_cparams("parallel", "arbitrary"),
        name="ffn",
    )(x, gain.reshape(1, D), w1, w3, w2)


def _router_kernel(x_ref, g_ref, rt_ref, i1_ref, i2_ref, g1_ref, g2_ref, *, n_exp):
    xn = _rms(x_ref[...], g_ref[...])
    logit = [jnp.sum(xn * rt_ref[e:e + 1, :], axis=-1, keepdims=True) for e in range(n_exp)]
    neg = jnp.float32(-jnp.inf)

    def top(vals):
        m = vals[0]
        for v in vals[1:]:
            m = jnp.maximum(m, v)
        idx = jnp.full(m.shape, n_exp, jnp.int32)
        for e in reversed(range(n_exp)):
            idx = jnp.where(vals[e] == m, e, idx)
        return m, idx

    m1, i1 = top(logit)
    rest = [jnp.where(i1 == e, neg, logit[e]) for e in range(n_exp)]
    m2, i2 = top(rest)
    d = jnp.exp(m2 - m1)
    i1_ref[...] = i1
    i2_ref[...] = i2
    g1_ref[...] = 1.0 / (1.0 + d)
    g2_ref[...] = d / (1.0 + d)


def router_top2(x, gain, router):
    T, D = x.shape
    E = router.shape[1]
    tm = _tile(T, 544)
    col = pl.BlockSpec((tm, 1), lambda i: (i, 0))
    return pl.pallas_call(
        functools.partial(_router_kernel, n_exp=E),
        grid=(T // tm,),
        in_specs=[pl.BlockSpec((tm, D), lambda i: (i, 0)),
                  pl.BlockSpec((1, D), lambda i: (0, 0)),
                  pl.BlockSpec((E, D), lambda i: (0, 0))],
        out_specs=[col, col, col, col],
        out_shape=[jax.ShapeDtypeStruct((T, 1), jnp.int32)] * 2 + [jax.ShapeDtypeStruct((T, 1), jnp.float32)] * 2,
        compiler_params=_cparams("parallel"),
        name="router",
    )(x, gain.reshape(1, D), router.T)


MOE_SUB = 256


def _moe_plan(i1, i2, g1, g2, n_exp, cap):
    T = i1.shape[0]
    n_tiles = (2 * T) // cap + n_exp
    e_a = jnp.concatenate([i1[:, 0], i2[:, 0]])
    tok = jnp.concatenate([jnp.arange(T, dtype=jnp.int32)] * 2)
    gate = jnp.concatenate([g1[:, 0], g2[:, 0]])
    onehot = (e_a[:, None] == jnp.arange(n_exp, dtype=jnp.int32)[None, :]).astype(jnp.int32)
    csum = jnp.cumsum(onehot, axis=0)
    rank = jnp.sum((csum - onehot) * onehot, axis=1)
    counts = csum[-1]
    nt_e = (counts + cap - 1) // cap
    t_end = jnp.cumsum(nt_e)
    t_start = t_end - nt_e
    pos = (t_start[e_a] * cap + rank).astype(jnp.int32)
    n_used = t_end[-1]
    tiles = jnp.arange(n_tiles, dtype=jnp.int32)
    t_buf = jnp.minimum(tiles, n_used - 1).astype(jnp.int32)
    t_exp = jnp.minimum(jnp.sum((t_buf[:, None] >= t_end[None, :]).astype(jnp.int32), axis=1), n_exp - 1)
    t_rows = jnp.where(tiles < n_used, jnp.clip(counts[t_exp] - (tiles - t_start[t_exp]) * cap, 0, cap), 0)
    src = jnp.zeros((n_tiles * cap,), jnp.int32).at[pos].set(tok)
    gate_rows = jnp.zeros((n_tiles * cap,), jnp.float32).at[pos].set(gate)
    return (t_exp.astype(jnp.int32), t_rows.astype(jnp.int32), t_buf, src, gate_rows.reshape(-1, 1),
            pos[:T], pos[T:])


def _row_copy(src_hbm, row, dst, r, sem):
    return pltpu.make_async_copy(src_hbm.at[pl.ds(row, 1)], dst.at[pl.ds(r, 1)], sem)


def _gather_norm_kernel(src_ref, rows_ref, x_hbm, g_ref, o_ref, buf, sem, *, sub_per_tile):
    step = pl.program_id(0)
    tg = o_ref.shape[0]
    active = (step % sub_per_tile) * tg < rows_ref[step // sub_per_tile]

    @pl.when(active)
    def _():
        def issue(r, c):
            _row_copy(x_hbm, src_ref[step * tg + r], buf, r, sem).start()
            return c

        def wait(r, c):
            _row_copy(x_hbm, 0, buf, r, sem).wait()
            return c

        lax.fori_loop(0, tg, issue, 0)
        lax.fori_loop(0, tg, wait, 0)
        o_ref[...] = _rms(buf[...], g_ref[...]).astype(o_ref.dtype)

    @pl.when(jnp.logical_not(active))
    def _():
        o_ref[...] = jnp.zeros_like(o_ref)


def gather_norm(x, gain, src, t_rows, cap):
    T, D = x.shape
    n_rows = src.shape[0]
    tg = MOE_SUB
    return pl.pallas_call(
        functools.partial(_gather_norm_kernel, sub_per_tile=cap // tg),
        grid_spec=pltpu.PrefetchScalarGridSpec(
            num_scalar_prefetch=2,
            grid=(n_rows // tg,),
            in_specs=[pl.BlockSpec(memory_space=pl.ANY),
                      pl.BlockSpec((1, D), lambda i, s, r: (0, 0))],
            out_specs=pl.BlockSpec((tg, D), lambda i, s, r: (i, 0)),
            scratch_shapes=[pltpu.VMEM((tg, D), jnp.float32), pltpu.SemaphoreType.DMA(())]),
        out_shape=jax.ShapeDtypeStruct((n_rows, D), jnp.bfloat16),
        compiler_params=_cparams("arbitrary"),
        name="moe_gather",
    )(src, t_rows, x, gain.reshape(1, D))


def _moe_kernel(te_ref, tr_ref, tb_ref, xg_ref, gate_ref, w1_ref, w3_ref, w2_ref, o_ref):
    k = pl.program_id(0)
    j = pl.program_id(1)
    rows = tr_ref[k]
    busy = rows > 0

    @pl.when(j == 0)
    def _():
        o_ref[...] = jnp.zeros_like(o_ref)

    @pl.when(busy)
    def _():
        w1 = w1_ref[...].astype(jnp.bfloat16)
        w3 = w3_ref[...].astype(jnp.bfloat16)
        w2 = w2_ref[...].astype(jnp.bfloat16)

        def body(s, c):
            r0 = pl.multiple_of(s * MOE_SUB, MOE_SUB)
            xs = xg_ref[pl.ds(r0, MOE_SUB), :]
            h1 = jnp.dot(xs, w1, preferred_element_type=jnp.float32)
            h3 = jnp.dot(xs, w3, preferred_element_type=jnp.float32)
            h = (_silu(h1) * h3).astype(jnp.bfloat16)
            o_ref[pl.ds(r0, MOE_SUB), :] += jnp.dot(h, w2, preferred_element_type=jnp.float32)
            return c

        lax.fori_loop(0, (rows + MOE_SUB - 1) // MOE_SUB, body, 0)

    @pl.when(busy & (j == pl.num_programs(1) - 1))
    def _():
        o_ref[...] = o_ref[...] * gate_ref[...]


def moe_experts(xg, gate_rows, t_exp, t_rows, t_buf, w1, w3, w2, cap, tf=256):
    n_rows, D = xg.shape
    E, _, F = w1.shape
    n_tiles = n_rows // cap
    nj = F // tf

    def jj(k, j, tr):
        return jnp.where(tr[k] > 0, j, nj - 1)

    one = pl.Buffered(1)
    return pl.pallas_call(
        _moe_kernel,
        grid_spec=pltpu.PrefetchScalarGridSpec(
            num_scalar_prefetch=3,
            grid=(n_tiles, nj),
            in_specs=[pl.BlockSpec((cap, D), lambda k, j, te, tr, tb: (tb[k], 0), pipeline_mode=one),
                      pl.BlockSpec((cap, 1), lambda k, j, te, tr, tb: (tb[k], 0)),
                      pl.BlockSpec((None, D, tf), lambda k, j, te, tr, tb: (te[k], 0, jj(k, j, tr))),
                      pl.BlockSpec((None, D, tf), lambda k, j, te, tr, tb: (te[k], 0, jj(k, j, tr))),
                      pl.BlockSpec((None, tf, D), lambda k, j, te, tr, tb: (te[k], jj(k, j, tr), 0))],
            out_specs=pl.BlockSpec((cap, D), lambda k, j, te, tr, tb: (k, 0), pipeline_mode=one)),
        out_shape=jax.ShapeDtypeStruct((n_rows, D), jnp.float32),
        compiler_params=_cparams("arbitrary", "arbitrary"),
        name="moe_experts",
    )(t_exp, t_rows, t_buf, xg, gate_rows, w1, w3, w2)


def _combine_kernel(p1_ref, p2_ref, x_ref, *rest, final_norm):
    if final_norm:
        g_ref, y_hbm, o_ref, a_buf, b_buf, sem = rest
    else:
        y_hbm, o_ref, a_buf, b_buf, sem = rest
    i = pl.program_id(0)
    tc = x_ref.shape[0]

    def issue(r, c):
        _row_copy(y_hbm, p1_ref[i * tc + r], a_buf, r, sem.at[0]).start()
        _row_copy(y_hbm, p2_ref[i * tc + r], b_buf, r, sem.at[1]).start()
        return c

    def wait(r, c):
        _row_copy(y_hbm, 0, a_buf, r, sem.at[0]).wait()
        _row_copy(y_hbm, 0, b_buf, r, sem.at[1]).wait()
        return c

    lax.fori_loop(0, tc, issue, 0)
    lax.fori_loop(0, tc, wait, 0)
    y = x_ref[...] + (a_buf[...] + b_buf[...])
    o_ref[...] = _rms(y, g_ref[...]) if final_norm else y


def moe_combine(x, yg, pos1, pos2, final_gain=None):
    T, D = x.shape
    tc = _tile(T, 256)
    final_norm = final_gain is not None
    tok = pl.BlockSpec((tc, D), lambda i, a, b: (i, 0))
    in_specs = [tok] + ([pl.BlockSpec((1, D), lambda i, a, b: (0, 0))] if final_norm else [])
    in_specs.append(pl.BlockSpec(memory_space=pl.ANY))
    args = (x,) + ((final_gain.reshape(1, D),) if final_norm else ()) + (yg,)
    return pl.pallas_call(
        functools.partial(_combine_kernel, final_norm=final_norm),
        grid_spec=pltpu.PrefetchScalarGridSpec(
            num_scalar_prefetch=2,
            grid=(T // tc,),
            in_specs=in_specs,
            out_specs=tok,
            scratch_shapes=[pltpu.VMEM((tc, D), jnp.float32), pltpu.VMEM((tc, D), jnp.float32),
                            pltpu.SemaphoreType.DMA((2,))]),
        out_shape=jax.ShapeDtypeStruct((T, D), jnp.float32),
        compiler_params=_cparams("arbitrary"),
        name="moe_combine",
    )(pos1, pos2, *args)


def moe_top2(x, gain, router, w1, w3, w2, final_gain=None):
    T, D = x.shape
    E = router.shape[1]
    i1, i2, g1, g2 = router_top2(x, gain, router)
    cap = -(-(2 * T * 21 // (20 * E)) // MOE_SUB) * MOE_SUB
    t_exp, t_rows, t_buf, src, gate_rows, pos1, pos2 = _moe_plan(i1, i2, g1, g2, E, cap)
    xg = gather_norm(x, gain, src, t_rows, cap)
    yg = moe_experts(xg, gate_rows, t_exp, t_rows, t_buf, w1, w3, w2, cap)
    return moe_combine(x, yg, pos1, pos2, final_gain)


def _cumsum_rows(x):
    n = x.shape[0]
    row = lax.broadcasted_iota(jnp.int32, x.shape, 0)
    s = 1
    while s < n:
        x = x + jnp.where(row >= s, pltpu.roll(x, s, 0), 0.0)
        s *= 2
    return x


def _hgrn_kernel(q_ref, f_ref, i_ref, lbl_ref, s0_ref, o_ref, so_ref, s_scr, *, chunk, layer_j, valid_len):
    c = pl.program_id(2)
    tl = q_ref.shape[-2]
    dk = q_ref.shape[-1]

    @pl.when(c == 0)
    def _():
        s_scr[...] = s0_ref[...]

    lg = lbl_ref[...]
    ex = jnp.exp(lg - jnp.max(lg, axis=0, keepdims=True))
    lb = jnp.sum(ex[0:layer_j + 1, :], axis=0, keepdims=True) / jnp.sum(ex, axis=0, keepdims=True)

    pad = LANES - chunk
    zpad = jnp.zeros((pad, dk), jnp.float32)
    row = lax.broadcasted_iota(jnp.int32, (chunk, LANES), 0)
    col = lax.broadcasted_iota(jnp.int32, (chunk, LANES), 1)
    causal = col <= row
    ones_cd = jnp.ones((chunk, dk), jnp.float32)
    tn_dims = (((0,), (0,)), ((), ()))
    nt_dims = (((1,), (1,)), ((), ()))

    for cc in range(tl // chunk):
        sl = slice(cc * chunk, (cc + 1) * chunk)
        f = lb + (1.0 - lb) * jax.nn.sigmoid(f_ref[sl, :])
        q = _silu(q_ref[sl, :])
        v = i_ref[sl, :]
        if valid_len is not None:
            t_idx = c * tl + cc * chunk + lax.broadcasted_iota(jnp.int32, (chunk, dk), 0)
            ok = t_idx < valid_len
            f = jnp.where(ok, f, 1.0)
            q = jnp.where(ok, q, 0.0)
        g = jnp.log(f)
        k = 1.0 - f
        b = _cumsum_rows(g)
        mid = chunk // 2 - 1
        bm = b[mid:mid + 1, :]
        bl = b[chunk - 1:chunk, :]
        qe = q * jnp.exp(b - bm)
        ke = jnp.concatenate([k * jnp.exp(bm - b), zpad], axis=0)
        vp = jnp.concatenate([v, zpad], axis=0)
        a = lax.dot_general(qe, ke, nt_dims, preferred_element_type=jnp.float32)
        a = jnp.where(causal, a, 0.0)
        s = s_scr[...]
        o = jnp.dot(a, vp, preferred_element_type=jnp.float32)
        o = o + jnp.dot(q * jnp.exp(b), s, preferred_element_type=jnp.float32)
        o_ref[sl, :] = o
        kd = jnp.concatenate([k * jnp.exp(bl - b), zpad], axis=0)
        ghi, glo = _split_bf16(g)
        blc = (lax.dot_general(ghi, ones_cd, tn_dims, preferred_element_type=jnp.float32)
               + lax.dot_general(glo, ones_cd, tn_dims, preferred_element_type=jnp.float32))
        s_scr[...] = jnp.exp(blc) * s + lax.dot_general(kd, vp, tn_dims, preferred_element_type=jnp.float32)

    @pl.when(c == pl.num_programs(2) - 1)
    def _():
        so_ref[...] = s_scr[...]


def hgrn_scan(pa, row0, B, L, tl, chunk, lb_logits, s0, layer_j, valid_len=None):
    H = s0.shape[1]
    dk = s0.shape[2]
    nt = L // tl
    rb0 = row0 // tl
    kern = functools.partial(_hgrn_kernel, chunk=chunk, layer_j=layer_j, valid_len=valid_len)

    def col(off):
        return pl.BlockSpec((tl, dk), lambda b, h, c: (rb0 + b * nt + c, off * H + h))

    return pl.pallas_call(
        kern,
        grid=(B, H, nt),
        in_specs=[col(0), col(1), col(2),
                  pl.BlockSpec((lb_logits.shape[0], dk), lambda b, h, c: (0, h)),
                  pl.BlockSpec((None, None, dk, dk), lambda b, h, c: (b, h, 0, 0))],
        out_specs=[pl.BlockSpec((tl, dk), lambda b, h, c: (b * nt + c, h)),
                   pl.BlockSpec((None, None, dk, dk), lambda b, h, c: (b, h, 0, 0))],
        out_shape=[jax.ShapeDtypeStruct((B * L, H * dk), jnp.float32),
                   jax.ShapeDtypeStruct(s0.shape, jnp.float32)],
        scratch_shapes=[pltpu.VMEM((dk, dk), jnp.float32)],
        compiler_params=_cparams("parallel", "parallel", "arbitrary"),
        name="hgrn_scan",
    )(pa, pa, pa, lb_logits, s0)


def _seg_sum(x, bones):
    outs = []
    for j in range(x.shape[1] // LANES):
        hi, lo = _split_bf16(x[:, j * LANES:(j + 1) * LANES])
        outs.append(jnp.dot(hi, bones, preferred_element_type=jnp.float32)
                    + jnp.dot(lo, bones, preferred_element_type=jnp.float32))
    return jnp.concatenate(outs, axis=1)


def _rwkv_prep_kernel(u_ref, up_ref, mu_ref, w0_ref, a0_ref, kk_ref, ka_ref, rk_ref,
                      w2_ref, a2_ref, g2_ref, bones_ref,
                      kk_o, wr_o, w_o, kka_o, km_o, vhi_o, vlo_o, g_o, c1_o, c2_o, bon_o, *, d_b):
    u = u_ref[...]
    us = u + (up_ref[...] - u) * mu_ref[...]
    r = us[:, 0:d_b]
    k = us[:, d_b:2 * d_b]
    v = us[:, 2 * d_b:3 * d_b]
    wa = us[:, 3 * d_b:3 * d_b + LANES]
    gd = us[:, 3 * d_b + LANES:3 * d_b + 2 * LANES]
    bones = bones_ref[...]
    z = w0_ref[...] + jnp.dot(jnp.tanh(wa), w2_ref[...], preferred_element_type=jnp.float32)
    nz = -z
    softplus = jnp.maximum(nz, 0.0) + jnp.log(1.0 + jnp.exp(-jnp.abs(nz)))
    w_log = -softplus - 0.5
    decay = jnp.exp(-jnp.exp(w_log))
    a = jax.nn.sigmoid(a0_ref[...] + jnp.dot(wa, a2_ref[...], preferred_element_type=jnp.float32))
    g = jnp.dot(jax.nn.sigmoid(gd), g2_ref[...], preferred_element_type=jnp.float32)
    kk = k * kk_ref[...]
    kk = kk / jnp.maximum(jnp.sqrt(_seg_sum(kk * kk, bones)), 1e-12)
    kmod = k * (1.0 + (a - 1.0) * ka_ref[...])
    kka = kk * a
    vhi, vlo = _split_bf16(v)
    kk_o[...] = kk
    wr_o[...] = decay * r
    w_o[...] = decay
    kka_o[...] = kka
    km_o[...] = kmod
    vhi_o[...] = vhi
    vlo_o[...] = vlo
    g_o[...] = g
    c1_o[...] = _seg_sum(kka * r, bones)
    c2_o[...] = _seg_sum(kmod * r, bones)
    bon_o[...] = _seg_sum(r * kmod * rk_ref[...], bones)


def rwkv_prep(u, u_prev, mu, w0, a0, k_k, k_a, r_k, w2p, a2p, g2, bones, d_b):
    T, DU = u.shape
    tm = _tile(T, 272)
    row = lambda n: pl.BlockSpec((1, n), lambda i: (0, 0))
    full = lambda a: pl.BlockSpec(a.shape, lambda i: (0, 0))
    tok = lambda n: pl.BlockSpec((tm, n), lambda i: (i, 0))
    return pl.pallas_call(
        functools.partial(_rwkv_prep_kernel, d_b=d_b),
        grid=(T // tm,),
        in_specs=[tok(DU), tok(DU), row(DU), row(d_b), row(d_b), row(d_b), row(d_b), row(d_b),
                  full(w2p), full(a2p), full(g2), full(bones)],
        out_specs=[tok(d_b)] * 11,
        out_shape=[jax.ShapeDtypeStruct((T, d_b), jnp.float32)] * 11,
        compiler_params=_cparams("parallel"),
        name="rwkv_prep",
    )(u, u_prev, mu.reshape(1, DU), w0.reshape(1, d_b), a0.reshape(1, d_b), k_k.reshape(1, d_b),
      k_a.reshape(1, d_b), r_k.reshape(1, d_b), w2p, a2p, g2, bones)


def _rwkv_scan_kernel(kk_ref, wr_ref, w_ref, kka_ref, k_ref, vhi_ref, vlo_ref, s0_ref, rhs_ref, md_ref,
                      r2_o, sa_o, so_ref, s_scr, *, n_pair):
    tb = kk_ref.shape[-2]
    n = HEAD_B

    @pl.when(pl.program_id(1) == 0)
    def _():
        s_scr[...] = s0_ref[...]

    r2_o[...] = jnp.zeros_like(r2_o)
    sa_o[...] = jnp.zeros_like(sa_o)
    rhs = rhs_ref[...]
    md = md_ref[...]
    lane = lax.broadcasted_iota(jnp.int32, (n, LANES), 1) & (n - 1)

    def token_step(row, sel):
        parts = []
        for p in range(n_pair):
            sl = slice(p * LANES, (p + 1) * LANES)
            s = s_scr[p]
            p1 = s * row(kk_ref, sl)
            p2 = s * row(wr_ref, sl)
            dh = md * row(vhi_ref, sl)
            dl = md * row(vlo_ref, sl)
            parts.append(jnp.concatenate([p1, dh], axis=1))
            parts.append(jnp.concatenate([p2, dl], axis=1))
        out = jnp.dot(jnp.concatenate(parts, axis=0), rhs, preferred_element_type=jnp.float32)
        for p in range(n_pair):
            sl = slice(p * LANES, (p + 1) * LANES)
            top = out[2 * p * n:(2 * p + 1) * n]
            bot = out[(2 * p + 1) * n:(2 * p + 2) * n]
            sa_b = top[:, 0:LANES]
            r2_b = bot[:, 0:LANES]
            v_b = top[:, LANES:2 * LANES] + bot[:, LANES:2 * LANES]
            s = s_scr[p]
            s_scr[p] = s * row(w_ref, sl) - sa_b * row(kka_ref, sl) + v_b * row(k_ref, sl)
            r2_o[p] = jnp.where(sel, r2_b, r2_o[p])
            sa_o[p] = jnp.where(sel, sa_b, sa_o[p])

    if tb % SUBLANES == 0:
        def group(gi, carry):
            base = pl.multiple_of(gi * SUBLANES, SUBLANES)
            for jj in range(SUBLANES):
                token_step(lambda ref, sl, jj=jj: ref[pl.ds(base, SUBLANES), sl][jj:jj + 1, :],
                           lane == gi * SUBLANES + jj)
            return carry

        lax.fori_loop(0, tb // SUBLANES, group, 0)
    else:
        for tt in range(tb):
            token_step(lambda ref, sl, tt=tt: ref[tt:tt + 1, sl], lane == tt)

    @pl.when(pl.program_id(1) == pl.num_programs(1) - 1)
    def _():
        so_ref[...] = s_scr[...]


def rwkv_scan(vecs, B, L, tb, s0_pair, rhs, md):
    n_pair = s0_pair.shape[1]
    d_b = n_pair * LANES
    nt = L // tb
    tok = pl.BlockSpec((None, tb, d_b), lambda b, t: (b, t, 0))
    st = pl.BlockSpec((None, n_pair, HEAD_B, LANES), lambda b, t: (b, 0, 0, 0))
    colo = pl.BlockSpec((None, None, n_pair, HEAD_B, LANES), lambda b, t: (b, t, 0, 0, 0))
    col_shape = jax.ShapeDtypeStruct((B, nt, n_pair, HEAD_B, LANES), jnp.float32)
    return pl.pallas_call(
        functools.partial(_rwkv_scan_kernel, n_pair=n_pair),
        grid=(B, nt),
        in_specs=[tok] * 7 + [st, pl.BlockSpec(rhs.shape, lambda b, t: (0, 0)),
                              pl.BlockSpec(md.shape, lambda b, t: (0, 0))],
        out_specs=[colo, colo, st],
        out_shape=[col_shape, col_shape, jax.ShapeDtypeStruct(s0_pair.shape, jnp.float32)],
        scratch_shapes=[pltpu.VMEM((n_pair, HEAD_B, LANES), jnp.float32)],
        compiler_params=_cparams("parallel", "arbitrary"),
        name="rwkv_scan",
    )(*vecs, s0_pair, rhs, md)


def _mix_out_kernel(oa_ref, go_ref, hg_ref, r2_ref, sa_ref, c1_ref, c2_ref, vhi_ref, vlo_ref, g_ref, bon_ref,
                    lnw_ref, lnb_ref, bones_ref, o_ref, *, d_a):
    oa = oa_ref[...]
    outs = []
    for h in range(d_a // DK_A):
        x = oa[:, h * DK_A:(h + 1) * DK_A]
        outs.append(x * lax.rsqrt(jnp.mean(x * x, axis=-1, keepdims=True) + NORM_EPS))
    o_a = jnp.concatenate(outs, axis=1) * hg_ref[...] * _silu(go_ref[...])
    bones = bones_ref[...]
    v = vhi_ref[...] + vlo_ref[...]
    y = r2_ref[...] - sa_ref[...] * c1_ref[...] + v * c2_ref[...]
    inv_n = 1.0 / HEAD_B
    mean = _seg_sum(y, bones) * inv_n
    d = y - mean
    var = _seg_sum(d * d, bones) * inv_n
    yn = d * lax.rsqrt(var + RWKV_GN_EPS) * lnw_ref[...] + lnb_ref[...]
    o_b = (yn + bon_ref[...] * v) * g_ref[...]
    o_ref[:, 0:d_a] = o_a
    o_ref[:, d_a:] = o_b


def mix_out(oa, pa, hg_row, r2, sa, c1, c2, vhi, vlo, g, bon, ln_w, ln_b, bones):
    T, d_a = oa.shape
    d_b = r2.shape[1]
    tm = _tile(T, 272)
    tok = lambda n: pl.BlockSpec((tm, n), lambda i: (i, 0))
    row = lambda n: pl.BlockSpec((1, n), lambda i: (0, 0))
    return pl.pallas_call(
        functools.partial(_mix_out_kernel, d_a=d_a),
        grid=(T // tm,),
        in_specs=[tok(d_a), pl.BlockSpec((tm, d_a), lambda i: (i, 3)), row(d_a)]
                 + [tok(d_b)] * 8 + [row(d_b), row(d_b), pl.BlockSpec(bones.shape, lambda i: (0, 0))],
        out_specs=tok(d_a + d_b),
        out_shape=jax.ShapeDtypeStruct((T, d_a + d_b), jnp.float32),
        compiler_params=_cparams("parallel"),
        name="mix_out",
    )(oa, pa, hg_row, r2, sa, c1, c2, vhi, vlo, g, bon, ln_w.reshape(1, d_b), ln_b.reshape(1, d_b), bones)


def _conv_u_kernel(gc_ref, h_ref, u_ref):
    u_ref[...] = gc_ref[...] * h_ref[...]


def conv_u(cp, d):
    T = cp.shape[0]
    tm = _tile(T, 544)
    return pl.pallas_call(
        _conv_u_kernel,
        grid=(T // tm,),
        in_specs=[pl.BlockSpec((tm, d), lambda i: (i, 1)), pl.BlockSpec((tm, d), lambda i: (i, 2))],
        out_specs=pl.BlockSpec((tm, d), lambda i: (i, 0)),
        out_shape=jax.ShapeDtypeStruct((T, d), jnp.float32),
        compiler_params=_cparams("parallel"),
        name="conv_u",
    )(cp, cp)


def _conv_mm_kernel(gb_ref, u_ref, u1_ref, u2_ref, ck_ref, w_ref, r_ref, o_ref, z_ref):
    @pl.when(pl.program_id(1) == 0)
    def _():
        y = ck_ref[0:1, :] * u2_ref[...]
        y = y + ck_ref[1:2, :] * u1_ref[...]
        y = y + ck_ref[2:3, :] * u_ref[...]
        z_ref[...] = gb_ref[...] * y

    o_ref[...] = r_ref[...] + jnp.dot(z_ref[...], w_ref[...], preferred_element_type=jnp.float32)


def conv_mm(cp, u, u1, u2, conv_k, w, res, tn=512):
    T, d = u.shape
    N = w.shape[1]
    assert conv_k.shape[0] == 3
    tm = _tile(T, 272)
    tn = _tile(N, tn, LANES)
    tok = pl.BlockSpec((tm, d), lambda i, j: (i, 0))
    return pl.pallas_call(
        _conv_mm_kernel,
        grid=(T // tm, N // tn),
        in_specs=[tok, tok, tok, tok,
                  pl.BlockSpec(conv_k.shape, lambda i, j: (0, 0)),
                  pl.BlockSpec((d, tn), lambda i, j: (0, j)),
                  pl.BlockSpec((tm, tn), lambda i, j: (i, j))],
        out_specs=pl.BlockSpec((tm, tn), lambda i, j: (i, j)),
        out_shape=jax.ShapeDtypeStruct((T, N), jnp.float32),
        scratch_shapes=[pltpu.VMEM((tm, d), jnp.float32)],
        compiler_params=_cparams("parallel", "arbitrary"),
        name="conv_mm",
    )(cp, u, u1, u2, conv_k, w, res)


def _shift_rows(x, groups, states, k):
    outs = []
    row = 0
    for (B, L), st in zip(groups, states):
        xg = x[row:row + B * L].reshape(B, L, -1)
        outs.append(jnp.concatenate([st[:, st.shape[1] - k:], xg[:, :L - k]], axis=1).reshape(B * L, -1))
        row += B * L
    return jnp.concatenate(outs, axis=0)


def _last_rows(x, groups, k):
    outs = []
    row = 0
    for B, L in groups:
        outs.append(x[row:row + B * L].reshape(B, L, -1)[:, L - k:])
        row += B * L
    return outs


def _pair_state(s):
    B, H, n, _ = s.shape
    return s.reshape(B, H // 2, 2, n, n).transpose(0, 1, 3, 2, 4).reshape(B, H // 2, n, 2 * n)


def _unpair_state(s):
    B, hp, n, _ = s.shape
    return s.reshape(B, hp, n, 2, n).transpose(0, 1, 3, 2, 4).reshape(B, 2 * hp, n, n)


def _cols_to_rows(c, tb):
    B, nt, n_pair, n, _ = c.shape
    c = c.reshape(B, nt, n_pair, n, 2, n)[..., :tb]
    return c.transpose(0, 1, 5, 2, 4, 3).reshape(B * nt * tb, n_pair * 2 * n)


def kernel(x_prompt, x_sample, state_hgrn, state_rwkv, state_shift, state_conv, norm_mix, w_in_ab,
           hgrn_lb_logits, hgrn_norm, rwkv_mu, rwkv_w0, rwkv_w2, rwkv_a0, rwkv_a2, rwkv_g2, rwkv_k_k,
           rwkv_k_a, rwkv_r_k, rwkv_ln_w, rwkv_ln_b, w_out_ab, norm_ffn, ffn_w1, ffn_w3, ffn_w2,
           conv_w_in, conv_k, conv_w_out, moe_router, moe_w1, moe_w3, moe_w2, norm_final):
    f32 = jnp.float32
    Bp, Lp, D = x_prompt.shape
    Bs, Ls, _ = x_sample.shape
    Tp, Ts = Bp * Lp, Bs * Ls
    groups = ((Bp, Lp), (Bs, Ls))
    n_even = w_in_ab.shape[0]
    n_odd = conv_w_in.shape[0]
    depth = n_even + n_odd
    H_A, dk = state_hgrn.shape[2], state_hgrn.shape[3]
    d_a = H_A * dk
    H_B, hb = state_rwkv.shape[2], state_rwkv.shape[3]
    d_b = H_B * hb
    d_shift = state_shift.shape[-1]
    lora_w = rwkv_w2.shape[1]
    lora_a = rwkv_a2.shape[1]
    assert hb == HEAD_B and dk == DK_A and lora_w + lora_a == LANES and H_B % 2 == 0
    assert w_in_ab.shape[2] == 4 * d_a + d_shift

    seg = jnp.arange(LANES) // HEAD_B
    bones = (seg[:, None] == seg[None, :]).astype(f32)
    seg2 = jnp.arange(2 * LANES) // HEAD_B
    rhs_scan = (seg2[:, None] == seg2[None, :]).astype(f32)
    md = (jnp.arange(LANES)[None, :] % HEAD_B == jnp.arange(HEAD_B)[:, None]).astype(f32)

    x = jnp.concatenate([x_prompt.reshape(Tp, D), x_sample.reshape(Ts, D)], axis=0)
    new_h, new_r, new_s, new_c = ([], []), ([], []), ([], []), ([], [])

    for layer in range(depth):
        j = layer // 2
        if layer % 2 == 0:
            w_in = w_in_ab[j]
            pa = rms_mm(x, norm_mix[layer], w_in[:, :4 * d_a], tn=512)
            u = rms_mm(x, norm_mix[layer], w_in[:, 4 * d_a:], tn=_tile(d_shift, 512, LANES))
            zeros_h = jnp.zeros((Bp,) + state_hgrn.shape[2:], f32)
            tl = _tile(Lp, 256)
            oa_p, sh_p = hgrn_scan(pa, 0, Bp, Lp, tl, min(32, tl), hgrn_lb_logits, zeros_h, j)
            lpad = -(-Ls // SUBLANES) * SUBLANES
            pa_s = jnp.pad(pa[Tp:, :3 * d_a].reshape(Bs, Ls, 3 * d_a), ((0, 0), (0, lpad - Ls), (0, 0)))
            oa_s, sh_s = hgrn_scan(pa_s.reshape(Bs * lpad, 3 * d_a), 0, Bs, lpad, lpad, lpad, hgrn_lb_logits,
                                   state_hgrn[j], j, valid_len=Ls)
            oa = jnp.concatenate([oa_p, oa_s.reshape(Bs, lpad, d_a)[:, :Ls].reshape(Ts, d_a)], axis=0)
            new_h[0].append(sh_p)
            new_h[1].append(sh_s)
            zeros_s = jnp.zeros((Bp, 1, d_shift), f32)
            u_prev = _shift_rows(u, groups, (zeros_s, state_shift[j][:, None, :]), 1)
            w2p = jnp.concatenate([rwkv_w2[j], jnp.zeros((lora_a, d_b), f32)], axis=0)
            a2p = jnp.concatenate([jnp.zeros((lora_w, d_b), f32), rwkv_a2[j]], axis=0)
            prep = rwkv_prep(u, u_prev, rwkv_mu[j], rwkv_w0[j], rwkv_a0[j], rwkv_k_k[j], rwkv_k_a[j],
                             rwkv_r_k[j].reshape(d_b), w2p, a2p, rwkv_g2[j], bones, d_b)
            kk, wr, wdec, kka, kmod, vhi, vlo, gg, c1, c2, bon = prep
            scan_in = (kk, wr, wdec, kka, kmod, vhi, vlo)
            tb_p = _tile(Lp, HEAD_B)
            zeros_r = jnp.zeros((Bp, H_B // 2, hb, 2 * hb), f32)
            r2_p, sa_p, sr_p = rwkv_scan([a[:Tp].reshape(Bp, Lp, d_b) for a in scan_in], Bp, Lp, tb_p,
                                         zeros_r, rhs_scan, md)
            r2_s, sa_s, sr_s = rwkv_scan([a[Tp:].reshape(Bs, Ls, d_b) for a in scan_in], Bs, Ls, Ls,
                                         _pair_state(state_rwkv[j]), rhs_scan, md)
            r2 = jnp.concatenate([_cols_to_rows(r2_p, tb_p), _cols_to_rows(r2_s, Ls)], axis=0)
            sa = jnp.concatenate([_cols_to_rows(sa_p, tb_p), _cols_to_rows(sa_s, Ls)], axis=0)
            new_r[0].append(_unpair_state(sr_p))
            new_r[1].append(_unpair_state(sr_s))
            last_u = _last_rows(u, groups, 1)
            new_s[0].append(last_u[0][:, 0])
            new_s[1].append(last_u[1][:, 0])
            hg_row = jnp.tile(hgrn_norm[j], H_A).reshape(1, d_a)
            o = mix_out(oa, pa, hg_row, r2, sa, c1, c2, vhi, vlo, gg, bon, rwkv_ln_w[j], rwkv_ln_b[j], bones)
            x = mm_res(o, w_out_ab[j], x)
            x = ffn(x, norm_ffn[layer], ffn_w1[j], ffn_w3[j], ffn_w2[j])
        else:
            d_c = state_conv.shape[-1]
            cw = state_conv.shape[2]
            cp = rms_mm(x, norm_mix[layer], conv_w_in[j], tn=512)
            uc = conv_u(cp, d_c)
            conv_states = (jnp.zeros((Bp, cw, d_c), f32), state_conv[j])
            u1 = _shift_rows(uc, groups, conv_states, 1)
            u2 = _shift_rows(uc, groups, conv_states, 2)
            last_c = _last_rows(uc, groups, cw)
            new_c[0].append(last_c[0])
            new_c[1].append(last_c[1])
            x = conv_mm(cp, uc, u1, u2, conv_k[j], conv_w_out[j], x)
            x = moe_top2(x, norm_ffn[layer], moe_router[j], moe_w1[j], moe_w3[j], moe_w2[j],
                         final_gain=norm_final if layer == depth - 1 else None)

    y = x if depth % 2 == 0 else rms_only(x, norm_final)
    y_prompt = y[:Tp].reshape(Bp, Lp, D)
    y_sample = y[Tp:].reshape(Bs, Ls, D)
    outs = [y_prompt, y_sample]
    for g in (0, 1):
        outs += [jnp.stack(new_h[g]), jnp.stack(new_r[g]), jnp.stack(new_s[g]), jnp.stack(new_c[g])]
    return tuple(outs)
```

```python
import functools

import jax
import jax.numpy as jnp
from jax import lax
from jax.experimental import pallas as pl
from jax.experimental.pallas import tpu as pltpu

NORM_EPS = 1e-6
RWKV_GN_EPS = 64e-5
HEAD_B = 64
DK_A = 128
LANES = 128
SUBLANES = 8
VMEM_LIMIT = 56 * 1024 * 1024


def _cparams(*sem):
    return pltpu.CompilerParams(dimension_semantics=sem, vmem_limit_bytes=VMEM_LIMIT)


def _tile(n, cap, mult=SUBLANES):
    best = None
    for d in range(mult, min(n, cap) + 1, mult):
        if n % d == 0:
            best = d
    return n if best is None else best


def _rms(x, gain):
    return x * lax.rsqrt(jnp.mean(x * x, axis=-1, keepdims=True) + NORM_EPS) * gain


def _silu(x):
    return x * jax.nn.sigmoid(x)


def _split_bf16(x):
    hi = x.astype(jnp.bfloat16).astype(jnp.float32)
    return hi, x - hi


def _rms_mm_kernel(x_ref, g_ref, w_ref, o_ref, xn_ref):
    @pl.when(pl.program_id(1) == 0)
    def _():
        xn_ref[...] = _rms(x_ref[...], g_ref[...])

    o_ref[...] = jnp.dot(xn_ref[...], w_ref[...], preferred_element_type=jnp.float32)


def rms_mm(x, gain, w, tn, col0=0, n_cols=None):
    T, K = x.shape
    N = w.shape[1] - col0 if n_cols is None else n_cols
    assert col0 % tn == 0 and N % tn == 0
    tm = _tile(T, 1088)
    cb0 = col0 // tn
    return pl.pallas_call(
        _rms_mm_kernel,
        grid=(T // tm, N // tn),
        in_specs=[pl.BlockSpec((tm, K), lambda i, j: (i, 0)),
                  pl.BlockSpec((1, K), lambda i, j: (0, 0)),
                  pl.BlockSpec((K, tn), lambda i, j: (0, cb0 + j))],
        out_specs=pl.BlockSpec((tm, tn), lambda i, j: (i, j)),
        out_shape=jax.ShapeDtypeStruct((T, N), jnp.float32),
        scratch_shapes=[pltpu.VMEM((tm, K), jnp.float32)],
        compiler_params=_cparams("parallel", "arbitrary"),
        name="rms_mm",
    )(x, gain.reshape(1, K), w)


def _mm_res_kernel(a_ref, w_ref, r_ref, o_ref):
    o_ref[...] = r_ref[...] + jnp.dot(a_ref[...], w_ref[...], preferred_element_type=jnp.float32)


def mm_res(a, w, res, tn=512):
    T, K = a.shape
    N = w.shape[1]
    tm = _tile(T, 1088)
    tn = _tile(N, tn, LANES)
    return pl.pallas_call(
        _mm_res_kernel,
        grid=(T // tm, N // tn),
        in_specs=[pl.BlockSpec((tm, K), lambda i, j: (i, 0)),
                  pl.BlockSpec((K, tn), lambda i, j: (0, j)),
                  pl.BlockSpec((tm, tn), lambda i, j: (i, j))],
        out_specs=pl.BlockSpec((tm, tn), lambda i, j: (i, j)),
        out_shape=jax.ShapeDtypeStruct((T, N), jnp.float32),
        compiler_params=_cparams("parallel", "parallel"),
        name="mm_res",
    )(a, w, res)


def _rms_only_kernel(x_ref, g_ref, o_ref):
    o_ref[...] = _rms(x_ref[...], g_ref[...])


def rms_only(x, gain):
    T, K = x.shape
    tm = _tile(T, 544)
    return pl.pallas_call(
        _rms_only_kernel,
        grid=(T // tm,),
        in_specs=[pl.BlockSpec((tm, K), lambda i: (i, 0)),
                  pl.BlockSpec((1, K), lambda i: (0, 0))],
        out_specs=pl.BlockSpec((tm, K), lambda i: (i, 0)),
        out_shape=jax.ShapeDtypeStruct((T, K), jnp.float32),
        compiler_params=_cparams("parallel"),
        name="rms_final",
    )(x, gain.reshape(1, K))


def _ffn_kernel(x_ref, g_ref, w1_ref, w3_ref, w2_ref, o_ref, xn_ref):
    j = pl.program_id(1)

    @pl.when(j == 0)
    def _():
        xn_ref[...] = _rms(x_ref[...], g_ref[...])
        o_ref[...] = jnp.zeros_like(o_ref)

    xn = xn_ref[...]
    h1 = jnp.dot(xn, w1_ref[...], preferred_element_type=jnp.float32)
    h3 = jnp.dot(xn, w3_ref[...], preferred_element_type=jnp.float32)
    o_ref[...] += jnp.dot(_silu(h1) * h3, w2_ref[...], preferred_element_type=jnp.float32)

    @pl.when(j == pl.num_programs(1) - 1)
    def _():
        o_ref[...] = x_ref[...] + o_ref[...]


def ffn(x, gain, w1, w3, w2, tf=256):
    T, D = x.shape
    F = w1.shape[1]
    tm = _tile(T, 1088)
    one = pl.Buffered(1)
    return pl.pallas_call(
        _ffn_kernel,
        grid=(T // tm, F // tf),
        in_specs=[pl.BlockSpec((tm, D), lambda i, j: (i, 0), pipeline_mode=one),
                  pl.BlockSpec((1, D), lambda i, j: (0, 0)),
                  pl.BlockSpec((D, tf), lambda i, j: (0, j)),
                  pl.BlockSpec((D, tf), lambda i, j: (0, j)),
                  pl.BlockSpec((tf, D), lambda i, j: (j, 0))],
        out_specs=pl.BlockSpec((tm, D), lambda i, j: (i, 0), pipeline_mode=one),
        out_shape=jax.ShapeDtypeStruct((T, D), jnp.float32),
        scratch_shapes=[pltpu.VMEM((tm, D), jnp.float32)],
        compiler_params=_cparams("parallel", "arbitrary"),
        name="ffn",
    )(x, gain.reshape(1, D), w1, w3, w2)


def _router_kernel(x_ref, g_ref, rt_ref, i1_ref, i2_ref, g1_ref, g2_ref, *, n_exp):
    xn = _rms(x_ref[...], g_ref[...])
    logit = [jnp.sum(xn * rt_ref[e:e + 1, :], axis=-1, keepdims=True) for e in range(n_exp)]
    neg = jnp.float32(-jnp.inf)

    def top(vals):
        m = vals[0]
        for v in vals[1:]:
            m = jnp.maximum(m, v)
        idx = jnp.full(m.shape, n_exp, jnp.int32)
        for e in reversed(range(n_exp)):
            idx = jnp.where(vals[e] == m, e, idx)
        return m, idx

    m1, i1 = top(logit)
    rest = [jnp.where(i1 == e, neg, logit[e]) for e in range(n_exp)]
    m2, i2 = top(rest)
    d = jnp.exp(m2 - m1)
    i1_ref[...] = i1
    i2_ref[...] = i2
    g1_ref[...] = 1.0 / (1.0 + d)
    g2_ref[...] = d / (1.0 + d)


def router_top2(x, gain, router):
    T, D = x.shape
    E = router.shape[1]
    tm = _tile(T, 544)
    col = pl.BlockSpec((tm, 1), lambda i: (i, 0))
    return pl.pallas_call(
        functools.partial(_router_kernel, n_exp=E),
        grid=(T // tm,),
        in_specs=[pl.BlockSpec((tm, D), lambda i: (i, 0)),
                  pl.BlockSpec((1, D), lambda i: (0, 0)),
                  pl.BlockSpec((E, D), lambda i: (0, 0))],
        out_specs=[col, col, col, col],
        out_shape=[jax.ShapeDtypeStruct((T, 1), jnp.int32)] * 2 + [jax.ShapeDtypeStruct((T, 1), jnp.float32)] * 2,
        compiler_params=_cparams("parallel"),
        name="router",
    )(x, gain.reshape(1, D), router.T)


MOE_SUB = 256


def _moe_plan(i1, i2, g1, g2, n_exp, cap):
    T = i1.shape[0]
    n_tiles = (2 * T) // cap + n_exp
    e_a = jnp.concatenate([i1[:, 0], i2[:, 0]])
    tok = jnp.concatenate([jnp.arange(T, dtype=jnp.int32)] * 2)
    gate = jnp.concatenate([g1[:, 0], g2[:, 0]])
    onehot = (e_a[:, None] == jnp.arange(n_exp, dtype=jnp.int32)[None, :]).astype(jnp.int32)
    csum = jnp.cumsum(onehot, axis=0)
    rank = jnp.sum((csum - onehot) * onehot, axis=1)
    counts = csum[-1]
    nt_e = (counts + cap - 1) // cap
    t_end = jnp.cumsum(nt_e)
    t_start = t_end - nt_e
    pos = (t_start[e_a] * cap + rank).astype(jnp.int32)
    n_used = t_end[-1]
    tiles = jnp.arange(n_tiles, dtype=jnp.int32)
    t_buf = jnp.minimum(tiles, n_used - 1).astype(jnp.int32)
    t_exp = jnp.minimum(jnp.sum((t_buf[:, None] >= t_end[None, :]).astype(jnp.int32), axis=1), n_exp - 1)
    t_rows = jnp.where(tiles < n_used, jnp.clip(counts[t_exp] - (tiles - t_start[t_exp]) * cap, 0, cap), 0)
    src = jnp.zeros((n_tiles * cap,), jnp.int32).at[pos].set(tok)
    gate_rows = jnp.zeros((n_tiles * cap,), jnp.float32).at[pos].set(gate)
    return (t_exp.astype(jnp.int32), t_rows.astype(jnp.int32), t_buf, src, gate_rows.reshape(-1, 1),
            pos[:T], pos[T:])


def _row_copy(src_hbm, row, dst, r, sem):
    return pltpu.make_async_copy(src_hbm.at[pl.ds(row, 1)], dst.at[pl.ds(r, 1)], sem)


def _gather_norm_kernel(src_ref, rows_ref, x_hbm, g_ref, o_ref, buf, sem, *, sub_per_tile):
    step = pl.program_id(0)
    tg = o_ref.shape[0]
    active = (step % sub_per_tile) * tg < rows_ref[step // sub_per_tile]

    @pl.when(active)
    def _():
        def issue(r, c):
            _row_copy(x_hbm, src_ref[step * tg + r], buf, r, sem).start()
            return c

        def wait(r, c):
            _row_copy(x_hbm, 0, buf, r, sem).wait()
            return c

        lax.fori_loop(0, tg, issue, 0)
        lax.fori_loop(0, tg, wait, 0)
        o_ref[...] = _rms(buf[...], g_ref[...]).astype(o_ref.dtype)

    @pl.when(jnp.logical_not(active))
    def _():
        o_ref[...] = jnp.zeros_like(o_ref)


def gather_norm(x, gain, src, t_rows, cap):
    T, D = x.shape
    n_rows = src.shape[0]
    tg = MOE_SUB
    return pl.pallas_call(
        functools.partial(_gather_norm_kernel, sub_per_tile=cap // tg),
        grid_spec=pltpu.PrefetchScalarGridSpec(
            num_scalar_prefetch=2,
            grid=(n_rows // tg,),
            in_specs=[pl.BlockSpec(memory_space=pl.ANY),
                      pl.BlockSpec((1, D), lambda i, s, r: (0, 0))],
            out_specs=pl.BlockSpec((tg, D), lambda i, s, r: (i, 0)),
            scratch_shapes=[pltpu.VMEM((tg, D), jnp.float32), pltpu.SemaphoreType.DMA(())]),
        out_shape=jax.ShapeDtypeStruct((n_rows, D), jnp.bfloat16),
        compiler_params=_cparams("arbitrary"),
        name="moe_gather",
    )(src, t_rows, x, gain.reshape(1, D))


def _moe_kernel(te_ref, tr_ref, tb_ref, xg_ref, gate_ref, w1_ref, w3_ref, w2_ref, o_ref):
    k = pl.program_id(0)
    j = pl.program_id(1)
    rows = tr_ref[k]
    busy = rows > 0

    @pl.when(j == 0)
    def _():
        o_ref[...] = jnp.zeros_like(o_ref)

    @pl.when(busy)
    def _():
        w1 = w1_ref[...].astype(jnp.bfloat16)
        w3 = w3_ref[...].astype(jnp.bfloat16)
        w2 = w2_ref[...].astype(jnp.bfloat16)

        def up(s):
            xs = xg_ref[pl.ds(pl.multiple_of(s * MOE_SUB, MOE_SUB), MOE_SUB), :]
            h1 = jnp.dot(xs, w1, preferred_element_type=jnp.float32)
            h3 = jnp.dot(xs, w3, preferred_element_type=jnp.float32)
            return (_silu(h1) * h3).astype(jnp.bfloat16)

        def down(s, h):
            r0 = pl.multiple_of(s * MOE_SUB, MOE_SUB)
            o_ref[pl.ds(r0, MOE_SUB), :] += jnp.dot(h, w2, preferred_element_type=jnp.float32)

        def body(s, h):
            h_next = up(s + 1)
            down(s, h)
            return h_next

        last = (rows + MOE_SUB - 1) // MOE_SUB - 1
        down(last, lax.fori_loop(0, last, body, up(0)))

    @pl.when(busy & (j == pl.num_programs(1) - 1))
    def _():
        o_ref[...] = o_ref[...] * gate_ref[...]


def moe_experts(xg, gate_rows, t_exp, t_rows, t_buf, w1, w3, w2, cap, tf=256):
    n_rows, D = xg.shape
    E, _, F = w1.shape
    n_tiles = n_rows // cap
    nj = F // tf

    def jj(k, j, tr):
        return jnp.where(tr[k] > 0, j, nj - 1)

    one = pl.Buffered(1)
    return pl.pallas_call(
        _moe_kernel,
        grid_spec=pltpu.PrefetchScalarGridSpec(
            num_scalar_prefetch=3,
            grid=(n_tiles, nj),
            in_specs=[pl.BlockSpec((cap, D), lambda k, j, te, tr, tb: (tb[k], 0), pipeline_mode=one),
                      pl.BlockSpec((cap, 1), lambda k, j, te, tr, tb: (tb[k], 0)),
                      pl.BlockSpec((None, D, tf), lambda k, j, te, tr, tb: (te[k], 0, jj(k, j, tr))),
                      pl.BlockSpec((None, D, tf), lambda k, j, te, tr, tb: (te[k], 0, jj(k, j, tr))),
                      pl.BlockSpec((None, tf, D), lambda k, j, te, tr, tb: (te[k], jj(k, j, tr), 0))],
            out_specs=pl.BlockSpec((cap, D), lambda k, j, te, tr, tb: (k, 0), pipeline_mode=one)),
        out_shape=jax.ShapeDtypeStruct((n_rows, D), jnp.float32),
        compiler_params=_cparams("arbitrary", "arbitrary"),
        name="moe_experts",
    )(t_exp, t_rows, t_buf, xg, gate_rows, w1, w3, w2)


def _combine_kernel(p1_ref, p2_ref, x_ref, *rest, final_norm):
    if final_norm:
        g_ref, y_hbm, o_ref, a_buf, b_buf, sem = rest
    else:
        y_hbm, o_ref, a_buf, b_buf, sem = rest
    i = pl.program_id(0)
    tc = x_ref.shape[0]

    def issue(r, c):
        _row_copy(y_hbm, p1_ref[i * tc + r], a_buf, r, sem.at[0]).start()
        _row_copy(y_hbm, p2_ref[i * tc + r], b_buf, r, sem.at[1]).start()
        return c

    def wait(r, c):
        _row_copy(y_hbm, 0, a_buf, r, sem.at[0]).wait()
        _row_copy(y_hbm, 0, b_buf, r, sem.at[1]).wait()
        return c

    lax.fori_loop(0, tc, issue, 0)
    lax.fori_loop(0, tc, wait, 0)
    y = x_ref[...] + (a_buf[...] + b_buf[...])
    o_ref[...] = _rms(y, g_ref[...]) if final_norm else y


def moe_combine(x, yg, pos1, pos2, final_gain=None):
    T, D = x.shape
    tc = _tile(T, 256)
    final_norm = final_gain is not None
    tok = pl.BlockSpec((tc, D), lambda i, a, b: (i, 0))
    in_specs = [tok] + ([pl.BlockSpec((1, D), lambda i, a, b: (0, 0))] if final_norm else [])
    in_specs.append(pl.BlockSpec(memory_space=pl.ANY))
    args = (x,) + ((final_gain.reshape(1, D),) if final_norm else ()) + (yg,)
    return pl.pallas_call(
        functools.partial(_combine_kernel, final_norm=final_norm),
        grid_spec=pltpu.PrefetchScalarGridSpec(
            num_scalar_prefetch=2,
            grid=(T // tc,),
            in_specs=in_specs,
            out_specs=tok,
            scratch_shapes=[pltpu.VMEM((tc, D), jnp.float32), pltpu.VMEM((tc, D), jnp.float32),
                            pltpu.SemaphoreType.DMA((2,))]),
        out_shape=jax.ShapeDtypeStruct((T, D), jnp.float32),
        compiler_params=_cparams("arbitrary"),
        name="moe_combine",
    )(pos1, pos2, *args)


def moe_top2(x, gain, router, w1, w3, w2, final_gain=None):
    T, D = x.shape
    E = router.shape[1]
    i1, i2, g1, g2 = router_top2(x, gain, router)
    cap = -(-(2 * T * 21 // (20 * E)) // MOE_SUB) * MOE_SUB
    t_exp, t_rows, t_buf, src, gate_rows, pos1, pos2 = _moe_plan(i1, i2, g1, g2, E, cap)
    xg = gather_norm(x, gain, src, t_rows, cap)
    yg = moe_experts(xg, gate_rows, t_exp, t_rows, t_buf, w1, w3, w2, cap)
    return moe_combine(x, yg, pos1, pos2, final_gain)


def _cumsum_rows(x):
    n = x.shape[0]
    row = lax.broadcasted_iota(jnp.int32, x.shape, 0)
    s = 1
    while s < n:
        x = x + jnp.where(row >= s, pltpu.roll(x, s, 0), 0.0)
        s *= 2
    return x


def _hgrn_kernel(q_ref, f_ref, i_ref, lbl_ref, s0_ref, o_ref, so_ref, s_scr, *, chunk, layer_j, valid_len):
    c = pl.program_id(1)
    tl = q_ref.shape[0]
    n_head, dk, _ = s_scr.shape

    @pl.when(c == 0)
    def _():
        s_scr[...] = s0_ref[...]

    lg = lbl_ref[...]
    ex = jnp.exp(lg - jnp.max(lg, axis=0, keepdims=True))
    lb_all = jnp.sum(ex[0:layer_j + 1, :], axis=0, keepdims=True) / jnp.sum(ex, axis=0, keepdims=True)

    pad = LANES - chunk
    zpad = jnp.zeros((pad, dk), jnp.float32)
    row = lax.broadcasted_iota(jnp.int32, (chunk, LANES), 0)
    col = lax.broadcasted_iota(jnp.int32, (chunk, LANES), 1)
    causal = col <= row
    ones_cd = jnp.ones((chunk, dk), jnp.float32)
    tn_dims = (((0,), (0,)), ((), ()))
    nt_dims = (((1,), (1,)), ((), ()))

    for cc in range(tl // chunk):
        sl = slice(cc * chunk, (cc + 1) * chunk)
        for h in range(n_head):
            hs = slice(h * dk, (h + 1) * dk)
            lb = lb_all[:, hs]
            f = lb + (1.0 - lb) * jax.nn.sigmoid(f_ref[sl, hs])
            q = _silu(q_ref[sl, hs])
            v = i_ref[sl, hs]
            if valid_len is not None:
                t_idx = c * tl + cc * chunk + lax.broadcasted_iota(jnp.int32, (chunk, dk), 0)
                ok = t_idx < valid_len
                f = jnp.where(ok, f, 1.0)
                q = jnp.where(ok, q, 0.0)
            g = jnp.log(f)
            k = 1.0 - f
            b = _cumsum_rows(g)
            mid = chunk // 2 - 1
            bm = b[mid:mid + 1, :]
            bl = b[chunk - 1:chunk, :]
            qe = q * jnp.exp(b - bm)
            ke = jnp.concatenate([k * jnp.exp(bm - b), zpad], axis=0)
            vp = jnp.concatenate([v, zpad], axis=0)
            a = lax.dot_general(qe, ke, nt_dims, preferred_element_type=jnp.float32)
            a = jnp.where(causal, a, 0.0)
            s = s_scr[h]
            o = jnp.dot(a, vp, preferred_element_type=jnp.float32)
            o = o + jnp.dot(q * jnp.exp(b), s, preferred_element_type=jnp.float32)
            o_ref[sl, hs] = o
            kd = jnp.concatenate([k * jnp.exp(bl - b), zpad], axis=0)
            ghi, glo = _split_bf16(g)
            blc = (lax.dot_general(ghi, ones_cd, tn_dims, preferred_element_type=jnp.float32)
                   + lax.dot_general(glo, ones_cd, tn_dims, preferred_element_type=jnp.float32))
            s_scr[h] = jnp.exp(blc) * s + lax.dot_general(kd, vp, tn_dims, preferred_element_type=jnp.float32)

    @pl.when(c == pl.num_programs(1) - 1)
    def _():
        so_ref[...] = s_scr[...]


def hgrn_scan(pa, B, L, tl, chunk, lb_logits, s0, layer_j, valid_len=None):
    H, dk = s0.shape[1], s0.shape[2]
    d_a = H * dk
    nt = L // tl
    kern = functools.partial(_hgrn_kernel, chunk=chunk, layer_j=layer_j, valid_len=valid_len)

    def col(off):
        return pl.BlockSpec((tl, d_a), lambda b, c: (b * nt + c, off))

    st = pl.BlockSpec((None, H, dk, dk), lambda b, c: (b, 0, 0, 0))
    return pl.pallas_call(
        kern,
        grid=(B, nt),
        in_specs=[col(0), col(1), col(2), pl.BlockSpec(lb_logits.shape, lambda b, c: (0, 0)), st],
        out_specs=[pl.BlockSpec((tl, d_a), lambda b, c: (b * nt + c, 0)), st],
        out_shape=[jax.ShapeDtypeStruct((B * L, d_a), jnp.float32),
                   jax.ShapeDtypeStruct(s0.shape, jnp.float32)],
        scratch_shapes=[pltpu.VMEM((H, dk, dk), jnp.float32)],
        compiler_params=_cparams("parallel", "arbitrary"),
        name="hgrn_scan",
    )(pa, pa, pa, lb_logits, s0)


def _seg_sum(x, bones):
    outs = []
    for j in range(x.shape[1] // LANES):
        hi, lo = _split_bf16(x[:, j * LANES:(j + 1) * LANES])
        outs.append(jnp.dot(hi, bones, preferred_element_type=jnp.float32)
                    + jnp.dot(lo, bones, preferred_element_type=jnp.float32))
    return jnp.concatenate(outs, axis=1)


def _rwkv_prep_kernel(u_ref, up_ref, mu_ref, w0_ref, a0_ref, kk_ref, ka_ref, rk_ref,
                      w2_ref, a2_ref, g2_ref, bones_ref,
                      kk_o, wr_o, w_o, kka_o, km_o, vhi_o, vlo_o, g_o, c1_o, c2_o, bon_o, *, d_b):
    u = u_ref[...]
    us = u + (up_ref[...] - u) * mu_ref[...]
    r = us[:, 0:d_b]
    k = us[:, d_b:2 * d_b]
    v = us[:, 2 * d_b:3 * d_b]
    wa = us[:, 3 * d_b:3 * d_b + LANES]
    gd = us[:, 3 * d_b + LANES:3 * d_b + 2 * LANES]
    bones = bones_ref[...]
    z = w0_ref[...] + jnp.dot(jnp.tanh(wa), w2_ref[...], preferred_element_type=jnp.float32)
    nz = -z
    softplus = jnp.maximum(nz, 0.0) + jnp.log(1.0 + jnp.exp(-jnp.abs(nz)))
    w_log = -softplus - 0.5
    decay = jnp.exp(-jnp.exp(w_log))
    a = jax.nn.sigmoid(a0_ref[...] + jnp.dot(wa, a2_ref[...], preferred_element_type=jnp.float32))
    g = jnp.dot(jax.nn.sigmoid(gd), g2_ref[...], preferred_element_type=jnp.float32)
    kk = k * kk_ref[...]
    kk = kk / jnp.maximum(jnp.sqrt(_seg_sum(kk * kk, bones)), 1e-12)
    kmod = k * (1.0 + (a - 1.0) * ka_ref[...])
    kka = kk * a
    vhi, vlo = _split_bf16(v)
    kk_o[...] = kk
    wr_o[...] = decay * r
    w_o[...] = decay
    kka_o[...] = kka
    km_o[...] = kmod
    vhi_o[...] = vhi
    vlo_o[...] = vlo
    g_o[...] = g
    c1_o[...] = _seg_sum(kka * r, bones)
    c2_o[...] = _seg_sum(kmod * r, bones)
    bon_o[...] = _seg_sum(r * kmod * rk_ref[...], bones)


def rwkv_prep(u, u_prev, mu, w0, a0, k_k, k_a, r_k, w2p, a2p, g2, bones, d_b):
    T, DU = u.shape
    tm = _tile(T, 272)
    row = lambda n: pl.BlockSpec((1, n), lambda i: (0, 0))
    full = lambda a: pl.BlockSpec(a.shape, lambda i: (0, 0))
    tok = lambda n: pl.BlockSpec((tm, n), lambda i: (i, 0))
    return pl.pallas_call(
        functools.partial(_rwkv_prep_kernel, d_b=d_b),
        grid=(T // tm,),
        in_specs=[tok(DU), tok(DU), row(DU), row(d_b), row(d_b), row(d_b), row(d_b), row(d_b),
                  full(w2p), full(a2p), full(g2), full(bones)],
        out_specs=[tok(d_b)] * 11,
        out_shape=[jax.ShapeDtypeStruct((T, d_b), jnp.float32)] * 11,
        compiler_params=_cparams("parallel"),
        name="rwkv_prep",
    )(u, u_prev, mu.reshape(1, DU), w0.reshape(1, d_b), a0.reshape(1, d_b), k_k.reshape(1, d_b),
      k_a.reshape(1, d_b), r_k.reshape(1, d_b), w2p, a2p, g2, bones)


def _rwkv_scan_kernel(kk_ref, wr_ref, w_ref, kka_ref, k_ref, vhi_ref, vlo_ref, c1_ref, c2_ref,
                      s0_ref, rhs_ref, md_ref, y_ref, so_ref, s_scr, r2_o, sa_o, *, n_pair):
    tb = kk_ref.shape[-2]
    n = HEAD_B

    @pl.when(pl.program_id(1) == 0)
    def _():
        s_scr[...] = s0_ref[...]

    r2_o[...] = jnp.zeros_like(r2_o)
    sa_o[...] = jnp.zeros_like(sa_o)
    rhs = rhs_ref[...]
    md = md_ref[...]
    lane = lax.broadcasted_iota(jnp.int32, (n, LANES), 1) & (n - 1)

    def token_step(row, sel):
        parts = []
        for p in range(n_pair):
            sl = slice(p * LANES, (p + 1) * LANES)
            s = s_scr[p]
            p1 = s * row(kk_ref, sl)
            p2 = s * row(wr_ref, sl)
            dh = md * row(vhi_ref, sl)
            dl = md * row(vlo_ref, sl)
            parts.append(jnp.concatenate([p1, dh], axis=1))
            parts.append(jnp.concatenate([p2, dl], axis=1))
        out = jnp.dot(jnp.concatenate(parts, axis=0), rhs, preferred_element_type=jnp.float32)
        for p in range(n_pair):
            sl = slice(p * LANES, (p + 1) * LANES)
            top = out[2 * p * n:(2 * p + 1) * n]
            bot = out[(2 * p + 1) * n:(2 * p + 2) * n]
            sa_b = top[:, 0:LANES]
            r2_b = bot[:, 0:LANES]
            v_b = top[:, LANES:2 * LANES] + bot[:, LANES:2 * LANES]
            s = s_scr[p]
            s_scr[p] = s * row(w_ref, sl) - sa_b * row(kka_ref, sl) + v_b * row(k_ref, sl)
            r2_o[p] = jnp.where(sel, r2_b, r2_o[p])
            sa_o[p] = jnp.where(sel, sa_b, sa_o[p])

    if tb % SUBLANES == 0:
        def group(gi, carry):
            base = pl.multiple_of(gi * SUBLANES, SUBLANES)
            for jj in range(SUBLANES):
                token_step(lambda ref, sl, jj=jj: ref[pl.ds(base, SUBLANES), sl][jj:jj + 1, :],
                           lane == gi * SUBLANES + jj)
            return carry

        lax.fori_loop(0, tb // SUBLANES, group, 0)
    else:
        for tt in range(tb):
            token_step(lambda ref, sl, tt=tt: ref[tt:tt + 1, sl], lane == tt)

    for p in range(n_pair):
        sl = slice(p * LANES, (p + 1) * LANES)
        zt = jnp.concatenate([r2_o[p], sa_o[p]], axis=0).T
        r2 = jnp.concatenate([zt[0:tb, 0:n], zt[n:n + tb, 0:n]], axis=1)
        sa = jnp.concatenate([zt[0:tb, n:2 * n], zt[n:n + tb, n:2 * n]], axis=1)
        v = vhi_ref[:, sl] + vlo_ref[:, sl]
        y_ref[:, sl] = r2 - sa * c1_ref[:, sl] + v * c2_ref[:, sl]

    @pl.when(pl.program_id(1) == pl.num_programs(1) - 1)
    def _():
        so_ref[...] = s_scr[...]


def rwkv_scan(vecs, tok, y_spec, y_shape, B, nt, s0_pair, rhs, md):
    n_pair = s0_pair.shape[1]
    st = pl.BlockSpec((None, n_pair, HEAD_B, LANES), lambda b, t: (b, 0, 0, 0))
    acc = pltpu.VMEM((n_pair, HEAD_B, LANES), jnp.float32)
    return pl.pallas_call(
        functools.partial(_rwkv_scan_kernel, n_pair=n_pair),
        grid=(B, nt),
        in_specs=[tok] * 9 + [st, pl.BlockSpec(rhs.shape, lambda b, t: (0, 0)),
                              pl.BlockSpec(md.shape, lambda b, t: (0, 0))],
        out_specs=[y_spec, st],
        out_shape=[y_shape, jax.ShapeDtypeStruct(s0_pair.shape, jnp.float32)],
        scratch_shapes=[acc, acc, acc],
        compiler_params=_cparams("parallel", "arbitrary"),
        name="rwkv_scan",
    )(*vecs, s0_pair, rhs, md)


def _mix_out_kernel(oa_ref, go_ref, hg_ref, y_ref, vhi_ref, vlo_ref, g_ref, bon_ref,
                    lnw_ref, lnb_ref, bones_ref, o_ref, *, d_a):
    oa = oa_ref[...]
    outs = []
    for h in range(d_a // DK_A):
        x = oa[:, h * DK_A:(h + 1) * DK_A]
        outs.append(x * lax.rsqrt(jnp.mean(x * x, axis=-1, keepdims=True) + NORM_EPS))
    o_a = jnp.concatenate(outs, axis=1) * hg_ref[...] * _silu(go_ref[...])
    bones = bones_ref[...]
    v = vhi_ref[...] + vlo_ref[...]
    y = y_ref[...]
    inv_n = 1.0 / HEAD_B
    mean = _seg_sum(y, bones) * inv_n
    d = y - mean
    var = _seg_sum(d * d, bones) * inv_n
    yn = d * lax.rsqrt(var + RWKV_GN_EPS) * lnw_ref[...] + lnb_ref[...]
    o_b = (yn + bon_ref[...] * v) * g_ref[...]
    o_ref[:, 0:d_a] = o_a
    o_ref[:, d_a:] = o_b


def mix_out(oa, pa, hg_row, y, vhi, vlo, g, bon, ln_w, ln_b, bones):
    T, d_a = oa.shape
    d_b = y.shape[1]
    tm = _tile(T, 272)
    tok = lambda n: pl.BlockSpec((tm, n), lambda i: (i, 0))
    row = lambda n: pl.BlockSpec((1, n), lambda i: (0, 0))
    return pl.pallas_call(
        functools.partial(_mix_out_kernel, d_a=d_a),
        grid=(T // tm,),
        in_specs=[tok(d_a), pl.BlockSpec((tm, d_a), lambda i: (i, 3)), row(d_a)]
                 + [tok(d_b)] * 5 + [row(d_b), row(d_b), pl.BlockSpec(bones.shape, lambda i: (0, 0))],
        out_specs=tok(d_a + d_b),
        out_shape=jax.ShapeDtypeStruct((T, d_a + d_b), jnp.float32),
        compiler_params=_cparams("parallel"),
        name="mix_out",
    )(oa, pa, hg_row, y, vhi, vlo, g, bon, ln_w.reshape(1, d_b), ln_b.reshape(1, d_b), bones)


def _conv_u_kernel(gc_ref, h_ref, u_ref):
    u_ref[...] = gc_ref[...] * h_ref[...]


def conv_u(cp, d):
    T = cp.shape[0]
    tm = _tile(T, 544)
    return pl.pallas_call(
        _conv_u_kernel,
        grid=(T // tm,),
        in_specs=[pl.BlockSpec((tm, d), lambda i: (i, 1)), pl.BlockSpec((tm, d), lambda i: (i, 2))],
        out_specs=pl.BlockSpec((tm, d), lambda i: (i, 0)),
        out_shape=jax.ShapeDtypeStruct((T, d), jnp.float32),
        compiler_params=_cparams("parallel"),
        name="conv_u",
    )(cp, cp)


def _conv_mm_kernel(gb_ref, u_ref, u1_ref, u2_ref, ck_ref, w_ref, r_ref, o_ref, z_ref):
    @pl.when(pl.program_id(1) == 0)
    def _():
        y = ck_ref[0:1, :] * u2_ref[...]
        y = y + ck_ref[1:2, :] * u1_ref[...]
        y = y + ck_ref[2:3, :] * u_ref[...]
        z_ref[...] = gb_ref[...] * y

    o_ref[...] = r_ref[...] + jnp.dot(z_ref[...], w_ref[...], preferred_element_type=jnp.float32)


def conv_mm(cp, u, u1, u2, conv_k, w, res, tn=512):
    T, d = u.shape
    N = w.shape[1]
    assert conv_k.shape[0] == 3
    tm = _tile(T, 272)
    tn = _tile(N, tn, LANES)
    tok = pl.BlockSpec((tm, d), lambda i, j: (i, 0))
    return pl.pallas_call(
        _conv_mm_kernel,
        grid=(T // tm, N // tn),
        in_specs=[tok, tok, tok, tok,
                  pl.BlockSpec(conv_k.shape, lambda i, j: (0, 0)),
                  pl.BlockSpec((d, tn), lambda i, j: (0, j)),
                  pl.BlockSpec((tm, tn), lambda i, j: (i, j))],
        out_specs=pl.BlockSpec((tm, tn), lambda i, j: (i, j)),
        out_shape=jax.ShapeDtypeStruct((T, N), jnp.float32),
        scratch_shapes=[pltpu.VMEM((tm, d), jnp.float32)],
        compiler_params=_cparams("parallel", "arbitrary"),
        name="conv_mm",
    )(cp, u, u1, u2, conv_k, w, res)


def _shift_rows(x, groups, states, k):
    outs = []
    row = 0
    for (B, L), st in zip(groups, states):
        xg = x[row:row + B * L].reshape(B, L, -1)
        outs.append(jnp.concatenate([st[:, st.shape[1] - k:], xg[:, :L - k]], axis=1).reshape(B * L, -1))
        row += B * L
    return jnp.concatenate(outs, axis=0)


def _last_rows(x, groups, k):
    outs = []
    row = 0
    for B, L in groups:
        outs.append(x[row:row + B * L].reshape(B, L, -1)[:, L - k:])
        row += B * L
    return outs


def _pair_state(s):
    B, H, n, _ = s.shape
    return s.reshape(B, H // 2, 2, n, n).transpose(0, 1, 3, 2, 4).reshape(B, H // 2, n, 2 * n)


def _unpair_state(s):
    B, hp, n, _ = s.shape
    return s.reshape(B, hp, n, 2, n).transpose(0, 1, 3, 2, 4).reshape(B, 2 * hp, n, n)


def kernel(x_prompt, x_sample, state_hgrn, state_rwkv, state_shift, state_conv, norm_mix, w_in_ab,
           hgrn_lb_logits, hgrn_norm, rwkv_mu, rwkv_w0, rwkv_w2, rwkv_a0, rwkv_a2, rwkv_g2, rwkv_k_k,
           rwkv_k_a, rwkv_r_k, rwkv_ln_w, rwkv_ln_b, w_out_ab, norm_ffn, ffn_w1, ffn_w3, ffn_w2,
           conv_w_in, conv_k, conv_w_out, moe_router, moe_w1, moe_w3, moe_w2, norm_final):
    f32 = jnp.float32
    Bp, Lp, D = x_prompt.shape
    Bs, Ls, _ = x_sample.shape
    Tp, Ts = Bp * Lp, Bs * Ls
    groups = ((Bp, Lp), (Bs, Ls))
    n_even = w_in_ab.shape[0]
    n_odd = conv_w_in.shape[0]
    depth = n_even + n_odd
    H_A, dk = state_hgrn.shape[2], state_hgrn.shape[3]
    d_a = H_A * dk
    H_B, hb = state_rwkv.shape[2], state_rwkv.shape[3]
    d_b = H_B * hb
    d_shift = state_shift.shape[-1]
    lora_w = rwkv_w2.shape[1]
    lora_a = rwkv_a2.shape[1]
    assert hb == HEAD_B and dk == DK_A and lora_w + lora_a == LANES and H_B % 2 == 0
    assert w_in_ab.shape[2] == 4 * d_a + d_shift

    seg = jnp.arange(LANES) // HEAD_B
    bones = (seg[:, None] == seg[None, :]).astype(f32)
    seg2 = jnp.arange(2 * LANES) // HEAD_B
    rhs_scan = (seg2[:, None] == seg2[None, :]).astype(f32)
    md = (jnp.arange(LANES)[None, :] % HEAD_B == jnp.arange(HEAD_B)[:, None]).astype(f32)

    x = jnp.concatenate([x_prompt.reshape(Tp, D), x_sample.reshape(Ts, D)], axis=0)
    new_h, new_r, new_s, new_c = ([], []), ([], []), ([], []), ([], [])

    for layer in range(depth):
        j = layer // 2
        if layer % 2 == 0:
            w_in = w_in_ab[j]
            tn_u = _tile(d_shift, 512, LANES)
            assert (4 * d_a) % tn_u == 0
            pa = rms_mm(x, norm_mix[layer], w_in, tn=512, n_cols=4 * d_a)
            u = rms_mm(x, norm_mix[layer], w_in, tn=tn_u, col0=4 * d_a)
            zeros_h = jnp.zeros((Bp,) + state_hgrn.shape[2:], f32)
            tl = _tile(Lp, 64)
            oa_p, sh_p = hgrn_scan(pa, Bp, Lp, tl, min(32, tl), hgrn_lb_logits, zeros_h, j)
            lpad = -(-Ls // SUBLANES) * SUBLANES
            pa_s = jnp.pad(pa[Tp:, :3 * d_a].reshape(Bs, Ls, 3 * d_a), ((0, 0), (0, lpad - Ls), (0, 0)))
            oa_s, sh_s = hgrn_scan(pa_s.reshape(Bs * lpad, 3 * d_a), Bs, lpad, lpad, lpad, hgrn_lb_logits,
                                   state_hgrn[j], j, valid_len=Ls)
            oa = jnp.concatenate([oa_p, oa_s.reshape(Bs, lpad, d_a)[:, :Ls].reshape(Ts, d_a)], axis=0)
            new_h[0].append(sh_p)
            new_h[1].append(sh_s)
            zeros_s = jnp.zeros((Bp, 1, d_shift), f32)
            u_prev = _shift_rows(u, groups, (zeros_s, state_shift[j][:, None, :]), 1)
            w2p = jnp.concatenate([rwkv_w2[j], jnp.zeros((lora_a, d_b), f32)], axis=0)
            a2p = jnp.concatenate([jnp.zeros((lora_w, d_b), f32), rwkv_a2[j]], axis=0)
            prep = rwkv_prep(u, u_prev, rwkv_mu[j], rwkv_w0[j], rwkv_a0[j], rwkv_k_k[j], rwkv_k_a[j],
                             rwkv_r_k[j].reshape(d_b), w2p, a2p, rwkv_g2[j], bones, d_b)
            kk, wr, wdec, kka, kmod, vhi, vlo, gg, c1, c2, bon = prep
            scan_in = (kk, wr, wdec, kka, kmod, vhi, vlo, c1, c2)
            tb_p = _tile(Lp, HEAD_B)
            nt_p = Lp // tb_p
            zeros_r = jnp.zeros((Bp, H_B // 2, hb, 2 * hb), f32)
            tok_p = pl.BlockSpec((tb_p, d_b), lambda b, t: (b * nt_p + t, 0))
            y_p, sr_p = rwkv_scan(scan_in, tok_p, tok_p, jax.ShapeDtypeStruct((Tp, d_b), f32), Bp, nt_p,
                                  zeros_r, rhs_scan, md)
            tok_s = pl.BlockSpec((None, Ls, d_b), lambda b, t: (b, 0, 0))
            y_s, sr_s = rwkv_scan([a[Tp:].reshape(Bs, Ls, d_b) for a in scan_in], tok_s, tok_s,
                                  jax.ShapeDtypeStruct((Bs, Ls, d_b), f32), Bs, 1,
                                  _pair_state(state_rwkv[j]), rhs_scan, md)
            yb = jnp.concatenate([y_p, y_s.reshape(Ts, d_b)], axis=0)
            new_r[0].append(_unpair_state(sr_p))
            new_r[1].append(_unpair_state(sr_s))
            last_u = _last_rows(u, groups, 1)
            new_s[0].append(last_u[0][:, 0])
            new_s[1].append(last_u[1][:, 0])
            hg_row = jnp.tile(hgrn_norm[j], H_A).reshape(1, d_a)
            o = mix_out(oa, pa, hg_row, yb, vhi, vlo, gg, bon, rwkv_ln_w[j], rwkv_ln_b[j], bones)
            x = mm_res(o, w_out_ab[j], x)
            x = ffn(x, norm_ffn[layer], ffn_w1[j], ffn_w3[j], ffn_w2[j])
        else:
            d_c = state_conv.shape[-1]
            cw = state_conv.shape[2]
            cp = rms_mm(x, norm_mix[layer], conv_w_in[j], tn=512)
            uc = conv_u(cp, d_c)
            conv_states = (jnp.zeros((Bp, cw, d_c), f32), state_conv[j])
            u1 = _shift_rows(uc, groups, conv_states, 1)
            u2 = _shift_rows(uc, groups, conv_states, 2)
            last_c = _last_rows(uc, groups, cw)
            new_c[0].append(last_c[0])
            new_c[1].append(last_c[1])
            x = conv_mm(cp, uc, u1, u2, conv_k[j], conv_w_out[j], x)
            x = moe_top2(x, norm_ffn[layer], moe_router[j], moe_w1[j], moe_w3[j], moe_w2[j],
                         final_gain=norm_final if layer == depth - 1 else None)

    y = x if depth % 2 == 0 else rms_only(x, norm_final)
    y_prompt = y[:Tp].reshape(Bp, Lp, D)
    y_sample = y[Tp:].reshape(Bs, Ls, D)
    outs = [y_prompt, y_sample]
    for g in (0, 1):
        outs += [jnp.stack(new_h[g]), jnp.stack(new_r[g]), jnp.stack(new_s[g]), jnp.stack(new_c[g])]
    return tuple(outs)
```

```python
import functools
import math

import jax
import jax.numpy as jnp
from jax import lax
from jax.experimental import pallas as pl
from jax.experimental.pallas import tpu as pltpu

NORM_EPS = 1e-6
RWKV_GN_EPS = 64e-5
HEAD_B = 64
DK_A = 128
LANES = 128
SUBLANES = 8
VMEM_LIMIT = 56 * 1024 * 1024


def _cparams(*sem):
    return pltpu.CompilerParams(dimension_semantics=sem, vmem_limit_bytes=VMEM_LIMIT)


def _tile(n, cap, mult=SUBLANES):
    best = None
    for d in range(mult, min(n, cap) + 1, mult):
        if n % d == 0:
            best = d
    return n if best is None else best


def _rms(x, gain):
    return x * lax.rsqrt(jnp.mean(x * x, axis=-1, keepdims=True) + NORM_EPS) * gain


def _silu(x):
    return x * jax.nn.sigmoid(x)


def _mxu_dot(a, w):
    return jnp.dot(a.astype(jnp.bfloat16), w.astype(jnp.bfloat16), preferred_element_type=jnp.float32)


def _split_bf16(x):
    hi = x.astype(jnp.bfloat16).astype(jnp.float32)
    return hi, x - hi


def _rms_mm_kernel(x_ref, g_ref, w_ref, o_ref, xn_ref):
    @pl.when(pl.program_id(1) == 0)
    def _():
        xn_ref[...] = _rms(x_ref[...], g_ref[...]).astype(xn_ref.dtype)

    o_ref[...] = _mxu_dot(xn_ref[...], w_ref[...])


def rms_mm(x, gain, w, tn, col0=0, n_cols=None):
    T, K = x.shape
    N = w.shape[1] - col0 if n_cols is None else n_cols
    assert col0 % tn == 0 and N % tn == 0
    tm = _tile(T, 1088)
    cb0 = col0 // tn
    return pl.pallas_call(
        _rms_mm_kernel,
        grid=(T // tm, N // tn),
        in_specs=[pl.BlockSpec((tm, K), lambda i, j: (i, 0)),
                  pl.BlockSpec((1, K), lambda i, j: (0, 0)),
                  pl.BlockSpec((K, tn), lambda i, j: (0, cb0 + j))],
        out_specs=pl.BlockSpec((tm, tn), lambda i, j: (i, j)),
        out_shape=jax.ShapeDtypeStruct((T, N), jnp.float32),
        scratch_shapes=[pltpu.VMEM((tm, K), jnp.bfloat16)],
        compiler_params=_cparams("parallel", "arbitrary"),
        name="rms_mm",
    )(x, gain.reshape(1, K), w)


def _mm_res_kernel(a_ref, w_ref, r_ref, o_ref):
    o_ref[...] = r_ref[...] + _mxu_dot(a_ref[...], w_ref[...])


def mm_res(a, w, res, tn=512):
    T, K = a.shape
    N = w.shape[1]
    tm = _tile(T, 1088)
    tn = _tile(N, tn, LANES)
    return pl.pallas_call(
        _mm_res_kernel,
        grid=(T // tm, N // tn),
        in_specs=[pl.BlockSpec((tm, K), lambda i, j: (i, 0)),
                  pl.BlockSpec((K, tn), lambda i, j: (0, j)),
                  pl.BlockSpec((tm, tn), lambda i, j: (i, j))],
        out_specs=pl.BlockSpec((tm, tn), lambda i, j: (i, j)),
        out_shape=jax.ShapeDtypeStruct((T, N), jnp.float32),
        compiler_params=_cparams("parallel", "parallel"),
        name="mm_res",
    )(a, w, res)


def _rms_only_kernel(x_ref, g_ref, o_ref):
    o_ref[...] = _rms(x_ref[...], g_ref[...])


def rms_only(x, gain):
    T, K = x.shape
    tm = _tile(T, 544)
    return pl.pallas_call(
        _rms_only_kernel,
        grid=(T // tm,),
        in_specs=[pl.BlockSpec((tm, K), lambda i: (i, 0)),
                  pl.BlockSpec((1, K), lambda i: (0, 0))],
        out_specs=pl.BlockSpec((tm, K), lambda i: (i, 0)),
        out_shape=jax.ShapeDtypeStruct((T, K), jnp.float32),
        compiler_params=_cparams("parallel"),
        name="rms_final",
    )(x, gain.reshape(1, K))


def _ffn_kernel(x_ref, g_ref, w1_ref, w3_ref, w2_ref, o_ref, xn_ref):
    j = pl.program_id(1)

    @pl.when(j == 0)
    def _():
        xn_ref[...] = _rms(x_ref[...], g_ref[...]).astype(xn_ref.dtype)
        o_ref[...] = jnp.zeros_like(o_ref)

    xn = xn_ref[...]
    h1 = _mxu_dot(xn, w1_ref[...])
    h3 = _mxu_dot(xn, w3_ref[...])
    o_ref[...] += _mxu_dot(_silu(h1) * h3, w2_ref[...])

    @pl.when(j == pl.num_programs(1) - 1)
    def _():
        o_ref[...] = x_ref[...] + o_ref[...]


def ffn(x, gain, w1, w3, w2, tf=256):
    T, D = x.shape
    F = w1.shape[1]
    tm = _tile(T, 1088)
    one = pl.Buffered(1)
    return pl.pallas_call(
        _ffn_kernel,
        grid=(T // tm, F // tf),
        in_specs=[pl.BlockSpec((tm, D), lambda i, j: (i, 0), pipeline_mode=one),
                  pl.BlockSpec((1, D), lambda i, j: (0, 0)),
                  pl.BlockSpec((D, tf), lambda i, j: (0, j)),
                  pl.BlockSpec((D, tf), lambda i, j: (0, j)),
                  pl.BlockSpec((tf, D), lambda i, j: (j, 0))],
        out_specs=pl.BlockSpec((tm, D), lambda i, j: (i, 0), pipeline_mode=one),
        out_shape=jax.ShapeDtypeStruct((T, D), jnp.float32),
        scratch_shapes=[pltpu.VMEM((tm, D), jnp.bfloat16)],
        compiler_params=_cparams("parallel", "arbitrary"),
        name="ffn",
    )(x, gain.reshape(1, D), w1, w3, w2)


def _router_kernel(x_ref, g_ref, rt_ref, i1_ref, i2_ref, g1_ref, g2_ref, *, n_exp):
    xn = _rms(x_ref[...], g_ref[...])
    logit = [jnp.sum(xn * rt_ref[e:e + 1, :], axis=-1, keepdims=True) for e in range(n_exp)]
    neg = jnp.float32(-jnp.inf)

    def top(vals):
        m = vals[0]
        for v in vals[1:]:
            m = jnp.maximum(m, v)
        idx = jnp.full(m.shape, n_exp, jnp.int32)
        for e in reversed(range(n_exp)):
            idx = jnp.where(vals[e] == m, e, idx)
        return m, idx

    m1, i1 = top(logit)
    rest = [jnp.where(i1 == e, neg, logit[e]) for e in range(n_exp)]
    m2, i2 = top(rest)
    d = jnp.exp(m2 - m1)
    i1_ref[...] = i1
    i2_ref[...] = i2
    g1_ref[...] = 1.0 / (1.0 + d)
    g2_ref[...] = d / (1.0 + d)


def router_top2(x, gain, router):
    T, D = x.shape
    E = router.shape[1]
    tm = _tile(T, 544)
    col = pl.BlockSpec((tm, 1), lambda i: (i, 0))
    return pl.pallas_call(
        functools.partial(_router_kernel, n_exp=E),
        grid=(T // tm,),
        in_specs=[pl.BlockSpec((tm, D), lambda i: (i, 0)),
                  pl.BlockSpec((1, D), lambda i: (0, 0)),
                  pl.BlockSpec((E, D), lambda i: (0, 0))],
        out_specs=[col, col, col, col],
        out_shape=[jax.ShapeDtypeStruct((T, 1), jnp.int32)] * 2 + [jax.ShapeDtypeStruct((T, 1), jnp.float32)] * 2,
        compiler_params=_cparams("parallel"),
        name="router",
    )(x, gain.reshape(1, D), router.T)


MOE_SUB = 256


def _moe_plan(i1, i2, n_exp, cap):
    T = i1.shape[0]
    n_tiles = (2 * T) // cap + n_exp
    e_a = jnp.concatenate([i1[:, 0], i2[:, 0]])
    tok = jnp.concatenate([jnp.arange(T, dtype=jnp.int32)] * 2)
    onehot = (e_a[:, None] == jnp.arange(n_exp, dtype=jnp.int32)[None, :]).astype(jnp.int32)
    csum = jnp.cumsum(onehot, axis=0)
    rank = jnp.sum((csum - onehot) * onehot, axis=1)
    counts = csum[-1]
    nt_e = (counts + cap - 1) // cap
    t_end = jnp.cumsum(nt_e)
    t_start = t_end - nt_e
    pos = (t_start[e_a] * cap + rank).astype(jnp.int32)
    n_used = t_end[-1]
    tiles = jnp.arange(n_tiles, dtype=jnp.int32)
    t_buf = jnp.minimum(tiles, n_used - 1).astype(jnp.int32)
    t_exp = jnp.minimum(jnp.sum((t_buf[:, None] >= t_end[None, :]).astype(jnp.int32), axis=1), n_exp - 1)
    t_rows = jnp.where(tiles < n_used, jnp.clip(counts[t_exp] - (tiles - t_start[t_exp]) * cap, 0, cap), 0)
    src = jnp.zeros((n_tiles * cap,), jnp.int32).at[pos].set(tok)
    return t_exp.astype(jnp.int32), t_rows.astype(jnp.int32), t_buf, src, pos[:T], pos[T:]


def _row_copy(src_hbm, row, dst, r, sem):
    return pltpu.make_async_copy(src_hbm.at[pl.ds(row, 1)], dst.at[pl.ds(r, 1)], sem)


def _gather_norm_kernel(src_ref, rows_ref, x_hbm, g_ref, o_ref, buf, sem, *, sub_per_tile):
    i = pl.program_id(0)
    n = pl.num_programs(0)
    tg = o_ref.shape[0]

    def active(step):
        return (step % sub_per_tile) * tg < rows_ref[step // sub_per_tile]

    def issue(step):
        slot = step % 2

        def one(r, c):
            _row_copy(x_hbm, src_ref[step * tg + r], buf.at[slot], r, sem.at[slot]).start()
            return c

        lax.fori_loop(0, tg, one, 0)

    @pl.when((i == 0) & active(0))
    def _():
        issue(0)

    nxt = jnp.minimum(i + 1, n - 1)

    @pl.when((i + 1 < n) & active(nxt))
    def _():
        issue(nxt)

    @pl.when(active(i))
    def _():
        slot = i % 2

        def wait(r, c):
            _row_copy(x_hbm, 0, buf.at[slot], r, sem.at[slot]).wait()
            return c

        lax.fori_loop(0, tg, wait, 0)
        o_ref[...] = _rms(buf[slot], g_ref[...]).astype(o_ref.dtype)

    @pl.when(jnp.logical_not(active(i)))
    def _():
        o_ref[...] = jnp.zeros_like(o_ref)


def gather_norm(x, gain, src, t_rows, cap):
    T, D = x.shape
    n_rows = src.shape[0]
    tg = MOE_SUB
    return pl.pallas_call(
        functools.partial(_gather_norm_kernel, sub_per_tile=cap // tg),
        grid_spec=pltpu.PrefetchScalarGridSpec(
            num_scalar_prefetch=2,
            grid=(n_rows // tg,),
            in_specs=[pl.BlockSpec(memory_space=pl.ANY),
                      pl.BlockSpec((1, D), lambda i, s, r: (0, 0))],
            out_specs=pl.BlockSpec((tg, D), lambda i, s, r: (i, 0)),
            scratch_shapes=[pltpu.VMEM((2, tg, D), jnp.float32), pltpu.SemaphoreType.DMA((2,))]),
        out_shape=jax.ShapeDtypeStruct((n_rows, D), jnp.bfloat16),
        compiler_params=_cparams("arbitrary"),
        name="moe_gather",
    )(src, t_rows, x, gain.reshape(1, D))


def _moe_kernel(te_ref, tr_ref, tb_ref, xg_ref, w1_ref, w3_ref, w2_ref, o_ref):
    k = pl.program_id(0)
    j = pl.program_id(1)
    rows = tr_ref[k]

    @pl.when(j == 0)
    def _():
        o_ref[...] = jnp.zeros_like(o_ref)

    @pl.when(rows > 0)
    def _():
        w1 = w1_ref[...].astype(jnp.bfloat16)
        w3 = w3_ref[...].astype(jnp.bfloat16)
        w2 = w2_ref[...].astype(jnp.bfloat16)

        def up(s):
            xs = xg_ref[pl.ds(pl.multiple_of(s * MOE_SUB, MOE_SUB), MOE_SUB), :]
            h1 = jnp.dot(xs, w1, preferred_element_type=jnp.float32)
            h3 = jnp.dot(xs, w3, preferred_element_type=jnp.float32)
            return (_silu(h1) * h3).astype(jnp.bfloat16)

        def down(s, h):
            r0 = pl.multiple_of(s * MOE_SUB, MOE_SUB)
            o_ref[pl.ds(r0, MOE_SUB), :] += jnp.dot(h, w2, preferred_element_type=jnp.float32)

        def body(s, h):
            h_next = up(s + 1)
            down(s, h)
            return h_next

        last = (rows + MOE_SUB - 1) // MOE_SUB - 1
        down(last, lax.fori_loop(0, last, body, up(0)))


def moe_experts(xg, t_exp, t_rows, t_buf, w1, w3, w2, cap, tf=256):
    n_rows, D = xg.shape
    E, _, F = w1.shape
    n_tiles = n_rows // cap
    nj = F // tf

    def jj(k, j, tr):
        return jnp.where(tr[k] > 0, j, nj - 1)

    one = pl.Buffered(1)
    return pl.pallas_call(
        _moe_kernel,
        grid_spec=pltpu.PrefetchScalarGridSpec(
            num_scalar_prefetch=3,
            grid=(n_tiles, nj),
            in_specs=[pl.BlockSpec((cap, D), lambda k, j, te, tr, tb: (tb[k], 0), pipeline_mode=one),
                      pl.BlockSpec((None, D, tf), lambda k, j, te, tr, tb: (te[k], 0, jj(k, j, tr))),
                      pl.BlockSpec((None, D, tf), lambda k, j, te, tr, tb: (te[k], 0, jj(k, j, tr))),
                      pl.BlockSpec((None, tf, D), lambda k, j, te, tr, tb: (te[k], jj(k, j, tr), 0))],
            out_specs=pl.BlockSpec((cap, D), lambda k, j, te, tr, tb: (k, 0), pipeline_mode=one)),
        out_shape=jax.ShapeDtypeStruct((n_rows, D), jnp.float32),
        compiler_params=_cparams("arbitrary", "arbitrary"),
        name="moe_experts",
    )(t_exp, t_rows, t_buf, xg, w1, w3, w2)


def _combine_kernel(p1_ref, p2_ref, x_ref, g1_ref, g2_ref, *rest, final_norm):
    if final_norm:
        gain_ref, y_hbm, o_ref, a_buf, b_buf, sem = rest
    else:
        y_hbm, o_ref, a_buf, b_buf, sem = rest
    i = pl.program_id(0)
    tc = x_ref.shape[0]

    def issue(step):
        slot = step % 2

        def one(r, c):
            _row_copy(y_hbm, p1_ref[step * tc + r], a_buf.at[slot], r, sem.at[0, slot]).start()
            _row_copy(y_hbm, p2_ref[step * tc + r], b_buf.at[slot], r, sem.at[1, slot]).start()
            return c

        lax.fori_loop(0, tc, one, 0)

    @pl.when(i == 0)
    def _():
        issue(0)

    @pl.when(i + 1 < pl.num_programs(0))
    def _():
        issue(i + 1)

    slot = i % 2

    def wait(r, c):
        _row_copy(y_hbm, 0, a_buf.at[slot], r, sem.at[0, slot]).wait()
        _row_copy(y_hbm, 0, b_buf.at[slot], r, sem.at[1, slot]).wait()
        return c

    lax.fori_loop(0, tc, wait, 0)
    y = x_ref[...] + (g1_ref[...] * a_buf[slot] + g2_ref[...] * b_buf[slot])
    o_ref[...] = _rms(y, gain_ref[...]) if final_norm else y


def moe_combine(x, yg, pos1, pos2, g1, g2, final_gain=None):
    T, D = x.shape
    tc = _tile(T, 256)
    final_norm = final_gain is not None
    tok = pl.BlockSpec((tc, D), lambda i, a, b: (i, 0))
    col = pl.BlockSpec((tc, 1), lambda i, a, b: (i, 0))
    in_specs = [tok, col, col] + ([pl.BlockSpec((1, D), lambda i, a, b: (0, 0))] if final_norm else [])
    in_specs.append(pl.BlockSpec(memory_space=pl.ANY))
    args = (x, g1, g2) + ((final_gain.reshape(1, D),) if final_norm else ()) + (yg,)
    return pl.pallas_call(
        functools.partial(_combine_kernel, final_norm=final_norm),
        grid_spec=pltpu.PrefetchScalarGridSpec(
            num_scalar_prefetch=2,
            grid=(T // tc,),
            in_specs=in_specs,
            out_specs=tok,
            scratch_shapes=[pltpu.VMEM((2, tc, D), jnp.float32), pltpu.VMEM((2, tc, D), jnp.float32),
                            pltpu.SemaphoreType.DMA((2, 2))]),
        out_shape=jax.ShapeDtypeStruct((T, D), jnp.float32),
        compiler_params=_cparams("arbitrary"),
        name="moe_combine",
    )(pos1, pos2, *args)


def moe_top2(x, gain, router, w1, w3, w2, final_gain=None):
    T, D = x.shape
    E = router.shape[1]
    i1, i2, g1, g2 = router_top2(x, gain, router)
    cap = -(-(2 * T * 21 // (20 * E)) // MOE_SUB) * MOE_SUB
    t_exp, t_rows, t_buf, src, pos1, pos2 = _moe_plan(i1, i2, E, cap)
    xg = gather_norm(x, gain, src, t_rows, cap)
    yg = moe_experts(xg, t_exp, t_rows, t_buf, w1, w3, w2, cap)
    return moe_combine(x, yg, pos1, pos2, g1, g2, final_gain)


def _cumsum_rows(x):
    n = x.shape[0]
    row = lax.broadcasted_iota(jnp.int32, x.shape, 0)
    s = 1
    while s < n:
        x = x + jnp.where(row >= s, pltpu.roll(x, s, 0), 0.0)
        s *= 2
    return x


def _hgrn_kernel(q_ref, f_ref, i_ref, lbl_ref, s0_ref, o_ref, so_ref, s_scr, *, chunk, layer_j, valid_len):
    c = pl.program_id(1)
    tl = q_ref.shape[0]
    n_head, dk, _ = s_scr.shape

    @pl.when(c == 0)
    def _():
        s_scr[...] = s0_ref[...]

    lg = lbl_ref[...]
    ex = jnp.exp(lg - jnp.max(lg, axis=0, keepdims=True))
    lb_all = jnp.sum(ex[0:layer_j + 1, :], axis=0, keepdims=True) / jnp.sum(ex, axis=0, keepdims=True)

    pad = LANES - chunk
    zpad = jnp.zeros((pad, dk), jnp.float32)
    row = lax.broadcasted_iota(jnp.int32, (chunk, LANES), 0)
    col = lax.broadcasted_iota(jnp.int32, (chunk, LANES), 1)
    causal = col <= row
    ones_cd = jnp.ones((chunk, dk), jnp.float32)
    tn_dims = (((0,), (0,)), ((), ()))
    nt_dims = (((1,), (1,)), ((), ()))

    for cc in range(tl // chunk):
        sl = slice(cc * chunk, (cc + 1) * chunk)
        for h in range(n_head):
            hs = slice(h * dk, (h + 1) * dk)
            lb = lb_all[:, hs]
            f = lb + (1.0 - lb) * jax.nn.sigmoid(f_ref[sl, hs])
            q = _silu(q_ref[sl, hs])
            v = i_ref[sl, hs]
            if valid_len is not None:
                t_idx = c * tl + cc * chunk + lax.broadcasted_iota(jnp.int32, (chunk, dk), 0)
                ok = t_idx < valid_len
                f = jnp.where(ok, f, 1.0)
                q = jnp.where(ok, q, 0.0)
            g = jnp.log(f)
            k = 1.0 - f
            b = _cumsum_rows(g)
            mid = chunk // 2 - 1
            bm = b[mid:mid + 1, :]
            bl = b[chunk - 1:chunk, :]
            qe = q * jnp.exp(b - bm)
            ke = jnp.concatenate([k * jnp.exp(bm - b), zpad], axis=0)
            vp = jnp.concatenate([v, zpad], axis=0)
            a = lax.dot_general(qe, ke, nt_dims, preferred_element_type=jnp.float32)
            a = jnp.where(causal, a, 0.0)
            s = s_scr[h]
            o = jnp.dot(a, vp, preferred_element_type=jnp.float32)
            o = o + jnp.dot(q * jnp.exp(b), s, preferred_element_type=jnp.float32)
            o_ref[sl, hs] = o
            kd = jnp.concatenate([k * jnp.exp(bl - b), zpad], axis=0)
            ghi, glo = _split_bf16(g)
            blc = (lax.dot_general(ghi, ones_cd, tn_dims, preferred_element_type=jnp.float32)
                   + lax.dot_general(glo, ones_cd, tn_dims, preferred_element_type=jnp.float32))
            s_scr[h] = jnp.exp(blc) * s + lax.dot_general(kd, vp, tn_dims, preferred_element_type=jnp.float32)

    @pl.when(c == pl.num_programs(1) - 1)
    def _():
        so_ref[...] = s_scr[...]


def hgrn_scan(pa, B, L, tl, chunk, lb_logits, s0, layer_j, valid_len=None):
    H, dk = s0.shape[1], s0.shape[2]
    d_a = H * dk
    nt = L // tl
    kern = functools.partial(_hgrn_kernel, chunk=chunk, layer_j=layer_j, valid_len=valid_len)

    def col(off):
        return pl.BlockSpec((tl, d_a), lambda b, c: (b * nt + c, off))

    st = pl.BlockSpec((None, H, dk, dk), lambda b, c: (b, 0, 0, 0))
    return pl.pallas_call(
        kern,
        grid=(B, nt),
        in_specs=[col(0), col(1), col(2), pl.BlockSpec(lb_logits.shape, lambda b, c: (0, 0)), st],
        out_specs=[pl.BlockSpec((tl, d_a), lambda b, c: (b * nt + c, 0)), st],
        out_shape=[jax.ShapeDtypeStruct((B * L, d_a), jnp.float32),
                   jax.ShapeDtypeStruct(s0.shape, jnp.float32)],
        scratch_shapes=[pltpu.VMEM((H, dk, dk), jnp.float32)],
        compiler_params=_cparams("parallel", "arbitrary"),
        name="hgrn_scan",
    )(pa, pa, pa, lb_logits, s0)


def _seg_sum(x, bones):
    outs = []
    for j in range(x.shape[1] // LANES):
        hi, lo = _split_bf16(x[:, j * LANES:(j + 1) * LANES])
        outs.append(jnp.dot(hi, bones, preferred_element_type=jnp.float32)
                    + jnp.dot(lo, bones, preferred_element_type=jnp.float32))
    return jnp.concatenate(outs, axis=1)


def _rwkv_prep_kernel(u_ref, halo_ref, e_ref, pos_ref, mu_ref, w0_ref, a0_ref, kk_ref, ka_ref, rk_ref,
                      w2_ref, a2_ref, g2_ref, bones_ref,
                      kk_o, wr_o, w_o, kka_o, km_o, vhi_o, vlo_o, g_o, c1_o, c2_o, bon_o,
                      *, d_b, first_state_tile):
    u = u_ref[...]
    has_state = pl.program_id(0) >= first_state_tile
    u_prev = _prev_in_seq(u, halo_ref[...], 1, pos_ref[...], jnp.where(has_state, e_ref[...], 0.0))
    us = u + (u_prev - u) * mu_ref[...]
    r = us[:, 0:d_b]
    k = us[:, d_b:2 * d_b]
    v = us[:, 2 * d_b:3 * d_b]
    wa = us[:, 3 * d_b:3 * d_b + LANES]
    gd = us[:, 3 * d_b + LANES:3 * d_b + 2 * LANES]
    bones = bones_ref[...]
    z = w0_ref[...] + jnp.dot(jnp.tanh(wa), w2_ref[...], preferred_element_type=jnp.float32)
    nz = -z
    softplus = jnp.maximum(nz, 0.0) + jnp.log(1.0 + jnp.exp(-jnp.abs(nz)))
    w_log = -softplus - 0.5
    decay = jnp.exp(-jnp.exp(w_log))
    a = jax.nn.sigmoid(a0_ref[...] + jnp.dot(wa, a2_ref[...], preferred_element_type=jnp.float32))
    g = jnp.dot(jax.nn.sigmoid(gd), g2_ref[...], preferred_element_type=jnp.float32)
    kk = k * kk_ref[...]
    kk = kk / jnp.maximum(jnp.sqrt(_seg_sum(kk * kk, bones)), 1e-12)
    kmod = k * (1.0 + (a - 1.0) * ka_ref[...])
    kka = kk * a
    vhi, vlo = _split_bf16(v)
    kk_o[...] = kk
    wr_o[...] = decay * r
    w_o[...] = decay
    kka_o[...] = kka
    km_o[...] = kmod
    vhi_o[...] = vhi
    vlo_o[...] = vlo
    g_o[...] = g
    c1_o[...] = _seg_sum(kka * r, bones)
    c2_o[...] = _seg_sum(kmod * r, bones)
    bon_o[...] = _seg_sum(r * kmod * rk_ref[...], bones)


def rwkv_prep(u, pos, e1, tm, n_zero_rows, mu, w0, a0, k_k, k_a, r_k, w2p, a2p, g2, bones, d_b):
    T, DU = u.shape
    row = lambda n: pl.BlockSpec((1, n), lambda i: (0, 0))
    full = lambda a: pl.BlockSpec(a.shape, lambda i: (0, 0))
    first, specs = _seq_tiles(tm, n_zero_rows)
    tok_u, halo_u, start_u = specs(DU)
    tok = pl.BlockSpec((tm, d_b), lambda i: (i, 0))
    return pl.pallas_call(
        functools.partial(_rwkv_prep_kernel, d_b=d_b, first_state_tile=first),
        grid=(T // tm,),
        in_specs=[tok_u, halo_u, start_u, pl.BlockSpec((tm, 1), lambda i: (i, 0)),
                  row(DU), row(d_b), row(d_b), row(d_b), row(d_b), row(d_b),
                  full(w2p), full(a2p), full(g2), full(bones)],
        out_specs=[tok] * 11,
        out_shape=[jax.ShapeDtypeStruct((T, d_b), jnp.float32)] * 11,
        compiler_params=_cparams("parallel"),
        name="rwkv_prep",
    )(u, u, e1, pos, mu.reshape(1, DU), w0.reshape(1, d_b), a0.reshape(1, d_b), k_k.reshape(1, d_b),
      k_a.reshape(1, d_b), r_k.reshape(1, d_b), w2p, a2p, g2, bones)


def _rwkv_scan_kernel(kk_ref, wr_ref, w_ref, kka_ref, k_ref, vhi_ref, vlo_ref, c1_ref, c2_ref,
                      s0_ref, rhs_ref, md_ref, y_ref, so_ref, s_scr, r2_o, sa_o, *, n_pair):
    tb = kk_ref.shape[-2]
    n = HEAD_B

    @pl.when(pl.program_id(1) == 0)
    def _():
        s_scr[...] = s0_ref[...]

    r2_o[...] = jnp.zeros_like(r2_o)
    sa_o[...] = jnp.zeros_like(sa_o)
    rhs = rhs_ref[...]
    md = md_ref[...]
    lane = lax.broadcasted_iota(jnp.int32, (n, LANES), 1) & (n - 1)

    def token_step(row, sel):
        parts = []
        for p in range(n_pair):
            sl = slice(p * LANES, (p + 1) * LANES)
            s = s_scr[p]
            p1 = s * row(kk_ref, sl)
            p2 = s * row(wr_ref, sl)
            dh = md * row(vhi_ref, sl)
            dl = md * row(vlo_ref, sl)
            parts.append(jnp.concatenate([p1, dh], axis=1))
            parts.append(jnp.concatenate([p2, dl], axis=1))
        out = jnp.dot(jnp.concatenate(parts, axis=0), rhs, preferred_element_type=jnp.float32)
        for p in range(n_pair):
            sl = slice(p * LANES, (p + 1) * LANES)
            top = out[2 * p * n:(2 * p + 1) * n]
            bot = out[(2 * p + 1) * n:(2 * p + 2) * n]
            sa_b = top[:, 0:LANES]
            r2_b = bot[:, 0:LANES]
            v_b = top[:, LANES:2 * LANES] + bot[:, LANES:2 * LANES]
            s = s_scr[p]
            s_scr[p] = s * row(w_ref, sl) - sa_b * row(kka_ref, sl) + v_b * row(k_ref, sl)
            r2_o[p] = jnp.where(sel, r2_b, r2_o[p])
            sa_o[p] = jnp.where(sel, sa_b, sa_o[p])

    if tb % SUBLANES == 0:
        def group(gi, carry):
            base = pl.multiple_of(gi * SUBLANES, SUBLANES)
            for jj in range(SUBLANES):
                token_step(lambda ref, sl, jj=jj: ref[pl.ds(base, SUBLANES), sl][jj:jj + 1, :],
                           lane == gi * SUBLANES + jj)
            return carry

        lax.fori_loop(0, tb // SUBLANES, group, 0)
    else:
        for tt in range(tb):
            token_step(lambda ref, sl, tt=tt: ref[tt:tt + 1, sl], lane == tt)

    for p in range(n_pair):
        sl = slice(p * LANES, (p + 1) * LANES)
        zt = jnp.concatenate([r2_o[p], sa_o[p]], axis=0).T
        r2 = jnp.concatenate([zt[0:tb, 0:n], zt[n:n + tb, 0:n]], axis=1)
        sa = jnp.concatenate([zt[0:tb, n:2 * n], zt[n:n + tb, n:2 * n]], axis=1)
        v = vhi_ref[:, sl] + vlo_ref[:, sl]
        y_ref[:, sl] = r2 - sa * c1_ref[:, sl] + v * c2_ref[:, sl]

    @pl.when(pl.program_id(1) == pl.num_programs(1) - 1)
    def _():
        so_ref[...] = s_scr[...]


def rwkv_scan(vecs, tok, y_spec, y_shape, B, nt, s0_pair, rhs, md):
    n_pair = s0_pair.shape[1]
    st = pl.BlockSpec((None, n_pair, HEAD_B, LANES), lambda b, t: (b, 0, 0, 0))
    acc = pltpu.VMEM((n_pair, HEAD_B, LANES), jnp.float32)
    return pl.pallas_call(
        functools.partial(_rwkv_scan_kernel, n_pair=n_pair),
        grid=(B, nt),
        in_specs=[tok] * 9 + [st, pl.BlockSpec(rhs.shape, lambda b, t: (0, 0)),
                              pl.BlockSpec(md.shape, lambda b, t: (0, 0))],
        out_specs=[y_spec, st],
        out_shape=[y_shape, jax.ShapeDtypeStruct(s0_pair.shape, jnp.float32)],
        scratch_shapes=[acc, acc, acc],
        compiler_params=_cparams("parallel", "arbitrary"),
        name="rwkv_scan",
    )(*vecs, s0_pair, rhs, md)


def _mix_out_kernel(oa_ref, go_ref, hg_ref, y_ref, vhi_ref, vlo_ref, g_ref, bon_ref,
                    lnw_ref, lnb_ref, bones_ref, o_ref, *, d_a):
    oa = oa_ref[...]
    outs = []
    for h in range(d_a // DK_A):
        x = oa[:, h * DK_A:(h + 1) * DK_A]
        outs.append(x * lax.rsqrt(jnp.mean(x * x, axis=-1, keepdims=True) + NORM_EPS))
    o_a = jnp.concatenate(outs, axis=1) * hg_ref[...] * _silu(go_ref[...])
    bones = bones_ref[...]
    v = vhi_ref[...] + vlo_ref[...]
    y = y_ref[...]
    inv_n = 1.0 / HEAD_B
    mean = _seg_sum(y, bones) * inv_n
    d = y - mean
    var = _seg_sum(d * d, bones) * inv_n
    yn = d * lax.rsqrt(var + RWKV_GN_EPS) * lnw_ref[...] + lnb_ref[...]
    o_b = (yn + bon_ref[...] * v) * g_ref[...]
    o_ref[:, 0:d_a] = o_a.astype(o_ref.dtype)
    o_ref[:, d_a:] = o_b.astype(o_ref.dtype)


def mix_out(oa, pa, hg_row, y, vhi, vlo, g, bon, ln_w, ln_b, bones):
    T, d_a = oa.shape
    d_b = y.shape[1]
    tm = _tile(T, 272)
    tok = lambda n: pl.BlockSpec((tm, n), lambda i: (i, 0))
    row = lambda n: pl.BlockSpec((1, n), lambda i: (0, 0))
    return pl.pallas_call(
        functools.partial(_mix_out_kernel, d_a=d_a),
        grid=(T // tm,),
        in_specs=[tok(d_a), pl.BlockSpec((tm, d_a), lambda i: (i, 3)), row(d_a)]
                 + [tok(d_b)] * 5 + [row(d_b), row(d_b), pl.BlockSpec(bones.shape, lambda i: (0, 0))],
        out_specs=tok(d_a + d_b),
        out_shape=jax.ShapeDtypeStruct((T, d_a + d_b), jnp.bfloat16),
        compiler_params=_cparams("parallel"),
        name="mix_out",
    )(oa, pa, hg_row, y, vhi, vlo, g, bon, ln_w.reshape(1, d_b), ln_b.reshape(1, d_b), bones)


def _conv_in_kernel(x_ref, g_ref, wb_ref, wc_ref, wh_ref, gb_ref, u_ref, xn_ref):
    @pl.when(pl.program_id(1) == 0)
    def _():
        xn_ref[...] = _rms(x_ref[...], g_ref[...]).astype(xn_ref.dtype)

    xn = xn_ref[...]
    gb_ref[...] = _mxu_dot(xn, wb_ref[...])
    u_ref[...] = _mxu_dot(xn, wc_ref[...]) * _mxu_dot(xn, wh_ref[...])


def conv_in(x, gain, w, tn=256):
    T, K = x.shape
    d = w.shape[1] // 3
    tm = _tile(T, 1088)
    nb = d // tn
    out = pl.BlockSpec((tm, tn), lambda i, j: (i, j))
    return pl.pallas_call(
        _conv_in_kernel,
        grid=(T // tm, nb),
        in_specs=[pl.BlockSpec((tm, K), lambda i, j: (i, 0)),
                  pl.BlockSpec((1, K), lambda i, j: (0, 0)),
                  pl.BlockSpec((K, tn), lambda i, j: (0, j)),
                  pl.BlockSpec((K, tn), lambda i, j: (0, nb + j)),
                  pl.BlockSpec((K, tn), lambda i, j: (0, 2 * nb + j))],
        out_specs=[out, out],
        out_shape=[jax.ShapeDtypeStruct((T, d), jnp.float32)] * 2,
        scratch_shapes=[pltpu.VMEM((tm, K), jnp.bfloat16)],
        compiler_params=_cparams("parallel", "arbitrary"),
        name="conv_in",
    )(x, gain.reshape(1, K), w, w, w)


def _prev_rows(tile, halo, k):
    rolled = pltpu.roll(tile, k, 0)
    row = lax.broadcasted_iota(jnp.int32, halo.shape, 0)
    head = jnp.where(row < k, pltpu.roll(halo, k, 0), rolled[0:SUBLANES])
    return jnp.concatenate([head, rolled[SUBLANES:]], axis=0)


def _prev_in_seq(tile, halo, k, pos, start_rows):
    return jnp.where(pos < k, start_rows, _prev_rows(tile, halo, k))


def _conv_mm_kernel(gb_ref, u_ref, halo_ref, pos_ref, e1_ref, e2_ref, ck_ref, w_ref, r_ref, o_ref, z_ref,
                    *, first_state_tile):
    @pl.when(pl.program_id(1) == 0)
    def _():
        u = u_ref[...]
        pos = pos_ref[...]
        has_state = pl.program_id(0) >= first_state_tile
        u1 = _prev_in_seq(u, halo_ref[...], 1, pos, jnp.where(has_state, e1_ref[...], 0.0))
        u2 = _prev_in_seq(u, halo_ref[...], 2, pos, jnp.where(has_state, e2_ref[...], 0.0))
        y = ck_ref[0:1, :] * u2
        y = y + ck_ref[1:2, :] * u1
        y = y + ck_ref[2:3, :] * u
        z_ref[...] = (gb_ref[...] * y).astype(z_ref.dtype)

    o_ref[...] = r_ref[...] + _mxu_dot(z_ref[...], w_ref[...])


def _seq_tiles(tm, n_zero_rows):
    first = n_zero_rows // tm
    hb = tm // SUBLANES

    def specs(n):
        return (pl.BlockSpec((tm, n), lambda i, *_: (i, 0)),
                pl.BlockSpec((SUBLANES, n), lambda i, *_: (jnp.maximum(i * hb - 1, 0), 0)),
                pl.BlockSpec((tm, n), lambda i, *_: (jnp.maximum(i - first, 0), 0)))

    return first, specs


def conv_mm(gb, u, pos, e1, e2, conv_k, w, res, tm, n_zero_rows, tn=512):
    T, d = u.shape
    N = w.shape[1]
    assert conv_k.shape[0] == 3
    tn = _tile(N, tn, LANES)
    first, specs = _seq_tiles(tm, n_zero_rows)
    tok, halo, start = specs(d)
    return pl.pallas_call(
        functools.partial(_conv_mm_kernel, first_state_tile=first),
        grid=(T // tm, N // tn),
        in_specs=[tok, tok, halo, pl.BlockSpec((tm, 1), lambda i, j: (i, 0)), start, start,
                  pl.BlockSpec(conv_k.shape, lambda i, j: (0, 0)),
                  pl.BlockSpec((d, tn), lambda i, j: (0, j)),
                  pl.BlockSpec((tm, tn), lambda i, j: (i, j))],
        out_specs=pl.BlockSpec((tm, tn), lambda i, j: (i, j)),
        out_shape=jax.ShapeDtypeStruct((T, N), jnp.float32),
        scratch_shapes=[pltpu.VMEM((tm, d), jnp.bfloat16)],
        compiler_params=_cparams("parallel", "arbitrary"),
        name="conv_mm",
    )(gb, u, u, pos, e1, e2, conv_k, w, res)


def _start_rows(state, L, k):
    B, ns, N = state.shape
    assert k <= min(L, ns)
    return jnp.zeros((B, L, N), state.dtype).at[:, :k].set(state[:, ns - k:]).reshape(B * L, N)


def _last_rows(x, groups, k):
    outs = []
    row = 0
    for B, L in groups:
        outs.append(x[row:row + B * L].reshape(B, L, -1)[:, L - k:])
        row += B * L
    return outs


def _pair_state(s):
    B, H, n, _ = s.shape
    return s.reshape(B, H // 2, 2, n, n).transpose(0, 1, 3, 2, 4).reshape(B, H // 2, n, 2 * n)


def _unpair_state(s):
    B, hp, n, _ = s.shape
    return s.reshape(B, hp, n, 2, n).transpose(0, 1, 3, 2, 4).reshape(B, 2 * hp, n, n)


def kernel(x_prompt, x_sample, state_hgrn, state_rwkv, state_shift, state_conv, norm_mix, w_in_ab,
           hgrn_lb_logits, hgrn_norm, rwkv_mu, rwkv_w0, rwkv_w2, rwkv_a0, rwkv_a2, rwkv_g2, rwkv_k_k,
           rwkv_k_a, rwkv_r_k, rwkv_ln_w, rwkv_ln_b, w_out_ab, norm_ffn, ffn_w1, ffn_w3, ffn_w2,
           conv_w_in, conv_k, conv_w_out, moe_router, moe_w1, moe_w3, moe_w2, norm_final):
    f32 = jnp.float32
    Bp, Lp, D = x_prompt.shape
    Bs, Ls, _ = x_sample.shape
    Tp, Ts = Bp * Lp, Bs * Ls
    groups = ((Bp, Lp), (Bs, Ls))
    n_even = w_in_ab.shape[0]
    n_odd = conv_w_in.shape[0]
    depth = n_even + n_odd
    H_A, dk = state_hgrn.shape[2], state_hgrn.shape[3]
    d_a = H_A * dk
    H_B, hb = state_rwkv.shape[2], state_rwkv.shape[3]
    d_b = H_B * hb
    d_shift = state_shift.shape[-1]
    lora_w = rwkv_w2.shape[1]
    lora_a = rwkv_a2.shape[1]
    assert hb == HEAD_B and dk == DK_A and lora_w + lora_a == LANES and H_B % 2 == 0
    assert w_in_ab.shape[2] == 4 * d_a + d_shift

    seg = jnp.arange(LANES) // HEAD_B
    bones = (seg[:, None] == seg[None, :]).astype(f32)
    seg2 = jnp.arange(2 * LANES) // HEAD_B
    rhs_scan = (seg2[:, None] == seg2[None, :]).astype(f32)
    md = (jnp.arange(LANES)[None, :] % HEAD_B == jnp.arange(HEAD_B)[:, None]).astype(f32)

    x = jnp.concatenate([x_prompt.reshape(Tp, D), x_sample.reshape(Ts, D)], axis=0)
    pos = jnp.concatenate([jnp.tile(jnp.arange(Lp, dtype=jnp.int32), Bp),
                           jnp.tile(jnp.arange(Ls, dtype=jnp.int32), Bs)]).reshape(Tp + Ts, 1)
    seq_tile = math.gcd(Tp, Ts)
    assert seq_tile % SUBLANES == 0
    new_h, new_r, new_s, new_c = ([], []), ([], []), ([], []), ([], [])

    for layer in range(depth):
        j = layer // 2
        if layer % 2 == 0:
            w_in = w_in_ab[j]
            tn_u = _tile(d_shift, 512, LANES)
            assert (4 * d_a) % tn_u == 0
            pa = rms_mm(x, norm_mix[layer], w_in, tn=512, n_cols=4 * d_a)
            u = rms_mm(x, norm_mix[layer], w_in, tn=tn_u, col0=4 * d_a)
            zeros_h = jnp.zeros((Bp,) + state_hgrn.shape[2:], f32)
            tl = _tile(Lp, 64)
            oa_p, sh_p = hgrn_scan(pa, Bp, Lp, tl, min(32, tl), hgrn_lb_logits, zeros_h, j)
            lpad = -(-Ls // SUBLANES) * SUBLANES
            pa_s = jnp.pad(pa[Tp:, :3 * d_a].reshape(Bs, Ls, 3 * d_a), ((0, 0), (0, lpad - Ls), (0, 0)))
            oa_s, sh_s = hgrn_scan(pa_s.reshape(Bs * lpad, 3 * d_a), Bs, lpad, lpad, lpad, hgrn_lb_logits,
                                   state_hgrn[j], j, valid_len=Ls)
            oa = jnp.concatenate([oa_p, oa_s.reshape(Bs, lpad, d_a)[:, :Ls].reshape(Ts, d_a)], axis=0)
            new_h[0].append(sh_p)
            new_h[1].append(sh_s)
            e_shift = _start_rows(state_shift[j][:, None, :], Ls, 1)
            w2p = jnp.concatenate([rwkv_w2[j], jnp.zeros((lora_a, d_b), f32)], axis=0)
            a2p = jnp.concatenate([jnp.zeros((lora_w, d_b), f32), rwkv_a2[j]], axis=0)
            prep = rwkv_prep(u, pos, e_shift, _tile(seq_tile, 256), Tp, rwkv_mu[j], rwkv_w0[j], rwkv_a0[j],
                             rwkv_k_k[j], rwkv_k_a[j], rwkv_r_k[j].reshape(d_b), w2p, a2p, rwkv_g2[j],
                             bones, d_b)
            kk, wr, wdec, kka, kmod, vhi, vlo, gg, c1, c2, bon = prep
            scan_in = (kk, wr, wdec, kka, kmod, vhi, vlo, c1, c2)
            tb_p = _tile(Lp, HEAD_B)
            nt_p = Lp // tb_p
            zeros_r = jnp.zeros((Bp, H_B // 2, hb, 2 * hb), f32)
            tok_p = pl.BlockSpec((tb_p, d_b), lambda b, t: (b * nt_p + t, 0))
            y_p, sr_p = rwkv_scan(scan_in, tok_p, tok_p, jax.ShapeDtypeStruct((Tp, d_b), f32), Bp, nt_p,
                                  zeros_r, rhs_scan, md)
            tok_s = pl.BlockSpec((None, Ls, d_b), lambda b, t: (b, 0, 0))
            y_s, sr_s = rwkv_scan([a[Tp:].reshape(Bs, Ls, d_b) for a in scan_in], tok_s, tok_s,
                                  jax.ShapeDtypeStruct((Bs, Ls, d_b), f32), Bs, 1,
                                  _pair_state(state_rwkv[j]), rhs_scan, md)
            yb = jnp.concatenate([y_p, y_s.reshape(Ts, d_b)], axis=0)
            new_r[0].append(_unpair_state(sr_p))
            new_r[1].append(_unpair_state(sr_s))
            last_u = _last_rows(u, groups, 1)
            new_s[0].append(last_u[0][:, 0])
            new_s[1].append(last_u[1][:, 0])
            hg_row = jnp.tile(hgrn_norm[j], H_A).reshape(1, d_a)
            o = mix_out(oa, pa, hg_row, yb, vhi, vlo, gg, bon, rwkv_ln_w[j], rwkv_ln_b[j], bones)
            x = mm_res(o, w_out_ab[j], x)
            x = ffn(x, norm_ffn[layer], ffn_w1[j], ffn_w3[j], ffn_w2[j])
        else:
            d_c = state_conv.shape[-1]
            cw = state_conv.shape[2]
            gb, uc = conv_in(x, norm_mix[layer], conv_w_in[j])
            last_c = _last_rows(uc, groups, cw)
            new_c[0].append(last_c[0])
            new_c[1].append(last_c[1])
            x = conv_mm(gb, uc, pos, _start_rows(state_conv[j], Ls, 1), _start_rows(state_conv[j], Ls, 2),
                        conv_k[j], conv_w_out[j], x, _tile(seq_tile, 512), Tp)
            x = moe_top2(x, norm_ffn[layer], moe_router[j], moe_w1[j], moe_w3[j], moe_w2[j],
                         final_gain=norm_final if layer == depth - 1 else None)

    y = x if depth % 2 == 0 else rms_only(x, norm_final)
    y_prompt = y[:Tp].reshape(Bp, Lp, D)
    y_sample = y[Tp:].reshape(Bs, Ls, D)
    outs = [y_prompt, y_sample]
    for g in (0, 1):
        outs += [jnp.stack(new_h[g]), jnp.stack(new_r[g]), jnp.stack(new_s[g]), jnp.stack(new_c[g])]
    return tuple(outs)
```

```python
import functools
import math

import jax
import jax.numpy as jnp
from jax import lax
from jax.experimental import pallas as pl
from jax.experimental.pallas import tpu as pltpu

NORM_EPS = 1e-6
RWKV_GN_EPS = 64e-5
HEAD_B = 64
DK_A = 128
LANES = 128
SUBLANES = 8
VMEM_LIMIT = 56 * 1024 * 1024


def _cparams(*sem):
    return pltpu.CompilerParams(dimension_semantics=sem, vmem_limit_bytes=VMEM_LIMIT)


def _tile(n, cap, mult=SUBLANES):
    best = None
    for d in range(mult, min(n, cap) + 1, mult):
        if n % d == 0:
            best = d
    return n if best is None else best


def _rms(x, gain):
    return x * lax.rsqrt(jnp.mean(x * x, axis=-1, keepdims=True) + NORM_EPS) * gain


def _silu(x):
    return x * jax.nn.sigmoid(x)


def _mxu_dot(a, w):
    return jnp.dot(a.astype(jnp.bfloat16), w.astype(jnp.bfloat16), preferred_element_type=jnp.float32)


def _split_bf16(x):
    hi = x.astype(jnp.bfloat16).astype(jnp.float32)
    return hi, x - hi


def _rms_mm_kernel(x_ref, g_ref, w_ref, o_ref, xn_ref):
    @pl.when(pl.program_id(1) == 0)
    def _():
        xn_ref[...] = _rms(x_ref[...], g_ref[...]).astype(xn_ref.dtype)

    o_ref[...] = _mxu_dot(xn_ref[...], w_ref[...])


def rms_mm(x, gain, w, tn, col0=0, n_cols=None):
    T, K = x.shape
    N = w.shape[1] - col0 if n_cols is None else n_cols
    assert col0 % tn == 0 and N % tn == 0
    tm = _tile(T, 1088)
    cb0 = col0 // tn
    return pl.pallas_call(
        _rms_mm_kernel,
        grid=(T // tm, N // tn),
        in_specs=[pl.BlockSpec((tm, K), lambda i, j: (i, 0)),
                  pl.BlockSpec((1, K), lambda i, j: (0, 0)),
                  pl.BlockSpec((K, tn), lambda i, j: (0, cb0 + j))],
        out_specs=pl.BlockSpec((tm, tn), lambda i, j: (i, j)),
        out_shape=jax.ShapeDtypeStruct((T, N), jnp.float32),
        scratch_shapes=[pltpu.VMEM((tm, K), jnp.bfloat16)],
        compiler_params=_cparams("parallel", "arbitrary"),
        name="rms_mm",
    )(x, gain.reshape(1, K), w)


def _mm_res_kernel(a_ref, w_ref, r_ref, o_ref):
    o_ref[...] = r_ref[...] + _mxu_dot(a_ref[...], w_ref[...])


def mm_res(a, w, res, tn=512):
    T, K = a.shape
    N = w.shape[1]
    tm = _tile(T, 1088)
    tn = _tile(N, tn, LANES)
    return pl.pallas_call(
        _mm_res_kernel,
        grid=(T // tm, N // tn),
        in_specs=[pl.BlockSpec((tm, K), lambda i, j: (i, 0)),
                  pl.BlockSpec((K, tn), lambda i, j: (0, j)),
                  pl.BlockSpec((tm, tn), lambda i, j: (i, j))],
        out_specs=pl.BlockSpec((tm, tn), lambda i, j: (i, j)),
        out_shape=jax.ShapeDtypeStruct((T, N), jnp.float32),
        compiler_params=_cparams("parallel", "parallel"),
        name="mm_res",
    )(a, w, res)


def _rms_only_kernel(x_ref, g_ref, o_ref):
    o_ref[...] = _rms(x_ref[...], g_ref[...])


def rms_only(x, gain):
    T, K = x.shape
    tm = _tile(T, 544)
    return pl.pallas_call(
        _rms_only_kernel,
        grid=(T // tm,),
        in_specs=[pl.BlockSpec((tm, K), lambda i: (i, 0)),
                  pl.BlockSpec((1, K), lambda i: (0, 0))],
        out_specs=pl.BlockSpec((tm, K), lambda i: (i, 0)),
        out_shape=jax.ShapeDtypeStruct((T, K), jnp.float32),
        compiler_params=_cparams("parallel"),
        name="rms_final",
    )(x, gain.reshape(1, K))


def _ffn_kernel(x_ref, g_ref, w1_ref, w3_ref, w2_ref, o_ref, xn_ref):
    j = pl.program_id(1)

    @pl.when(j == 0)
    def _():
        xn_ref[...] = _rms(x_ref[...], g_ref[...]).astype(xn_ref.dtype)
        o_ref[...] = jnp.zeros_like(o_ref)

    xn = xn_ref[...]
    h1 = _mxu_dot(xn, w1_ref[...])
    h3 = _mxu_dot(xn, w3_ref[...])
    o_ref[...] += _mxu_dot(_silu(h1) * h3, w2_ref[...])

    @pl.when(j == pl.num_programs(1) - 1)
    def _():
        o_ref[...] = x_ref[...] + o_ref[...]


def ffn(x, gain, w1, w3, w2, tf=256):
    T, D = x.shape
    F = w1.shape[1]
    tm = _tile(T, 1088)
    one = pl.Buffered(1)
    return pl.pallas_call(
        _ffn_kernel,
        grid=(T // tm, F // tf),
        in_specs=[pl.BlockSpec((tm, D), lambda i, j: (i, 0), pipeline_mode=one),
                  pl.BlockSpec((1, D), lambda i, j: (0, 0)),
                  pl.BlockSpec((D, tf), lambda i, j: (0, j)),
                  pl.BlockSpec((D, tf), lambda i, j: (0, j)),
                  pl.BlockSpec((tf, D), lambda i, j: (j, 0))],
        out_specs=pl.BlockSpec((tm, D), lambda i, j: (i, 0), pipeline_mode=one),
        out_shape=jax.ShapeDtypeStruct((T, D), jnp.float32),
        scratch_shapes=[pltpu.VMEM((tm, D), jnp.bfloat16)],
        compiler_params=_cparams("parallel", "arbitrary"),
        name="ffn",
    )(x, gain.reshape(1, D), w1, w3, w2)


def _router_kernel(x_ref, g_ref, rt_ref, i1_ref, i2_ref, g1_ref, g2_ref, *, n_exp):
    xn = _rms(x_ref[...], g_ref[...])
    logit = [jnp.sum(xn * rt_ref[e:e + 1, :], axis=-1, keepdims=True) for e in range(n_exp)]
    neg = jnp.float32(-jnp.inf)

    def top(vals):
        m = vals[0]
        for v in vals[1:]:
            m = jnp.maximum(m, v)
        idx = jnp.full(m.shape, n_exp, jnp.int32)
        for e in reversed(range(n_exp)):
            idx = jnp.where(vals[e] == m, e, idx)
        return m, idx

    m1, i1 = top(logit)
    rest = [jnp.where(i1 == e, neg, logit[e]) for e in range(n_exp)]
    m2, i2 = top(rest)
    d = jnp.exp(m2 - m1)
    i1_ref[...] = i1
    i2_ref[...] = i2
    g1_ref[...] = 1.0 / (1.0 + d)
    g2_ref[...] = d / (1.0 + d)


def router_top2(x, gain, router):
    T, D = x.shape
    E = router.shape[1]
    tm = _tile(T, 544)
    col = pl.BlockSpec((tm, 1), lambda i: (i, 0))
    return pl.pallas_call(
        functools.partial(_router_kernel, n_exp=E),
        grid=(T // tm,),
        in_specs=[pl.BlockSpec((tm, D), lambda i: (i, 0)),
                  pl.BlockSpec((1, D), lambda i: (0, 0)),
                  pl.BlockSpec((E, D), lambda i: (0, 0))],
        out_specs=[col, col, col, col],
        out_shape=[jax.ShapeDtypeStruct((T, 1), jnp.int32)] * 2 + [jax.ShapeDtypeStruct((T, 1), jnp.float32)] * 2,
        compiler_params=_cparams("parallel"),
        name="router",
    )(x, gain.reshape(1, D), router.T)


MOE_SUB = 256


def _moe_plan(i1, i2, n_exp, cap):
    T = i1.shape[0]
    n_tiles = (2 * T) // cap + n_exp
    e_a = jnp.concatenate([i1[:, 0], i2[:, 0]])
    tok = jnp.concatenate([jnp.arange(T, dtype=jnp.int32)] * 2)
    onehot = (e_a[:, None] == jnp.arange(n_exp, dtype=jnp.int32)[None, :]).astype(jnp.int32)
    csum = jnp.cumsum(onehot, axis=0)
    rank = jnp.sum((csum - onehot) * onehot, axis=1)
    counts = csum[-1]
    nt_e = (counts + cap - 1) // cap
    t_end = jnp.cumsum(nt_e)
    t_start = t_end - nt_e
    pos = (t_start[e_a] * cap + rank).astype(jnp.int32)
    n_used = t_end[-1]
    tiles = jnp.arange(n_tiles, dtype=jnp.int32)
    t_buf = jnp.minimum(tiles, n_used - 1).astype(jnp.int32)
    t_exp = jnp.minimum(jnp.sum((t_buf[:, None] >= t_end[None, :]).astype(jnp.int32), axis=1), n_exp - 1)
    t_rows = jnp.where(tiles < n_used, jnp.clip(counts[t_exp] - (tiles - t_start[t_exp]) * cap, 0, cap), 0)
    src = jnp.zeros((n_tiles * cap,), jnp.int32).at[pos].set(tok)
    return t_exp.astype(jnp.int32), t_rows.astype(jnp.int32), t_buf, src, pos[:T], pos[T:]


DMA_UNROLL = 8


def _row_copy(src_hbm, row, dst, r, sem):
    return pltpu.make_async_copy(src_hbm.at[pl.ds(row, 1)], dst.at[pl.ds(r, 1)], sem)


def _wait_rows(dst, sem):
    pltpu.make_async_copy(dst, dst, sem).wait()


def _gather_norm_kernel(src_ref, rows_ref, x_hbm, g_ref, o_ref, buf, sem, *, sub_per_tile):
    i = pl.program_id(0)
    n = pl.num_programs(0)
    tg = o_ref.shape[0]

    def active(step):
        return (step % sub_per_tile) * tg < rows_ref[step // sub_per_tile]

    def issue(step):
        slot = step % 2

        def one(r, c):
            _row_copy(x_hbm, src_ref[step * tg + r], buf.at[slot], r, sem.at[slot]).start()
            return c

        lax.fori_loop(0, tg, one, 0, unroll=DMA_UNROLL)

    @pl.when((i == 0) & active(0))
    def _():
        issue(0)

    nxt = jnp.minimum(i + 1, n - 1)

    @pl.when((i + 1 < n) & active(nxt))
    def _():
        issue(nxt)

    @pl.when(active(i))
    def _():
        slot = i % 2
        _wait_rows(buf.at[slot], sem.at[slot])
        o_ref[...] = _rms(buf[slot], g_ref[...]).astype(o_ref.dtype)

    @pl.when(jnp.logical_not(active(i)))
    def _():
        o_ref[...] = jnp.zeros_like(o_ref)


def gather_norm(x, gain, src, t_rows, cap):
    T, D = x.shape
    n_rows = src.shape[0]
    tg = MOE_SUB
    return pl.pallas_call(
        functools.partial(_gather_norm_kernel, sub_per_tile=cap // tg),
        grid_spec=pltpu.PrefetchScalarGridSpec(
            num_scalar_prefetch=2,
            grid=(n_rows // tg,),
            in_specs=[pl.BlockSpec(memory_space=pl.ANY),
                      pl.BlockSpec((1, D), lambda i, s, r: (0, 0))],
            out_specs=pl.BlockSpec((tg, D), lambda i, s, r: (i, 0)),
            scratch_shapes=[pltpu.VMEM((2, tg, D), jnp.float32), pltpu.SemaphoreType.DMA((2,))]),
        out_shape=jax.ShapeDtypeStruct((n_rows, D), jnp.bfloat16),
        compiler_params=_cparams("arbitrary"),
        name="moe_gather",
    )(src, t_rows, x, gain.reshape(1, D))


def _moe_kernel(te_ref, tr_ref, tb_ref, xg_ref, w1_ref, w3_ref, w2_ref, o_ref):
    k = pl.program_id(0)
    j = pl.program_id(1)
    rows = tr_ref[k]

    @pl.when(j == 0)
    def _():
        o_ref[...] = jnp.zeros_like(o_ref)

    @pl.when(rows > 0)
    def _():
        w1 = w1_ref[...].astype(jnp.bfloat16)
        w3 = w3_ref[...].astype(jnp.bfloat16)
        w2 = w2_ref[...].astype(jnp.bfloat16)

        def up(s):
            xs = xg_ref[pl.ds(pl.multiple_of(s * MOE_SUB, MOE_SUB), MOE_SUB), :]
            h1 = jnp.dot(xs, w1, preferred_element_type=jnp.float32)
            h3 = jnp.dot(xs, w3, preferred_element_type=jnp.float32)
            return (_silu(h1) * h3).astype(jnp.bfloat16)

        def down(s, h):
            r0 = pl.multiple_of(s * MOE_SUB, MOE_SUB)
            o_ref[pl.ds(r0, MOE_SUB), :] += jnp.dot(h, w2, preferred_element_type=jnp.float32)

        def body(s, h):
            h_next = up(s + 1)
            down(s, h)
            return h_next

        last = (rows + MOE_SUB - 1) // MOE_SUB - 1
        down(last, lax.fori_loop(0, last, body, up(0)))


def moe_experts(xg, t_exp, t_rows, t_buf, w1, w3, w2, cap, tf=256):
    n_rows, D = xg.shape
    E, _, F = w1.shape
    n_tiles = n_rows // cap
    nj = F // tf

    def jj(k, j, tr):
        return jnp.where(tr[k] > 0, j, nj - 1)

    one = pl.Buffered(1)
    return pl.pallas_call(
        _moe_kernel,
        grid_spec=pltpu.PrefetchScalarGridSpec(
            num_scalar_prefetch=3,
            grid=(n_tiles, nj),
            in_specs=[pl.BlockSpec((cap, D), lambda k, j, te, tr, tb: (tb[k], 0), pipeline_mode=one),
                      pl.BlockSpec((None, D, tf), lambda k, j, te, tr, tb: (te[k], 0, jj(k, j, tr))),
                      pl.BlockSpec((None, D, tf), lambda k, j, te, tr, tb: (te[k], 0, jj(k, j, tr))),
                      pl.BlockSpec((None, tf, D), lambda k, j, te, tr, tb: (te[k], jj(k, j, tr), 0))],
            out_specs=pl.BlockSpec((cap, D), lambda k, j, te, tr, tb: (k, 0), pipeline_mode=one)),
        out_shape=jax.ShapeDtypeStruct((n_rows, D), jnp.float32),
        compiler_params=_cparams("arbitrary", "arbitrary"),
        name="moe_experts",
    )(t_exp, t_rows, t_buf, xg, w1, w3, w2)


def _combine_kernel(p1_ref, p2_ref, x_ref, g1_ref, g2_ref, *rest, final_norm, n_first_tiles):
    if final_norm:
        gain_ref, y_hbm, oa_ref, ob_ref, a_buf, b_buf, sem = rest
    else:
        y_hbm, oa_ref, ob_ref, a_buf, b_buf, sem = rest
    i = pl.program_id(0)
    tc = x_ref.shape[0]

    def issue(step):
        slot = step % 2

        def one(r, c):
            _row_copy(y_hbm, p1_ref[step * tc + r], a_buf.at[slot], r, sem.at[0, slot]).start()
            _row_copy(y_hbm, p2_ref[step * tc + r], b_buf.at[slot], r, sem.at[1, slot]).start()
            return c

        lax.fori_loop(0, tc, one, 0, unroll=DMA_UNROLL)

    @pl.when(i == 0)
    def _():
        issue(0)

    @pl.when(i + 1 < pl.num_programs(0))
    def _():
        issue(i + 1)

    slot = i % 2
    _wait_rows(a_buf.at[slot], sem.at[0, slot])
    _wait_rows(b_buf.at[slot], sem.at[1, slot])
    y = x_ref[...] + (g1_ref[...] * a_buf[slot] + g2_ref[...] * b_buf[slot])
    y = _rms(y, gain_ref[...]) if final_norm else y

    @pl.when(i < n_first_tiles)
    def _():
        oa_ref[...] = y

    @pl.when(i >= n_first_tiles)
    def _():
        ob_ref[...] = y


def moe_combine(x, yg, pos1, pos2, g1, g2, n_first, final_gain=None):
    T, D = x.shape
    tc = _tile(math.gcd(n_first, T - n_first), 256)
    nf = n_first // tc
    final_norm = final_gain is not None
    tok = pl.BlockSpec((tc, D), lambda i, a, b: (i, 0))
    col = pl.BlockSpec((tc, 1), lambda i, a, b: (i, 0))
    in_specs = [tok, col, col] + ([pl.BlockSpec((1, D), lambda i, a, b: (0, 0))] if final_norm else [])
    in_specs.append(pl.BlockSpec(memory_space=pl.ANY))
    args = (x, g1, g2) + ((final_gain.reshape(1, D),) if final_norm else ()) + (yg,)
    return pl.pallas_call(
        functools.partial(_combine_kernel, final_norm=final_norm, n_first_tiles=nf),
        grid_spec=pltpu.PrefetchScalarGridSpec(
            num_scalar_prefetch=2,
            grid=(T // tc,),
            in_specs=in_specs,
            out_specs=[pl.BlockSpec((tc, D), lambda i, a, b: (jnp.minimum(i, nf - 1), 0)),
                       pl.BlockSpec((tc, D), lambda i, a, b: (jnp.maximum(i - nf, 0), 0))],
            scratch_shapes=[pltpu.VMEM((2, tc, D), jnp.float32), pltpu.VMEM((2, tc, D), jnp.float32),
                            pltpu.SemaphoreType.DMA((2, 2))]),
        out_shape=[jax.ShapeDtypeStruct((n_first, D), jnp.float32),
                   jax.ShapeDtypeStruct((T - n_first, D), jnp.float32)],
        compiler_params=_cparams("arbitrary"),
        name="moe_combine",
    )(pos1, pos2, *args)


def moe_top2(x, gain, router, w1, w3, w2, n_first, final_gain=None):
    T, D = x.shape
    E = router.shape[1]
    i1, i2, g1, g2 = router_top2(x, gain, router)
    cap = -(-(2 * T * 21 // (20 * E)) // MOE_SUB) * MOE_SUB
    t_exp, t_rows, t_buf, src, pos1, pos2 = _moe_plan(i1, i2, E, cap)
    xg = gather_norm(x, gain, src, t_rows, cap)
    yg = moe_experts(xg, t_exp, t_rows, t_buf, w1, w3, w2, cap)
    return moe_combine(x, yg, pos1, pos2, g1, g2, n_first, final_gain)


def _cumsum_rows(x):
    n = x.shape[0]
    row = lax.broadcasted_iota(jnp.int32, x.shape, 0)
    s = 1
    while s < n:
        x = x + jnp.where(row >= s, pltpu.roll(x, s, 0), 0.0)
        s *= 2
    return x


def _hgrn_kernel(q_ref, f_ref, i_ref, lbl_ref, s0_ref, o_ref, so_ref, s_scr, *, chunk, layer_j, valid_len):
    c = pl.program_id(1)
    tl = q_ref.shape[0]
    n_head, dk, _ = s_scr.shape

    @pl.when(c == 0)
    def _():
        s_scr[...] = s0_ref[...]

    lg = lbl_ref[...]
    ex = jnp.exp(lg - jnp.max(lg, axis=0, keepdims=True))
    lb_all = jnp.sum(ex[0:layer_j + 1, :], axis=0, keepdims=True) / jnp.sum(ex, axis=0, keepdims=True)

    pad = LANES - chunk
    zpad = jnp.zeros((pad, dk), jnp.float32)
    row = lax.broadcasted_iota(jnp.int32, (chunk, LANES), 0)
    col = lax.broadcasted_iota(jnp.int32, (chunk, LANES), 1)
    causal = col <= row
    ones_cd = jnp.ones((chunk, dk), jnp.float32)
    tn_dims = (((0,), (0,)), ((), ()))
    nt_dims = (((1,), (1,)), ((), ()))

    for cc in range(tl // chunk):
        sl = slice(cc * chunk, (cc + 1) * chunk)
        for h in range(n_head):
            hs = slice(h * dk, (h + 1) * dk)
            lb = lb_all[:, hs]
            f = lb + (1.0 - lb) * jax.nn.sigmoid(f_ref[sl, hs])
            q = _silu(q_ref[sl, hs])
            v = i_ref[sl, hs]
            if valid_len is not None:
                t_idx = c * tl + cc * chunk + lax.broadcasted_iota(jnp.int32, (chunk, dk), 0)
                ok = t_idx < valid_len
                f = jnp.where(ok, f, 1.0)
                q = jnp.where(ok, q, 0.0)
            g = jnp.log(f)
            k = 1.0 - f
            b = _cumsum_rows(g)
            mid = chunk // 2 - 1
            bm = b[mid:mid + 1, :]
            bl = b[chunk - 1:chunk, :]
            qe = q * jnp.exp(b - bm)
            ke = jnp.concatenate([k * jnp.exp(bm - b), zpad], axis=0)
            vp = jnp.concatenate([v, zpad], axis=0)
            a = lax.dot_general(qe, ke, nt_dims, preferred_element_type=jnp.float32)
            a = jnp.where(causal, a, 0.0)
            s = s_scr[h]
            o = jnp.dot(a, vp, preferred_element_type=jnp.float32)
            o = o + jnp.dot(q * jnp.exp(b), s, preferred_element_type=jnp.float32)
            o_ref[sl, hs] = o
            kd = jnp.concatenate([k * jnp.exp(bl - b), zpad], axis=0)
            ghi, glo = _split_bf16(g)
            blc = (lax.dot_general(ghi, ones_cd, tn_dims, preferred_element_type=jnp.float32)
                   + lax.dot_general(glo, ones_cd, tn_dims, preferred_element_type=jnp.float32))
            s_scr[h] = jnp.exp(blc) * s + lax.dot_general(kd, vp, tn_dims, preferred_element_type=jnp.float32)

    @pl.when(c == pl.num_programs(1) - 1)
    def _():
        so_ref[...] = s_scr[...]


def hgrn_scan(pa, B, L, tl, chunk, lb_logits, s0, layer_j, valid_len=None):
    H, dk = s0.shape[1], s0.shape[2]
    d_a = H * dk
    nt = L // tl
    kern = functools.partial(_hgrn_kernel, chunk=chunk, layer_j=layer_j, valid_len=valid_len)

    def col(off):
        return pl.BlockSpec((tl, d_a), lambda b, c: (b * nt + c, off))

    st = pl.BlockSpec((None, H, dk, dk), lambda b, c: (b, 0, 0, 0))
    return pl.pallas_call(
        kern,
        grid=(B, nt),
        in_specs=[col(0), col(1), col(2), pl.BlockSpec(lb_logits.shape, lambda b, c: (0, 0)), st],
        out_specs=[pl.BlockSpec((tl, d_a), lambda b, c: (b * nt + c, 0)), st],
        out_shape=[jax.ShapeDtypeStruct((B * L, d_a), jnp.float32),
                   jax.ShapeDtypeStruct(s0.shape, jnp.float32)],
        scratch_shapes=[pltpu.VMEM((H, dk, dk), jnp.float32)],
        compiler_params=_cparams("parallel", "arbitrary"),
        name="hgrn_scan",
    )(pa, pa, pa, lb_logits, s0)


def _seg_sum(x, bones):
    outs = []
    for j in range(x.shape[1] // LANES):
        hi, lo = _split_bf16(x[:, j * LANES:(j + 1) * LANES])
        outs.append(jnp.dot(hi, bones, preferred_element_type=jnp.float32)
                    + jnp.dot(lo, bones, preferred_element_type=jnp.float32))
    return jnp.concatenate(outs, axis=1)


def _rwkv_prep_kernel(u_ref, halo_ref, e_ref, pos_ref, mu_ref, w0_ref, a0_ref, kk_ref, ka_ref, rk_ref,
                      w2_ref, a2_ref, g2_ref, bones_ref,
                      kk_o, wr_o, w_o, kka_o, km_o, vhi_o, vlo_o, g_o, c1_o, c2_o, bon_o,
                      *, d_b, first_state_tile):
    u = u_ref[...]
    has_state = pl.program_id(0) >= first_state_tile
    u_prev = _prev_in_seq(u, halo_ref[...], 1, pos_ref[...], jnp.where(has_state, e_ref[...], 0.0))
    us = u + (u_prev - u) * mu_ref[...]
    r = us[:, 0:d_b]
    k = us[:, d_b:2 * d_b]
    v = us[:, 2 * d_b:3 * d_b]
    wa = us[:, 3 * d_b:3 * d_b + LANES]
    gd = us[:, 3 * d_b + LANES:3 * d_b + 2 * LANES]
    bones = bones_ref[...]
    z = w0_ref[...] + jnp.dot(jnp.tanh(wa), w2_ref[...], preferred_element_type=jnp.float32)
    nz = -z
    softplus = jnp.maximum(nz, 0.0) + jnp.log(1.0 + jnp.exp(-jnp.abs(nz)))
    w_log = -softplus - 0.5
    decay = jnp.exp(-jnp.exp(w_log))
    a = jax.nn.sigmoid(a0_ref[...] + jnp.dot(wa, a2_ref[...], preferred_element_type=jnp.float32))
    g = jnp.dot(jax.nn.sigmoid(gd), g2_ref[...], preferred_element_type=jnp.float32)
    kk = k * kk_ref[...]
    kk = kk / jnp.maximum(jnp.sqrt(_seg_sum(kk * kk, bones)), 1e-12)
    kmod = k * (1.0 + (a - 1.0) * ka_ref[...])
    kka = kk * a
    vhi, vlo = _split_bf16(v)
    kk_o[...] = kk
    wr_o[...] = decay * r
    w_o[...] = decay
    kka_o[...] = kka
    km_o[...] = kmod
    vhi_o[...] = vhi
    vlo_o[...] = vlo
    g_o[...] = g
    c1_o[...] = _seg_sum(kka * r, bones)
    c2_o[...] = _seg_sum(kmod * r, bones)
    bon_o[...] = _seg_sum(r * kmod * rk_ref[...], bones)


def rwkv_prep(u, pos, e1, tm, n_zero_rows, mu, w0, a0, k_k, k_a, r_k, w2p, a2p, g2, bones, d_b):
    T, DU = u.shape
    row = lambda n: pl.BlockSpec((1, n), lambda i: (0, 0))
    full = lambda a: pl.BlockSpec(a.shape, lambda i: (0, 0))
    first, specs = _seq_tiles(tm, n_zero_rows)
    tok_u, halo_u, start_u = specs(DU)
    tok = pl.BlockSpec((tm, d_b), lambda i: (i, 0))
    return pl.pallas_call(
        functools.partial(_rwkv_prep_kernel, d_b=d_b, first_state_tile=first),
        grid=(T // tm,),
        in_specs=[tok_u, halo_u, start_u, pl.BlockSpec((tm, 1), lambda i: (i, 0)),
                  row(DU), row(d_b), row(d_b), row(d_b), row(d_b), row(d_b),
                  full(w2p), full(a2p), full(g2), full(bones)],
        out_specs=[tok] * 11,
        out_shape=[jax.ShapeDtypeStruct((T, d_b), jnp.float32)] * 11,
        compiler_params=_cparams("parallel"),
        name="rwkv_prep",
    )(u, u, e1, pos, mu.reshape(1, DU), w0.reshape(1, d_b), a0.reshape(1, d_b), k_k.reshape(1, d_b),
      k_a.reshape(1, d_b), r_k.reshape(1, d_b), w2p, a2p, g2, bones)


def _rwkv_scan_kernel(kk_ref, wr_ref, w_ref, kka_ref, k_ref, vhi_ref, vlo_ref, c1_ref, c2_ref,
                      s0_ref, rhs_ref, md_ref, y_ref, so_ref, s_scr, r2_o, sa_o, *, n_pair):
    tb = kk_ref.shape[-2]
    n = HEAD_B

    @pl.when(pl.program_id(1) == 0)
    def _():
        s_scr[...] = s0_ref[...]

    r2_o[...] = jnp.zeros_like(r2_o)
    sa_o[...] = jnp.zeros_like(sa_o)
    rhs = rhs_ref[...]
    md = md_ref[...]
    lane = lax.broadcasted_iota(jnp.int32, (n, LANES), 1) & (n - 1)

    def token_step(row, sel):
        lanes = [slice(p * LANES, (p + 1) * LANES) for p in range(n_pair)]
        parts = []
        for p in range(n_pair):
            s = s_scr[p]
            parts.append(jnp.concatenate([s * row(kk_ref, lanes[p]), s * row(wr_ref, lanes[p])], axis=1))
        for p in range(0, n_pair, 2):
            parts.append(jnp.concatenate([md * row(vhi_ref, lanes[p]), md * row(vhi_ref, lanes[p + 1])], axis=1))
        out = jnp.dot(jnp.concatenate(parts, axis=0), rhs, preferred_element_type=jnp.float32)
        for p in range(n_pair):
            sl = lanes[p]
            red = out[p * n:(p + 1) * n]
            sa_b = red[:, 0:LANES]
            r2_b = red[:, LANES:2 * LANES]
            vrow = (n_pair + p // 2) * n
            v_b = out[vrow:vrow + n, (p % 2) * LANES:(p % 2 + 1) * LANES]
            s = s_scr[p]
            s_scr[p] = s * row(w_ref, sl) - sa_b * row(kka_ref, sl) + v_b * row(k_ref, sl)
            r2_o[p] = jnp.where(sel, r2_b, r2_o[p])
            sa_o[p] = jnp.where(sel, sa_b, sa_o[p])

    if tb % SUBLANES == 0:
        def group(gi, carry):
            base = pl.multiple_of(gi * SUBLANES, SUBLANES)
            for jj in range(SUBLANES):
                token_step(lambda ref, sl, jj=jj: ref[pl.ds(base, SUBLANES), sl][jj:jj + 1, :],
                           lane == gi * SUBLANES + jj)
            return carry

        lax.fori_loop(0, tb // SUBLANES, group, 0)
    else:
        for tt in range(tb):
            token_step(lambda ref, sl, tt=tt: ref[tt:tt + 1, sl], lane == tt)

    for p in range(n_pair):
        sl = slice(p * LANES, (p + 1) * LANES)
        zt = jnp.concatenate([r2_o[p], sa_o[p]], axis=0).T
        r2 = jnp.concatenate([zt[0:tb, 0:n], zt[n:n + tb, 0:n]], axis=1)
        sa = jnp.concatenate([zt[0:tb, n:2 * n], zt[n:n + tb, n:2 * n]], axis=1)
        v = vhi_ref[:, sl] + vlo_ref[:, sl]
        y_ref[:, sl] = r2 - sa * c1_ref[:, sl] + v * c2_ref[:, sl]

    @pl.when(pl.program_id(1) == pl.num_programs(1) - 1)
    def _():
        so_ref[...] = s_scr[...]


def rwkv_scan(vecs, tok, y_spec, y_shape, B, nt, s0_pair, rhs, md):
    n_pair = s0_pair.shape[1]
    st = pl.BlockSpec((None, n_pair, HEAD_B, LANES), lambda b, t: (b, 0, 0, 0))
    acc = pltpu.VMEM((n_pair, HEAD_B, LANES), jnp.float32)
    return pl.pallas_call(
        functools.partial(_rwkv_scan_kernel, n_pair=n_pair),
        grid=(B, nt),
        in_specs=[tok] * 9 + [st, pl.BlockSpec(rhs.shape, lambda b, t: (0, 0)),
                              pl.BlockSpec(md.shape, lambda b, t: (0, 0))],
        out_specs=[y_spec, st],
        out_shape=[y_shape, jax.ShapeDtypeStruct(s0_pair.shape, jnp.float32)],
        scratch_shapes=[acc, acc, acc],
        compiler_params=_cparams("parallel", "arbitrary"),
        name="rwkv_scan",
    )(*vecs, s0_pair, rhs, md)


def _mix_out_kernel(oa_ref, go_ref, hg_ref, y_ref, vhi_ref, vlo_ref, g_ref, bon_ref,
                    lnw_ref, lnb_ref, bones_ref, o_ref, *, d_a):
    oa = oa_ref[...]
    outs = []
    for h in range(d_a // DK_A):
        x = oa[:, h * DK_A:(h + 1) * DK_A]
        outs.append(x * lax.rsqrt(jnp.mean(x * x, axis=-1, keepdims=True) + NORM_EPS))
    o_a = jnp.concatenate(outs, axis=1) * hg_ref[...] * _silu(go_ref[...])
    bones = bones_ref[...]
    v = vhi_ref[...] + vlo_ref[...]
    y = y_ref[...]
    inv_n = 1.0 / HEAD_B
    mean = _seg_sum(y, bones) * inv_n
    d = y - mean
    var = _seg_sum(d * d, bones) * inv_n
    yn = d * lax.rsqrt(var + RWKV_GN_EPS) * lnw_ref[...] + lnb_ref[...]
    o_b = (yn + bon_ref[...] * v) * g_ref[...]
    o_ref[:, 0:d_a] = o_a.astype(o_ref.dtype)
    o_ref[:, d_a:] = o_b.astype(o_ref.dtype)


def mix_out(oa, pa, hg_row, y, vhi, vlo, g, bon, ln_w, ln_b, bones):
    T, d_a = oa.shape
    d_b = y.shape[1]
    tm = _tile(T, 272)
    tok = lambda n: pl.BlockSpec((tm, n), lambda i: (i, 0))
    row = lambda n: pl.BlockSpec((1, n), lambda i: (0, 0))
    return pl.pallas_call(
        functools.partial(_mix_out_kernel, d_a=d_a),
        grid=(T // tm,),
        in_specs=[tok(d_a), pl.BlockSpec((tm, d_a), lambda i: (i, 3)), row(d_a)]
                 + [tok(d_b)] * 5 + [row(d_b), row(d_b), pl.BlockSpec(bones.shape, lambda i: (0, 0))],
        out_specs=tok(d_a + d_b),
        out_shape=jax.ShapeDtypeStruct((T, d_a + d_b), jnp.bfloat16),
        compiler_params=_cparams("parallel"),
        name="mix_out",
    )(oa, pa, hg_row, y, vhi, vlo, g, bon, ln_w.reshape(1, d_b), ln_b.reshape(1, d_b), bones)


def _conv_in_kernel(x_ref, g_ref, wb_ref, wc_ref, wh_ref, gb_ref, u_ref, xn_ref):
    @pl.when(pl.program_id(1) == 0)
    def _():
        xn_ref[...] = _rms(x_ref[...], g_ref[...]).astype(xn_ref.dtype)

    xn = xn_ref[...]
    gb_ref[...] = _mxu_dot(xn, wb_ref[...])
    u_ref[...] = _mxu_dot(xn, wc_ref[...]) * _mxu_dot(xn, wh_ref[...])


def conv_in(x, gain, w, tn=256):
    T, K = x.shape
    d = w.shape[1] // 3
    tm = _tile(T, 1088)
    nb = d // tn
    out = pl.BlockSpec((tm, tn), lambda i, j: (i, j))
    return pl.pallas_call(
        _conv_in_kernel,
        grid=(T // tm, nb),
        in_specs=[pl.BlockSpec((tm, K), lambda i, j: (i, 0)),
                  pl.BlockSpec((1, K), lambda i, j: (0, 0)),
                  pl.BlockSpec((K, tn), lambda i, j: (0, j)),
                  pl.BlockSpec((K, tn), lambda i, j: (0, nb + j)),
                  pl.BlockSpec((K, tn), lambda i, j: (0, 2 * nb + j))],
        out_specs=[out, out],
        out_shape=[jax.ShapeDtypeStruct((T, d), jnp.float32)] * 2,
        scratch_shapes=[pltpu.VMEM((tm, K), jnp.bfloat16)],
        compiler_params=_cparams("parallel", "arbitrary"),
        name="conv_in",
    )(x, gain.reshape(1, K), w, w, w)


def _prev_rows(tile, halo, k):
    rolled = pltpu.roll(tile, k, 0)
    row = lax.broadcasted_iota(jnp.int32, halo.shape, 0)
    head = jnp.where(row < k, pltpu.roll(halo, k, 0), rolled[0:SUBLANES])
    return jnp.concatenate([head, rolled[SUBLANES:]], axis=0)


def _prev_in_seq(tile, halo, k, pos, start_rows):
    return jnp.where(pos < k, start_rows, _prev_rows(tile, halo, k))


def _conv_mm_kernel(gb_ref, u_ref, halo_ref, pos_ref, e1_ref, e2_ref, ck_ref, w_ref, r_ref, o_ref,
                    *, first_state_tile):
    u = u_ref[...]
    pos = pos_ref[...]
    has_state = pl.program_id(0) >= first_state_tile
    u1 = _prev_in_seq(u, halo_ref[...], 1, pos, jnp.where(has_state, e1_ref[...], 0.0))
    u2 = _prev_in_seq(u, halo_ref[...], 2, pos, jnp.where(has_state, e2_ref[...], 0.0))
    y = ck_ref[0:1, :] * u2
    y = y + ck_ref[1:2, :] * u1
    y = y + ck_ref[2:3, :] * u
    o_ref[...] = r_ref[...] + _mxu_dot(gb_ref[...] * y, w_ref[...])


def _seq_tiles(tm, n_zero_rows):
    first = n_zero_rows // tm
    hb = tm // SUBLANES

    def specs(n):
        return (pl.BlockSpec((tm, n), lambda i, *_: (i, 0)),
                pl.BlockSpec((SUBLANES, n), lambda i, *_: (jnp.maximum(i * hb - 1, 0), 0)),
                pl.BlockSpec((tm, n), lambda i, *_: (jnp.maximum(i - first, 0), 0)))

    return first, specs


def conv_mm(gb, u, pos, e1, e2, conv_k, w, res, tm, n_zero_rows):
    T, d = u.shape
    N = w.shape[1]
    assert conv_k.shape[0] == 3
    first, specs = _seq_tiles(tm, n_zero_rows)
    tok, halo, start = specs(d)
    one = pl.Buffered(1)
    start = pl.BlockSpec(start.block_shape, start.index_map, pipeline_mode=one)
    return pl.pallas_call(
        functools.partial(_conv_mm_kernel, first_state_tile=first),
        grid=(T // tm,),
        in_specs=[tok, tok, halo, pl.BlockSpec((tm, 1), lambda i: (i, 0)), start, start,
                  pl.BlockSpec(conv_k.shape, lambda i: (0, 0)),
                  pl.BlockSpec((d, N), lambda i: (0, 0), pipeline_mode=one),
                  pl.BlockSpec((tm, N), lambda i: (i, 0))],
        out_specs=pl.BlockSpec((tm, N), lambda i: (i, 0)),
        out_shape=jax.ShapeDtypeStruct((T, N), jnp.float32),
        compiler_params=_cparams("parallel"),
        name="conv_mm",
    )(gb, u, u, pos, e1, e2, conv_k, w, res)


def _start_rows(state, L, k):
    B, ns, N = state.shape
    assert k <= min(L, ns)
    return jnp.zeros((B, L, N), state.dtype).at[:, :k].set(state[:, ns - k:]).reshape(B * L, N)


def _last_rows(x, groups, k):
    outs = []
    row = 0
    for B, L in groups:
        outs.append(x[row:row + B * L].reshape(B, L, -1)[:, L - k:])
        row += B * L
    return outs


def _pair_state(s):
    B, H, n, _ = s.shape
    return s.reshape(B, H // 2, 2, n, n).transpose(0, 1, 3, 2, 4).reshape(B, H // 2, n, 2 * n)


def _unpair_state(s):
    B, hp, n, _ = s.shape
    return s.reshape(B, hp, n, 2, n).transpose(0, 1, 3, 2, 4).reshape(B, 2 * hp, n, n)


def kernel(x_prompt, x_sample, state_hgrn, state_rwkv, state_shift, state_conv, norm_mix, w_in_ab,
           hgrn_lb_logits, hgrn_norm, rwkv_mu, rwkv_w0, rwkv_w2, rwkv_a0, rwkv_a2, rwkv_g2, rwkv_k_k,
           rwkv_k_a, rwkv_r_k, rwkv_ln_w, rwkv_ln_b, w_out_ab, norm_ffn, ffn_w1, ffn_w3, ffn_w2,
           conv_w_in, conv_k, conv_w_out, moe_router, moe_w1, moe_w3, moe_w2, norm_final):
    f32 = jnp.float32
    Bp, Lp, D = x_prompt.shape
    Bs, Ls, _ = x_sample.shape
    Tp, Ts = Bp * Lp, Bs * Ls
    groups = ((Bp, Lp), (Bs, Ls))
    n_even = w_in_ab.shape[0]
    n_odd = conv_w_in.shape[0]
    depth = n_even + n_odd
    H_A, dk = state_hgrn.shape[2], state_hgrn.shape[3]
    d_a = H_A * dk
    H_B, hb = state_rwkv.shape[2], state_rwkv.shape[3]
    d_b = H_B * hb
    d_shift = state_shift.shape[-1]
    lora_w = rwkv_w2.shape[1]
    lora_a = rwkv_a2.shape[1]
    assert hb == HEAD_B and dk == DK_A and lora_w + lora_a == LANES and H_B % 2 == 0
    assert w_in_ab.shape[2] == 4 * d_a + d_shift

    seg = jnp.arange(LANES) // HEAD_B
    bones = (seg[:, None] == seg[None, :]).astype(f32)
    seg2 = jnp.arange(2 * LANES) // HEAD_B
    rhs_scan = (seg2[:, None] == seg2[None, :]).astype(f32)
    md = (jnp.arange(LANES)[None, :] % HEAD_B == jnp.arange(HEAD_B)[:, None]).astype(f32)

    x = jnp.concatenate([x_prompt.reshape(Tp, D), x_sample.reshape(Ts, D)], axis=0)
    pos = jnp.concatenate([jnp.tile(jnp.arange(Lp, dtype=jnp.int32), Bp),
                           jnp.tile(jnp.arange(Ls, dtype=jnp.int32), Bs)]).reshape(Tp + Ts, 1)
    seq_tile = math.gcd(Tp, Ts)
    assert seq_tile % SUBLANES == 0
    new_h, new_r, new_s, new_c = ([], []), ([], []), ([], []), ([], [])

    for layer in range(depth):
        j = layer // 2
        if layer % 2 == 0:
            w_in = w_in_ab[j]
            tn_u = _tile(d_shift, 512, LANES)
            assert (4 * d_a) % tn_u == 0
            pa = rms_mm(x, norm_mix[layer], w_in, tn=512, n_cols=4 * d_a)
            u = rms_mm(x, norm_mix[layer], w_in, tn=tn_u, col0=4 * d_a)
            zeros_h = jnp.zeros((Bp,) + state_hgrn.shape[2:], f32)
            tl = _tile(Lp, 64)
            oa_p, sh_p = hgrn_scan(pa, Bp, Lp, tl, min(32, tl), hgrn_lb_logits, zeros_h, j)
            lpad = -(-Ls // SUBLANES) * SUBLANES
            pa_s = jnp.pad(pa[Tp:, :3 * d_a].reshape(Bs, Ls, 3 * d_a), ((0, 0), (0, lpad - Ls), (0, 0)))
            oa_s, sh_s = hgrn_scan(pa_s.reshape(Bs * lpad, 3 * d_a), Bs, lpad, lpad, lpad, hgrn_lb_logits,
                                   state_hgrn[j], j, valid_len=Ls)
            oa = jnp.concatenate([oa_p, oa_s.reshape(Bs, lpad, d_a)[:, :Ls].reshape(Ts, d_a)], axis=0)
            new_h[0].append(sh_p)
            new_h[1].append(sh_s)
            e_shift = _start_rows(state_shift[j][:, None, :], Ls, 1)
            w2p = jnp.concatenate([rwkv_w2[j], jnp.zeros((lora_a, d_b), f32)], axis=0)
            a2p = jnp.concatenate([jnp.zeros((lora_w, d_b), f32), rwkv_a2[j]], axis=0)
            prep = rwkv_prep(u, pos, e_shift, _tile(seq_tile, 256), Tp, rwkv_mu[j], rwkv_w0[j], rwkv_a0[j],
                             rwkv_k_k[j], rwkv_k_a[j], rwkv_r_k[j].reshape(d_b), w2p, a2p, rwkv_g2[j],
                             bones, d_b)
            kk, wr, wdec, kka, kmod, vhi, vlo, gg, c1, c2, bon = prep
            scan_in = (kk, wr, wdec, kka, kmod, vhi, vlo, c1, c2)
            tb_p = _tile(Lp, HEAD_B)
            nt_p = Lp // tb_p
            zeros_r = jnp.zeros((Bp, H_B // 2, hb, 2 * hb), f32)
            tok_p = pl.BlockSpec((tb_p, d_b), lambda b, t: (b * nt_p + t, 0))
            y_p, sr_p = rwkv_scan(scan_in, tok_p, tok_p, jax.ShapeDtypeStruct((Tp, d_b), f32), Bp, nt_p,
                                  zeros_r, rhs_scan, md)
            tok_s = pl.BlockSpec((None, Ls, d_b), lambda b, t: (b, 0, 0))
            y_s, sr_s = rwkv_scan([a[Tp:].reshape(Bs, Ls, d_b) for a in scan_in], tok_s, tok_s,
                                  jax.ShapeDtypeStruct((Bs, Ls, d_b), f32), Bs, 1,
                                  _pair_state(state_rwkv[j]), rhs_scan, md)
            yb = jnp.concatenate([y_p, y_s.reshape(Ts, d_b)], axis=0)
            new_r[0].append(_unpair_state(sr_p))
            new_r[1].append(_unpair_state(sr_s))
            last_u = _last_rows(u, groups, 1)
            new_s[0].append(last_u[0][:, 0])
            new_s[1].append(last_u[1][:, 0])
            hg_row = jnp.tile(hgrn_norm[j], H_A).reshape(1, d_a)
            o = mix_out(oa, pa, hg_row, yb, vhi, vlo, gg, bon, rwkv_ln_w[j], rwkv_ln_b[j], bones)
            x = mm_res(o, w_out_ab[j], x)
            x = ffn(x, norm_ffn[layer], ffn_w1[j], ffn_w3[j], ffn_w2[j])
        else:
            d_c = state_conv.shape[-1]
            cw = state_conv.shape[2]
            gb, uc = conv_in(x, norm_mix[layer], conv_w_in[j])
            last_c = _last_rows(uc, groups, cw)
            new_c[0].append(last_c[0])
            new_c[1].append(last_c[1])
            x = conv_mm(gb, uc, pos, _start_rows(state_conv[j], Ls, 1), _start_rows(state_conv[j], Ls, 2),
                        conv_k[j], conv_w_out[j], x, _tile(seq_tile, 256), Tp)
            xp, xs = moe_top2(x, norm_ffn[layer], moe_router[j], moe_w1[j], moe_w3[j], moe_w2[j], Tp,
                              final_gain=norm_final if layer == depth - 1 else None)
            if layer < depth - 1:
                x = jnp.concatenate([xp, xs], axis=0)

    if depth % 2 == 1:
        y = rms_only(x, norm_final)
        xp, xs = y[:Tp], y[Tp:]
    outs = [xp.reshape(Bp, Lp, D), xs.reshape(Bs, Ls, D)]
    for g in (0, 1):
        outs += [jnp.stack(new_h[g]), jnp.stack(new_r[g]), jnp.stack(new_s[g]), jnp.stack(new_c[g])]
    return tuple(outs)
```

```python
import functools
import math

import jax
import jax.numpy as jnp
from jax import lax
from jax.experimental import pallas as pl
from jax.experimental.pallas import tpu as pltpu

NORM_EPS = 1e-6
RWKV_GN_EPS = 64e-5
HEAD_B = 64
DK_A = 128
LANES = 128
SUBLANES = 8
VMEM_LIMIT = 56 * 1024 * 1024


def _cparams(*sem):
    return pltpu.CompilerParams(dimension_semantics=sem, vmem_limit_bytes=VMEM_LIMIT)


def _tile(n, cap, mult=SUBLANES):
    best = None
    for d in range(mult, min(n, cap) + 1, mult):
        if n % d == 0:
            best = d
    return n if best is None else best


def _rms(x, gain):
    return x * lax.rsqrt(jnp.mean(x * x, axis=-1, keepdims=True) + NORM_EPS) * gain


def _silu(x):
    return x * jax.nn.sigmoid(x)


def _mxu_dot(a, w):
    return jnp.dot(a.astype(jnp.bfloat16), w.astype(jnp.bfloat16), preferred_element_type=jnp.float32)


def _split_bf16(x):
    hi = x.astype(jnp.bfloat16).astype(jnp.float32)
    return hi, x - hi


def _rms_mm_kernel(x_ref, g_ref, w_ref, o_ref, xn_ref):
    @pl.when(pl.program_id(1) == 0)
    def _():
        xn_ref[...] = _rms(x_ref[...], g_ref[...]).astype(xn_ref.dtype)

    o_ref[...] = _mxu_dot(xn_ref[...], w_ref[...])


def rms_mm(x, gain, w, tn, col0=0, n_cols=None):
    T, K = x.shape
    N = w.shape[1] - col0 if n_cols is None else n_cols
    assert col0 % tn == 0 and N % tn == 0
    tm = _tile(T, 1088)
    cb0 = col0 // tn
    return pl.pallas_call(
        _rms_mm_kernel,
        grid=(T // tm, N // tn),
        in_specs=[pl.BlockSpec((tm, K), lambda i, j: (i, 0)),
                  pl.BlockSpec((1, K), lambda i, j: (0, 0)),
                  pl.BlockSpec((K, tn), lambda i, j: (0, cb0 + j))],
        out_specs=pl.BlockSpec((tm, tn), lambda i, j: (i, j)),
        out_shape=jax.ShapeDtypeStruct((T, N), jnp.float32),
        scratch_shapes=[pltpu.VMEM((tm, K), jnp.bfloat16)],
        compiler_params=_cparams("parallel", "arbitrary"),
        name="rms_mm",
    )(x, gain.reshape(1, K), w)


def _mm_res_kernel(a_ref, w_ref, r_ref, o_ref):
    o_ref[...] = r_ref[...] + _mxu_dot(a_ref[...], w_ref[...])


def mm_res(a, w, res, tn=512):
    T, K = a.shape
    N = w.shape[1]
    tm = _tile(T, 1088)
    tn = _tile(N, tn, LANES)
    return pl.pallas_call(
        _mm_res_kernel,
        grid=(T // tm, N // tn),
        in_specs=[pl.BlockSpec((tm, K), lambda i, j: (i, 0)),
                  pl.BlockSpec((K, tn), lambda i, j: (0, j)),
                  pl.BlockSpec((tm, tn), lambda i, j: (i, j))],
        out_specs=pl.BlockSpec((tm, tn), lambda i, j: (i, j)),
        out_shape=jax.ShapeDtypeStruct((T, N), jnp.float32),
        compiler_params=_cparams("parallel", "parallel"),
        name="mm_res",
    )(a, w, res)


def _rms_only_kernel(x_ref, g_ref, o_ref):
    o_ref[...] = _rms(x_ref[...], g_ref[...])


def rms_only(x, gain):
    T, K = x.shape
    tm = _tile(T, 544)
    return pl.pallas_call(
        _rms_only_kernel,
        grid=(T // tm,),
        in_specs=[pl.BlockSpec((tm, K), lambda i: (i, 0)),
                  pl.BlockSpec((1, K), lambda i: (0, 0))],
        out_specs=pl.BlockSpec((tm, K), lambda i: (i, 0)),
        out_shape=jax.ShapeDtypeStruct((T, K), jnp.float32),
        compiler_params=_cparams("parallel"),
        name="rms_final",
    )(x, gain.reshape(1, K))


def _ffn_kernel(x_ref, g_ref, w1_ref, w3_ref, w2_ref, o_ref, xn_ref):
    j = pl.program_id(1)

    @pl.when(j == 0)
    def _():
        xn_ref[...] = _rms(x_ref[...], g_ref[...]).astype(xn_ref.dtype)
        o_ref[...] = jnp.zeros_like(o_ref)

    xn = xn_ref[...]
    h1 = _mxu_dot(xn, w1_ref[...])
    h3 = _mxu_dot(xn, w3_ref[...])
    o_ref[...] += _mxu_dot(_silu(h1) * h3, w2_ref[...])

    @pl.when(j == pl.num_programs(1) - 1)
    def _():
        o_ref[...] = x_ref[...] + o_ref[...]


def ffn(x, gain, w1, w3, w2, tf=256):
    T, D = x.shape
    F = w1.shape[1]
    tm = _tile(T, 1088)
    one = pl.Buffered(1)
    return pl.pallas_call(
        _ffn_kernel,
        grid=(T // tm, F // tf),
        in_specs=[pl.BlockSpec((tm, D), lambda i, j: (i, 0), pipeline_mode=one),
                  pl.BlockSpec((1, D), lambda i, j: (0, 0)),
                  pl.BlockSpec((D, tf), lambda i, j: (0, j)),
                  pl.BlockSpec((D, tf), lambda i, j: (0, j)),
                  pl.BlockSpec((tf, D), lambda i, j: (j, 0))],
        out_specs=pl.BlockSpec((tm, D), lambda i, j: (i, 0), pipeline_mode=one),
        out_shape=jax.ShapeDtypeStruct((T, D), jnp.float32),
        scratch_shapes=[pltpu.VMEM((tm, D), jnp.bfloat16)],
        compiler_params=_cparams("parallel", "arbitrary"),
        name="ffn",
    )(x, gain.reshape(1, D), w1, w3, w2)


def _router_kernel(x_ref, g_ref, rt_ref, i1_ref, i2_ref, g1_ref, g2_ref, *, n_exp):
    xn = _rms(x_ref[...], g_ref[...])
    logit = [jnp.sum(xn * rt_ref[e:e + 1, :], axis=-1, keepdims=True) for e in range(n_exp)]
    neg = jnp.float32(-jnp.inf)

    def top(vals):
        m = vals[0]
        for v in vals[1:]:
            m = jnp.maximum(m, v)
        idx = jnp.full(m.shape, n_exp, jnp.int32)
        for e in reversed(range(n_exp)):
            idx = jnp.where(vals[e] == m, e, idx)
        return m, idx

    m1, i1 = top(logit)
    rest = [jnp.where(i1 == e, neg, logit[e]) for e in range(n_exp)]
    m2, i2 = top(rest)
    d = jnp.exp(m2 - m1)
    i1_ref[...] = i1
    i2_ref[...] = i2
    g1_ref[...] = 1.0 / (1.0 + d)
    g2_ref[...] = d / (1.0 + d)


def router_top2(x, gain, router):
    T, D = x.shape
    E = router.shape[1]
    tm = _tile(T, 544)
    col = pl.BlockSpec((tm, 1), lambda i: (i, 0))
    return pl.pallas_call(
        functools.partial(_router_kernel, n_exp=E),
        grid=(T // tm,),
        in_specs=[pl.BlockSpec((tm, D), lambda i: (i, 0)),
                  pl.BlockSpec((1, D), lambda i: (0, 0)),
                  pl.BlockSpec((E, D), lambda i: (0, 0))],
        out_specs=[col, col, col, col],
        out_shape=[jax.ShapeDtypeStruct((T, 1), jnp.int32)] * 2 + [jax.ShapeDtypeStruct((T, 1), jnp.float32)] * 2,
        compiler_params=_cparams("parallel"),
        name="router",
    )(x, gain.reshape(1, D), router.T)


MOE_SUB = 256


def _moe_plan(i1, i2, n_exp, cap):
    T = i1.shape[0]
    n_tiles = (2 * T) // cap + n_exp
    e_a = jnp.concatenate([i1[:, 0], i2[:, 0]])
    tok = jnp.concatenate([jnp.arange(T, dtype=jnp.int32)] * 2)
    onehot = (e_a[:, None] == jnp.arange(n_exp, dtype=jnp.int32)[None, :]).astype(jnp.int32)
    csum = jnp.cumsum(onehot, axis=0)
    rank = jnp.sum((csum - onehot) * onehot, axis=1)
    counts = csum[-1]
    nt_e = (counts + cap - 1) // cap
    t_end = jnp.cumsum(nt_e)
    t_start = t_end - nt_e
    pos = (t_start[e_a] * cap + rank).astype(jnp.int32)
    n_used = t_end[-1]
    tiles = jnp.arange(n_tiles, dtype=jnp.int32)
    t_buf = jnp.minimum(tiles, n_used - 1).astype(jnp.int32)
    t_exp = jnp.minimum(jnp.sum((t_buf[:, None] >= t_end[None, :]).astype(jnp.int32), axis=1), n_exp - 1)
    t_rows = jnp.where(tiles < n_used, jnp.clip(counts[t_exp] - (tiles - t_start[t_exp]) * cap, 0, cap), 0)
    src = jnp.zeros((n_tiles * cap,), jnp.int32).at[pos].set(tok)
    return t_exp.astype(jnp.int32), t_rows.astype(jnp.int32), t_buf, src, pos[:T], pos[T:]


DMA_UNROLL = 8


def _row_copy(src_hbm, row, dst, r, sem):
    return pltpu.make_async_copy(src_hbm.at[pl.ds(row, 1)], dst.at[pl.ds(r, 1)], sem)


def _wait_rows(dst, sem):
    pltpu.make_async_copy(dst, dst, sem).wait()


def _gather_norm_kernel(src_ref, rows_ref, x_hbm, g_ref, o_ref, buf, sem, *, sub_per_tile):
    i = pl.program_id(0)
    n = pl.num_programs(0)
    tg = o_ref.shape[0]

    def active(step):
        return (step % sub_per_tile) * tg < rows_ref[step // sub_per_tile]

    def issue(step):
        slot = step % 2

        def one(r, c):
            _row_copy(x_hbm, src_ref[step * tg + r], buf.at[slot], r, sem.at[slot]).start()
            return c

        lax.fori_loop(0, tg, one, 0, unroll=DMA_UNROLL)

    @pl.when((i == 0) & active(0))
    def _():
        issue(0)

    nxt = jnp.minimum(i + 1, n - 1)

    @pl.when((i + 1 < n) & active(nxt))
    def _():
        issue(nxt)

    @pl.when(active(i))
    def _():
        slot = i % 2
        _wait_rows(buf.at[slot], sem.at[slot])
        o_ref[...] = _rms(buf[slot], g_ref[...]).astype(o_ref.dtype)

    @pl.when(jnp.logical_not(active(i)))
    def _():
        o_ref[...] = jnp.zeros_like(o_ref)


def gather_norm(x, gain, src, t_rows, cap):
    T, D = x.shape
    n_rows = src.shape[0]
    tg = MOE_SUB
    return pl.pallas_call(
        functools.partial(_gather_norm_kernel, sub_per_tile=cap // tg),
        grid_spec=pltpu.PrefetchScalarGridSpec(
            num_scalar_prefetch=2,
            grid=(n_rows // tg,),
            in_specs=[pl.BlockSpec(memory_space=pl.ANY),
                      pl.BlockSpec((1, D), lambda i, s, r: (0, 0))],
            out_specs=pl.BlockSpec((tg, D), lambda i, s, r: (i, 0)),
            scratch_shapes=[pltpu.VMEM((2, tg, D), jnp.float32), pltpu.SemaphoreType.DMA((2,))]),
        out_shape=jax.ShapeDtypeStruct((n_rows, D), jnp.bfloat16),
        compiler_params=_cparams("arbitrary"),
        name="moe_gather",
    )(src, t_rows, x, gain.reshape(1, D))


def _moe_kernel(te_ref, tr_ref, tb_ref, xg_ref, w1_ref, w3_ref, w2_ref, o_ref):
    k = pl.program_id(0)
    j = pl.program_id(1)
    rows = tr_ref[k]

    @pl.when(j == 0)
    def _():
        o_ref[...] = jnp.zeros_like(o_ref)

    @pl.when(rows > 0)
    def _():
        w1 = w1_ref[...].astype(jnp.bfloat16)
        w3 = w3_ref[...].astype(jnp.bfloat16)
        w2 = w2_ref[...].astype(jnp.bfloat16)

        def up(s):
            xs = xg_ref[pl.ds(pl.multiple_of(s * MOE_SUB, MOE_SUB), MOE_SUB), :]
            h1 = jnp.dot(xs, w1, preferred_element_type=jnp.float32)
            h3 = jnp.dot(xs, w3, preferred_element_type=jnp.float32)
            return (_silu(h1) * h3).astype(jnp.bfloat16)

        def down(s, h):
            r0 = pl.multiple_of(s * MOE_SUB, MOE_SUB)
            o_ref[pl.ds(r0, MOE_SUB), :] += jnp.dot(h, w2, preferred_element_type=jnp.float32)

        def body(s, h):
            h_next = up(s + 1)
            down(s, h)
            return h_next

        last = (rows + MOE_SUB - 1) // MOE_SUB - 1
        down(last, lax.fori_loop(0, last, body, up(0)))


def moe_experts(xg, t_exp, t_rows, t_buf, w1, w3, w2, cap, tf=256):
    n_rows, D = xg.shape
    E, _, F = w1.shape
    n_tiles = n_rows // cap
    nj = F // tf

    def jj(k, j, tr):
        return jnp.where(tr[k] > 0, j, nj - 1)

    one = pl.Buffered(1)
    return pl.pallas_call(
        _moe_kernel,
        grid_spec=pltpu.PrefetchScalarGridSpec(
            num_scalar_prefetch=3,
            grid=(n_tiles, nj),
            in_specs=[pl.BlockSpec((cap, D), lambda k, j, te, tr, tb: (tb[k], 0), pipeline_mode=one),
                      pl.BlockSpec((None, D, tf), lambda k, j, te, tr, tb: (te[k], 0, jj(k, j, tr))),
                      pl.BlockSpec((None, D, tf), lambda k, j, te, tr, tb: (te[k], 0, jj(k, j, tr))),
                      pl.BlockSpec((None, tf, D), lambda k, j, te, tr, tb: (te[k], jj(k, j, tr), 0))],
            out_specs=pl.BlockSpec((cap, D), lambda k, j, te, tr, tb: (k, 0), pipeline_mode=one)),
        out_shape=jax.ShapeDtypeStruct((n_rows, D), jnp.float32),
        compiler_params=_cparams("arbitrary", "arbitrary"),
        name="moe_experts",
    )(t_exp, t_rows, t_buf, xg, w1, w3, w2)


def _combine_kernel(p1_ref, p2_ref, x_ref, g1_ref, g2_ref, *rest, final_norm, n_first_tiles):
    if final_norm:
        gain_ref, y_hbm, oa_ref, ob_ref, a_buf, b_buf, sem = rest
    else:
        y_hbm, oa_ref, ob_ref, a_buf, b_buf, sem = rest
    i = pl.program_id(0)
    tc = x_ref.shape[0]

    def issue(step):
        slot = step % 2

        def one(r, c):
            _row_copy(y_hbm, p1_ref[step * tc + r], a_buf.at[slot], r, sem.at[0, slot]).start()
            _row_copy(y_hbm, p2_ref[step * tc + r], b_buf.at[slot], r, sem.at[1, slot]).start()
            return c

        lax.fori_loop(0, tc, one, 0, unroll=DMA_UNROLL)

    @pl.when(i == 0)
    def _():
        issue(0)

    @pl.when(i + 1 < pl.num_programs(0))
    def _():
        issue(i + 1)

    slot = i % 2
    _wait_rows(a_buf.at[slot], sem.at[0, slot])
    _wait_rows(b_buf.at[slot], sem.at[1, slot])
    y = x_ref[...] + (g1_ref[...] * a_buf[slot] + g2_ref[...] * b_buf[slot])
    y = _rms(y, gain_ref[...]) if final_norm else y

    @pl.when(i < n_first_tiles)
    def _():
        oa_ref[...] = y

    @pl.when(i >= n_first_tiles)
    def _():
        ob_ref[...] = y


def moe_combine(x, yg, pos1, pos2, g1, g2, n_first, final_gain=None):
    T, D = x.shape
    tc = _tile(math.gcd(n_first, T - n_first), 256)
    nf = n_first // tc
    final_norm = final_gain is not None
    tok = pl.BlockSpec((tc, D), lambda i, a, b: (i, 0))
    col = pl.BlockSpec((tc, 1), lambda i, a, b: (i, 0))
    in_specs = [tok, col, col] + ([pl.BlockSpec((1, D), lambda i, a, b: (0, 0))] if final_norm else [])
    in_specs.append(pl.BlockSpec(memory_space=pl.ANY))
    args = (x, g1, g2) + ((final_gain.reshape(1, D),) if final_norm else ()) + (yg,)
    return pl.pallas_call(
        functools.partial(_combine_kernel, final_norm=final_norm, n_first_tiles=nf),
        grid_spec=pltpu.PrefetchScalarGridSpec(
            num_scalar_prefetch=2,
            grid=(T // tc,),
            in_specs=in_specs,
            out_specs=[pl.BlockSpec((tc, D), lambda i, a, b: (jnp.minimum(i, nf - 1), 0)),
                       pl.BlockSpec((tc, D), lambda i, a, b: (jnp.maximum(i - nf, 0), 0))],
            scratch_shapes=[pltpu.VMEM((2, tc, D), jnp.float32), pltpu.VMEM((2, tc, D), jnp.float32),
                            pltpu.SemaphoreType.DMA((2, 2))]),
        out_shape=[jax.ShapeDtypeStruct((n_first, D), jnp.float32),
                   jax.ShapeDtypeStruct((T - n_first, D), jnp.float32)],
        compiler_params=_cparams("arbitrary"),
        name="moe_combine",
    )(pos1, pos2, *args)


def moe_top2(x, gain, router, w1, w3, w2, n_first, final_gain=None):
    T, D = x.shape
    E = router.shape[1]
    i1, i2, g1, g2 = router_top2(x, gain, router)
    cap = -(-(2 * T * 21 // (20 * E)) // MOE_SUB) * MOE_SUB
    t_exp, t_rows, t_buf, src, pos1, pos2 = _moe_plan(i1, i2, E, cap)
    xg = gather_norm(x, gain, src, t_rows, cap)
    yg = moe_experts(xg, t_exp, t_rows, t_buf, w1, w3, w2, cap)
    return moe_combine(x, yg, pos1, pos2, g1, g2, n_first, final_gain)


def _cumsum_rows(x):
    n = x.shape[0]
    row = lax.broadcasted_iota(jnp.int32, x.shape, 0)
    s = 1
    while s < n:
        x = x + jnp.where(row >= s, pltpu.roll(x, s, 0), 0.0)
        s *= 2
    return x


def _hgrn_kernel(q_ref, f_ref, i_ref, lbl_ref, s0_ref, o_ref, so_ref, s_scr, *, chunk, layer_j, valid_len):
    c = pl.program_id(1)
    tl = q_ref.shape[0]
    n_head, dk, _ = s_scr.shape

    @pl.when(c == 0)
    def _():
        s_scr[...] = s0_ref[...]

    lg = lbl_ref[...]
    ex = jnp.exp(lg - jnp.max(lg, axis=0, keepdims=True))
    lb_all = jnp.sum(ex[0:layer_j + 1, :], axis=0, keepdims=True) / jnp.sum(ex, axis=0, keepdims=True)

    pad = LANES - chunk
    zpad = jnp.zeros((pad, dk), jnp.float32)
    row = lax.broadcasted_iota(jnp.int32, (chunk, LANES), 0)
    col = lax.broadcasted_iota(jnp.int32, (chunk, LANES), 1)
    causal = col <= row
    ones_cd = jnp.ones((chunk, dk), jnp.float32)
    tn_dims = (((0,), (0,)), ((), ()))
    nt_dims = (((1,), (1,)), ((), ()))

    for cc in range(tl // chunk):
        sl = slice(cc * chunk, (cc + 1) * chunk)
        for h in range(n_head):
            hs = slice(h * dk, (h + 1) * dk)
            lb = lb_all[:, hs]
            f = lb + (1.0 - lb) * jax.nn.sigmoid(f_ref[sl, hs])
            q = _silu(q_ref[sl, hs])
            v = i_ref[sl, hs]
            if valid_len is not None:
                t_idx = c * tl + cc * chunk + lax.broadcasted_iota(jnp.int32, (chunk, dk), 0)
                ok = t_idx < valid_len
                f = jnp.where(ok, f, 1.0)
                q = jnp.where(ok, q, 0.0)
            g = jnp.log(f)
            k = 1.0 - f
            b = _cumsum_rows(g)
            mid = chunk // 2 - 1
            bm = b[mid:mid + 1, :]
            bl = b[chunk - 1:chunk, :]
            qe = q * jnp.exp(b - bm)
            ke = jnp.concatenate([k * jnp.exp(bm - b), zpad], axis=0)
            vp = jnp.concatenate([v, zpad], axis=0)
            a = lax.dot_general(qe, ke, nt_dims, preferred_element_type=jnp.float32)
            a = jnp.where(causal, a, 0.0)
            s = s_scr[h]
            o = jnp.dot(a, vp, preferred_element_type=jnp.float32)
            o = o + jnp.dot(q * jnp.exp(b), s, preferred_element_type=jnp.float32)
            o_ref[sl, hs] = o
            kd = jnp.concatenate([k * jnp.exp(bl - b), zpad], axis=0)
            ghi, glo = _split_bf16(g)
            blc = (lax.dot_general(ghi, ones_cd, tn_dims, preferred_element_type=jnp.float32)
                   + lax.dot_general(glo, ones_cd, tn_dims, preferred_element_type=jnp.float32))
            s_scr[h] = jnp.exp(blc) * s + lax.dot_general(kd, vp, tn_dims, preferred_element_type=jnp.float32)

    @pl.when(c == pl.num_programs(1) - 1)
    def _():
        so_ref[...] = s_scr[...]


def hgrn_scan(pa, B, L, tl, chunk, lb_logits, s0, layer_j, valid_len=None):
    H, dk = s0.shape[1], s0.shape[2]
    d_a = H * dk
    nt = L // tl
    kern = functools.partial(_hgrn_kernel, chunk=chunk, layer_j=layer_j, valid_len=valid_len)

    def col(off):
        return pl.BlockSpec((tl, d_a), lambda b, c: (b * nt + c, off))

    st = pl.BlockSpec((None, H, dk, dk), lambda b, c: (b, 0, 0, 0))
    return pl.pallas_call(
        kern,
        grid=(B, nt),
        in_specs=[col(0), col(1), col(2), pl.BlockSpec(lb_logits.shape, lambda b, c: (0, 0)), st],
        out_specs=[pl.BlockSpec((tl, d_a), lambda b, c: (b * nt + c, 0)), st],
        out_shape=[jax.ShapeDtypeStruct((B * L, d_a), jnp.float32),
                   jax.ShapeDtypeStruct(s0.shape, jnp.float32)],
        scratch_shapes=[pltpu.VMEM((H, dk, dk), jnp.float32)],
        compiler_params=_cparams("parallel", "arbitrary"),
        name="hgrn_scan",
    )(pa, pa, pa, lb_logits, s0)


def _seg_sum(x, bones):
    outs = []
    for j in range(x.shape[1] // LANES):
        hi, lo = _split_bf16(x[:, j * LANES:(j + 1) * LANES])
        outs.append(jnp.dot(hi, bones, preferred_element_type=jnp.float32)
                    + jnp.dot(lo, bones, preferred_element_type=jnp.float32))
    return jnp.concatenate(outs, axis=1)


def _rwkv_prep_kernel(u_ref, halo_ref, e_ref, pos_ref, mu_ref, w0_ref, a0_ref, kk_ref, ka_ref, rk_ref,
                      w2_ref, a2_ref, g2_ref, bones_ref,
                      kk_o, wr_o, w_o, kka_o, km_o, vhi_o, vlo_o, g_o, c1_o, c2_o, bon_o,
                      *, d_b, first_state_tile):
    u = u_ref[...]
    has_state = pl.program_id(0) >= first_state_tile
    u_prev = _prev_in_seq(u, halo_ref[...], 1, pos_ref[...], jnp.where(has_state, e_ref[...], 0.0))
    us = u + (u_prev - u) * mu_ref[...]
    r = us[:, 0:d_b]
    k = us[:, d_b:2 * d_b]
    v = us[:, 2 * d_b:3 * d_b]
    wa = us[:, 3 * d_b:3 * d_b + LANES]
    gd = us[:, 3 * d_b + LANES:3 * d_b + 2 * LANES]
    bones = bones_ref[...]
    z = w0_ref[...] + jnp.dot(jnp.tanh(wa), w2_ref[...], preferred_element_type=jnp.float32)
    nz = -z
    softplus = jnp.maximum(nz, 0.0) + jnp.log(1.0 + jnp.exp(-jnp.abs(nz)))
    w_log = -softplus - 0.5
    decay = jnp.exp(-jnp.exp(w_log))
    a = jax.nn.sigmoid(a0_ref[...] + jnp.dot(wa, a2_ref[...], preferred_element_type=jnp.float32))
    g = jnp.dot(jax.nn.sigmoid(gd), g2_ref[...], preferred_element_type=jnp.float32)
    kk = k * kk_ref[...]
    kk = kk / jnp.maximum(jnp.sqrt(_seg_sum(kk * kk, bones)), 1e-12)
    kmod = k * (1.0 + (a - 1.0) * ka_ref[...])
    kka = kk * a
    vhi, vlo = _split_bf16(v)
    kk_o[...] = kk
    wr_o[...] = decay * r
    w_o[...] = decay
    kka_o[...] = kka
    km_o[...] = kmod
    vhi_o[...] = vhi
    vlo_o[...] = vlo
    g_o[...] = g
    c1_o[...] = _seg_sum(kka * r, bones)
    c2_o[...] = _seg_sum(kmod * r, bones)
    bon_o[...] = _seg_sum(r * kmod * rk_ref[...], bones)


def rwkv_prep(u, pos, e1, tm, n_zero_rows, mu, w0, a0, k_k, k_a, r_k, w2p, a2p, g2, bones, d_b):
    T, DU = u.shape
    row = lambda n: pl.BlockSpec((1, n), lambda i: (0, 0))
    full = lambda a: pl.BlockSpec(a.shape, lambda i: (0, 0))
    first, specs = _seq_tiles(tm, n_zero_rows)
    tok_u, halo_u, start_u = specs(DU)
    tok = pl.BlockSpec((tm, d_b), lambda i: (i, 0))
    return pl.pallas_call(
        functools.partial(_rwkv_prep_kernel, d_b=d_b, first_state_tile=first),
        grid=(T // tm,),
        in_specs=[tok_u, halo_u, start_u, pl.BlockSpec((tm, 1), lambda i: (i, 0)),
                  row(DU), row(d_b), row(d_b), row(d_b), row(d_b), row(d_b),
                  full(w2p), full(a2p), full(g2), full(bones)],
        out_specs=[tok] * 11,
        out_shape=[jax.ShapeDtypeStruct((T, d_b), jnp.float32)] * 11,
        compiler_params=_cparams("parallel"),
        name="rwkv_prep",
    )(u, u, e1, pos, mu.reshape(1, DU), w0.reshape(1, d_b), a0.reshape(1, d_b), k_k.reshape(1, d_b),
      k_a.reshape(1, d_b), r_k.reshape(1, d_b), w2p, a2p, g2, bones)


SCAN_GROUP = 32


def _rwkv_scan_kernel(kk_ref, wr_ref, w_ref, kka_ref, k_ref, vhi_ref, vlo_ref, c1_ref, c2_ref,
                      s0_ref, rhs_ref, md_ref, y_ref, so_ref, s_scr, r2_o, sa_o, *, n_pair):
    tb = kk_ref.shape[-2]
    n = HEAD_B

    @pl.when(pl.program_id(1) == 0)
    def _():
        s_scr[...] = s0_ref[...]

    r2_o[...] = jnp.zeros_like(r2_o)
    sa_o[...] = jnp.zeros_like(sa_o)
    rhs = rhs_ref[...]
    md = md_ref[...]
    lane = lax.broadcasted_iota(jnp.int32, (n, LANES), 1) & (n - 1)

    def token_step(row, sel):
        parts = []
        for p in range(n_pair):
            sl = slice(p * LANES, (p + 1) * LANES)
            s = s_scr[p]
            p1 = s * row(kk_ref, sl)
            p2 = s * row(wr_ref, sl)
            dh = md * row(vhi_ref, sl)
            dl = md * row(vlo_ref, sl)
            parts.append(jnp.concatenate([p1, dh], axis=1))
            parts.append(jnp.concatenate([p2, dl], axis=1))
        out = jnp.dot(jnp.concatenate(parts, axis=0), rhs, preferred_element_type=jnp.float32)
        for p in range(n_pair):
            sl = slice(p * LANES, (p + 1) * LANES)
            top = out[2 * p * n:(2 * p + 1) * n]
            bot = out[(2 * p + 1) * n:(2 * p + 2) * n]
            sa_b = top[:, 0:LANES]
            r2_b = bot[:, 0:LANES]
            v_b = top[:, LANES:2 * LANES] + bot[:, LANES:2 * LANES]
            s = s_scr[p]
            s_scr[p] = s * row(w_ref, sl) - sa_b * row(kka_ref, sl) + v_b * row(k_ref, sl)
            r2_o[p] = jnp.where(sel, r2_b, r2_o[p])
            sa_o[p] = jnp.where(sel, sa_b, sa_o[p])

    if tb % SCAN_GROUP == 0:
        def group(gi, carry):
            for sub in range(SCAN_GROUP // SUBLANES):
                t0 = gi * SCAN_GROUP + sub * SUBLANES
                base = pl.multiple_of(t0, SUBLANES)
                for jj in range(SUBLANES):
                    token_step(lambda ref, sl, jj=jj, base=base: ref[pl.ds(base, SUBLANES), sl][jj:jj + 1, :],
                               lane == t0 + jj)
            return carry

        lax.fori_loop(0, tb // SCAN_GROUP, group, 0)
    else:
        for tt in range(tb):
            token_step(lambda ref, sl, tt=tt: ref[tt:tt + 1, sl], lane == tt)

    for p in range(n_pair):
        sl = slice(p * LANES, (p + 1) * LANES)
        zt = jnp.concatenate([r2_o[p], sa_o[p]], axis=0).T
        r2 = jnp.concatenate([zt[0:tb, 0:n], zt[n:n + tb, 0:n]], axis=1)
        sa = jnp.concatenate([zt[0:tb, n:2 * n], zt[n:n + tb, n:2 * n]], axis=1)
        v = vhi_ref[:, sl] + vlo_ref[:, sl]
        y_ref[:, sl] = r2 - sa * c1_ref[:, sl] + v * c2_ref[:, sl]

    @pl.when(pl.program_id(1) == pl.num_programs(1) - 1)
    def _():
        so_ref[...] = s_scr[...]


def rwkv_scan(vecs, tok, y_spec, y_shape, B, nt, s0_pair, rhs, md):
    n_pair = s0_pair.shape[1]
    st = pl.BlockSpec((None, n_pair, HEAD_B, LANES), lambda b, t: (b, 0, 0, 0))
    acc = pltpu.VMEM((n_pair, HEAD_B, LANES), jnp.float32)
    return pl.pallas_call(
        functools.partial(_rwkv_scan_kernel, n_pair=n_pair),
        grid=(B, nt),
        in_specs=[tok] * 9 + [st, pl.BlockSpec(rhs.shape, lambda b, t: (0, 0)),
                              pl.BlockSpec(md.shape, lambda b, t: (0, 0))],
        out_specs=[y_spec, st],
        out_shape=[y_shape, jax.ShapeDtypeStruct(s0_pair.shape, jnp.float32)],
        scratch_shapes=[acc, acc, acc],
        compiler_params=_cparams("parallel", "arbitrary"),
        name="rwkv_scan",
    )(*vecs, s0_pair, rhs, md)


def _mix_out_kernel(oa1_ref, oa2_ref, go_ref, hg_ref, y1_ref, y2_ref, vhi_ref, vlo_ref, g_ref, bon_ref,
                    lnw_ref, lnb_ref, bones_ref, o_ref, *, d_a, n_first_tiles):
    first = pl.program_id(0) < n_first_tiles
    oa = jnp.where(first, oa1_ref[...], oa2_ref[...])
    outs = []
    for h in range(d_a // DK_A):
        x = oa[:, h * DK_A:(h + 1) * DK_A]
        outs.append(x * lax.rsqrt(jnp.mean(x * x, axis=-1, keepdims=True) + NORM_EPS))
    o_a = jnp.concatenate(outs, axis=1) * hg_ref[...] * _silu(go_ref[...])
    bones = bones_ref[...]
    v = vhi_ref[...] + vlo_ref[...]
    y = jnp.where(first, y1_ref[...], y2_ref[...])
    inv_n = 1.0 / HEAD_B
    mean = _seg_sum(y, bones) * inv_n
    d = y - mean
    var = _seg_sum(d * d, bones) * inv_n
    yn = d * lax.rsqrt(var + RWKV_GN_EPS) * lnw_ref[...] + lnb_ref[...]
    o_b = (yn + bon_ref[...] * v) * g_ref[...]
    o_ref[:, 0:d_a] = o_a.astype(o_ref.dtype)
    o_ref[:, d_a:] = o_b.astype(o_ref.dtype)


def mix_out(oa_pair, pa, hg_row, y_pair, vhi, vlo, g, bon, ln_w, ln_b, bones):
    n_first, d_a = oa_pair[0].shape
    n_rest = oa_pair[1].shape[0]
    T = n_first + n_rest
    d_b = y_pair[0].shape[1]
    tm = _tile(math.gcd(n_first, n_rest), 256, 2 * SUBLANES)
    nf = n_first // tm
    tok = lambda n: pl.BlockSpec((tm, n), lambda i: (i, 0))
    row = lambda n: pl.BlockSpec((1, n), lambda i: (0, 0))
    lead = lambda n: pl.BlockSpec((tm, n), lambda i: (jnp.minimum(i, nf - 1), 0))
    rest = lambda n: pl.BlockSpec((tm, n), lambda i: (jnp.maximum(i - nf, 0), 0))
    return pl.pallas_call(
        functools.partial(_mix_out_kernel, d_a=d_a, n_first_tiles=nf),
        grid=(T // tm,),
        in_specs=[lead(d_a), rest(d_a), pl.BlockSpec((tm, d_a), lambda i: (i, 3)), row(d_a),
                  lead(d_b), rest(d_b)]
                 + [tok(d_b)] * 4 + [row(d_b), row(d_b), pl.BlockSpec(bones.shape, lambda i: (0, 0))],
        out_specs=tok(d_a + d_b),
        out_shape=jax.ShapeDtypeStruct((T, d_a + d_b), jnp.bfloat16),
        compiler_params=_cparams("parallel"),
        name="mix_out",
    )(*oa_pair, pa, hg_row, *y_pair, vhi, vlo, g, bon, ln_w.reshape(1, d_b), ln_b.reshape(1, d_b), bones)


def _conv_in_kernel(x_ref, g_ref, wb_ref, wc_ref, wh_ref, gb_ref, u_ref, xn_ref):
    @pl.when(pl.program_id(1) == 0)
    def _():
        xn_ref[...] = _rms(x_ref[...], g_ref[...]).astype(xn_ref.dtype)

    xn = xn_ref[...]
    gb_ref[...] = _mxu_dot(xn, wb_ref[...])
    u_ref[...] = _mxu_dot(xn, wc_ref[...]) * _mxu_dot(xn, wh_ref[...])


def conv_in(x, gain, w, tn=256):
    T, K = x.shape
    d = w.shape[1] // 3
    tm = _tile(T, 1088)
    nb = d // tn
    out = pl.BlockSpec((tm, tn), lambda i, j: (i, j))
    return pl.pallas_call(
        _conv_in_kernel,
        grid=(T // tm, nb),
        in_specs=[pl.BlockSpec((tm, K), lambda i, j: (i, 0)),
                  pl.BlockSpec((1, K), lambda i, j: (0, 0)),
                  pl.BlockSpec((K, tn), lambda i, j: (0, j)),
                  pl.BlockSpec((K, tn), lambda i, j: (0, nb + j)),
                  pl.BlockSpec((K, tn), lambda i, j: (0, 2 * nb + j))],
        out_specs=[out, out],
        out_shape=[jax.ShapeDtypeStruct((T, d), jnp.float32)] * 2,
        scratch_shapes=[pltpu.VMEM((tm, K), jnp.bfloat16)],
        compiler_params=_cparams("parallel", "arbitrary"),
        name="conv_in",
    )(x, gain.reshape(1, K), w, w, w)


def _prev_rows(tile, halo, k):
    rolled = pltpu.roll(tile, k, 0)
    row = lax.broadcasted_iota(jnp.int32, halo.shape, 0)
    head = jnp.where(row < k, pltpu.roll(halo, k, 0), rolled[0:SUBLANES])
    return jnp.concatenate([head, rolled[SUBLANES:]], axis=0)


def _prev_in_seq(tile, halo, k, pos, start_rows):
    return jnp.where(pos < k, start_rows, _prev_rows(tile, halo, k))


def _conv_mm_kernel(gb_ref, u_ref, halo_ref, pos_ref, e1_ref, e2_ref, ck_ref, w_ref, r_ref, o_ref,
                    *, first_state_tile):
    u = u_ref[...]
    pos = pos_ref[...]
    has_state = pl.program_id(0) >= first_state_tile
    u1 = _prev_in_seq(u, halo_ref[...], 1, pos, jnp.where(has_state, e1_ref[...], 0.0))
    u2 = _prev_in_seq(u, halo_ref[...], 2, pos, jnp.where(has_state, e2_ref[...], 0.0))
    y = ck_ref[0:1, :] * u2
    y = y + ck_ref[1:2, :] * u1
    y = y + ck_ref[2:3, :] * u
    o_ref[...] = r_ref[...] + _mxu_dot(gb_ref[...] * y, w_ref[...])


def _seq_tiles(tm, n_zero_rows):
    first = n_zero_rows // tm
    hb = tm // SUBLANES

    def specs(n):
        return (pl.BlockSpec((tm, n), lambda i, *_: (i, 0)),
                pl.BlockSpec((SUBLANES, n), lambda i, *_: (jnp.maximum(i * hb - 1, 0), 0)),
                pl.BlockSpec((tm, n), lambda i, *_: (jnp.maximum(i - first, 0), 0)))

    return first, specs


def conv_mm(gb, u, pos, e1, e2, conv_k, w, res, tm, n_zero_rows):
    T, d = u.shape
    N = w.shape[1]
    assert conv_k.shape[0] == 3
    first, specs = _seq_tiles(tm, n_zero_rows)
    tok, halo, start = specs(d)
    one = pl.Buffered(1)
    start = pl.BlockSpec(start.block_shape, start.index_map, pipeline_mode=one)
    return pl.pallas_call(
        functools.partial(_conv_mm_kernel, first_state_tile=first),
        grid=(T // tm,),
        in_specs=[tok, tok, halo, pl.BlockSpec((tm, 1), lambda i: (i, 0)), start, start,
                  pl.BlockSpec(conv_k.shape, lambda i: (0, 0)),
                  pl.BlockSpec((d, N), lambda i: (0, 0), pipeline_mode=one),
                  pl.BlockSpec((tm, N), lambda i: (i, 0))],
        out_specs=pl.BlockSpec((tm, N), lambda i: (i, 0)),
        out_shape=jax.ShapeDtypeStruct((T, N), jnp.float32),
        compiler_params=_cparams("parallel"),
        name="conv_mm",
    )(gb, u, u, pos, e1, e2, conv_k, w, res)


def _start_rows(state, L, k):
    B, ns, N = state.shape
    assert k <= min(L, ns)
    return jnp.zeros((B, L, N), state.dtype).at[:, :k].set(state[:, ns - k:]).reshape(B * L, N)


def _last_rows(x, groups, k):
    outs = []
    row = 0
    for B, L in groups:
        idx = [row + b * L + L - k + i for b in range(B) for i in range(k)]
        outs.append(jnp.take(x, jnp.asarray(idx, jnp.int32), axis=0).reshape(B, k, -1))
        row += B * L
    return outs


def _pair_state(s):
    B, H, n, _ = s.shape
    return s.reshape(B, H // 2, 2, n, n).transpose(0, 1, 3, 2, 4).reshape(B, H // 2, n, 2 * n)


def _unpair_state(s):
    B, hp, n, _ = s.shape
    return s.reshape(B, hp, n, 2, n).transpose(0, 1, 3, 2, 4).reshape(B, 2 * hp, n, n)


def kernel(x_prompt, x_sample, state_hgrn, state_rwkv, state_shift, state_conv, norm_mix, w_in_ab,
           hgrn_lb_logits, hgrn_norm, rwkv_mu, rwkv_w0, rwkv_w2, rwkv_a0, rwkv_a2, rwkv_g2, rwkv_k_k,
           rwkv_k_a, rwkv_r_k, rwkv_ln_w, rwkv_ln_b, w_out_ab, norm_ffn, ffn_w1, ffn_w3, ffn_w2,
           conv_w_in, conv_k, conv_w_out, moe_router, moe_w1, moe_w3, moe_w2, norm_final):
    f32 = jnp.float32
    Bp, Lp, D = x_prompt.shape
    Bs, Ls, _ = x_sample.shape
    Tp, Ts = Bp * Lp, Bs * Ls
    groups = ((Bp, Lp), (Bs, Ls))
    n_even = w_in_ab.shape[0]
    n_odd = conv_w_in.shape[0]
    depth = n_even + n_odd
    H_A, dk = state_hgrn.shape[2], state_hgrn.shape[3]
    d_a = H_A * dk
    H_B, hb = state_rwkv.shape[2], state_rwkv.shape[3]
    d_b = H_B * hb
    d_shift = state_shift.shape[-1]
    lora_w = rwkv_w2.shape[1]
    lora_a = rwkv_a2.shape[1]
    assert hb == HEAD_B and dk == DK_A and lora_w + lora_a == LANES and H_B % 2 == 0
    assert w_in_ab.shape[2] == 4 * d_a + d_shift

    seg = jnp.arange(LANES) // HEAD_B
    bones = (seg[:, None] == seg[None, :]).astype(f32)
    seg2 = jnp.arange(2 * LANES) // HEAD_B
    rhs_scan = (seg2[:, None] == seg2[None, :]).astype(f32)
    md = (jnp.arange(LANES)[None, :] % HEAD_B == jnp.arange(HEAD_B)[:, None]).astype(f32)

    x = jnp.concatenate([x_prompt.reshape(Tp, D), x_sample.reshape(Ts, D)], axis=0)
    pos = jnp.concatenate([jnp.tile(jnp.arange(Lp, dtype=jnp.int32), Bp),
                           jnp.tile(jnp.arange(Ls, dtype=jnp.int32), Bs)]).reshape(Tp + Ts, 1)
    seq_tile = math.gcd(Tp, Ts)
    assert seq_tile % SUBLANES == 0
    new_h, new_r, new_s, new_c = ([], []), ([], []), ([], []), ([], [])

    for layer in range(depth):
        j = layer // 2
        if layer % 2 == 0:
            w_in = w_in_ab[j]
            tn_u = _tile(d_shift, 512, LANES)
            assert (4 * d_a) % tn_u == 0
            pa = rms_mm(x, norm_mix[layer], w_in, tn=512, n_cols=4 * d_a)
            u = rms_mm(x, norm_mix[layer], w_in, tn=tn_u, col0=4 * d_a)
            zeros_h = jnp.zeros((Bp,) + state_hgrn.shape[2:], f32)
            tl = _tile(Lp, 64)
            oa_p, sh_p = hgrn_scan(pa, Bp, Lp, tl, min(32, tl), hgrn_lb_logits, zeros_h, j)
            lpad = -(-Ls // SUBLANES) * SUBLANES
            pa_s = jnp.pad(pa[Tp:, :3 * d_a].reshape(Bs, Ls, 3 * d_a), ((0, 0), (0, lpad - Ls), (0, 0)))
            oa_s, sh_s = hgrn_scan(pa_s.reshape(Bs * lpad, 3 * d_a), Bs, lpad, lpad, lpad, hgrn_lb_logits,
                                   state_hgrn[j], j, valid_len=Ls)
            oa = (oa_p, oa_s.reshape(Bs, lpad, d_a)[:, :Ls].reshape(Ts, d_a))
            new_h[0].append(sh_p)
            new_h[1].append(sh_s)
            e_shift = _start_rows(state_shift[j][:, None, :], Ls, 1)
            w2p = jnp.concatenate([rwkv_w2[j], jnp.zeros((lora_a, d_b), f32)], axis=0)
            a2p = jnp.concatenate([jnp.zeros((lora_w, d_b), f32), rwkv_a2[j]], axis=0)
            prep = rwkv_prep(u, pos, e_shift, _tile(seq_tile, 256), Tp, rwkv_mu[j], rwkv_w0[j], rwkv_a0[j],
                             rwkv_k_k[j], rwkv_k_a[j], rwkv_r_k[j].reshape(d_b), w2p, a2p, rwkv_g2[j],
                             bones, d_b)
            kk, wr, wdec, kka, kmod, vhi, vlo, gg, c1, c2, bon = prep
            scan_in = (kk, wr, wdec, kka, kmod, vhi, vlo, c1, c2)
            tb_p = _tile(Lp, HEAD_B)
            nt_p = Lp // tb_p
            zeros_r = jnp.zeros((Bp, H_B // 2, hb, 2 * hb), f32)
            tok_p = pl.BlockSpec((tb_p, d_b), lambda b, t: (b * nt_p + t, 0))
            y_p, sr_p = rwkv_scan(scan_in, tok_p, tok_p, jax.ShapeDtypeStruct((Tp, d_b), f32), Bp, nt_p,
                                  zeros_r, rhs_scan, md)
            tok_s = pl.BlockSpec((None, Ls, d_b), lambda b, t: (b, 0, 0))
            y_s, sr_s = rwkv_scan([a[Tp:].reshape(Bs, Ls, d_b) for a in scan_in], tok_s, tok_s,
                                  jax.ShapeDtypeStruct((Bs, Ls, d_b), f32), Bs, 1,
                                  _pair_state(state_rwkv[j]), rhs_scan, md)
            yb = (y_p, y_s.reshape(Ts, d_b))
            new_r[0].append(_unpair_state(sr_p))
            new_r[1].append(_unpair_state(sr_s))
            last_u = _last_rows(u, groups, 1)
            new_s[0].append(last_u[0][:, 0])
            new_s[1].append(last_u[1][:, 0])
            hg_row = jnp.tile(hgrn_norm[j], H_A).reshape(1, d_a)
            o = mix_out(oa, pa, hg_row, yb, vhi, vlo, gg, bon, rwkv_ln_w[j], rwkv_ln_b[j], bones)
            x = mm_res(o, w_out_ab[j], x)
            x = ffn(x, norm_ffn[layer], ffn_w1[j], ffn_w3[j], ffn_w2[j])
        else:
            d_c = state_conv.shape[-1]
            cw = state_conv.shape[2]
            gb, uc = conv_in(x, norm_mix[layer], conv_w_in[j])
            last_c = _last_rows(uc, groups, cw)
            new_c[0].append(last_c[0])
            new_c[1].append(last_c[1])
            x = conv_mm(gb, uc, pos, _start_rows(state_conv[j], Ls, 1), _start_rows(state_conv[j], Ls, 2),
                        conv_k[j], conv_w_out[j], x, _tile(seq_tile, 256), Tp)
            xp, xs = moe_top2(x, norm_ffn[layer], moe_router[j], moe_w1[j], moe_w3[j], moe_w2[j], Tp,
                              final_gain=norm_final if layer == depth - 1 else None)
            if layer < depth - 1:
                x = jnp.concatenate([xp, xs], axis=0)

    if depth % 2 == 1:
        y = rms_only(x, norm_final)
        xp, xs = y[:Tp], y[Tp:]
    outs = [xp.reshape(Bp, Lp, D), xs.reshape(Bs, Ls, D)]
    for g in (0, 1):
        outs += [jnp.stack(new_h[g]), jnp.stack(new_r[g]), jnp.stack(new_s[g]), jnp.stack(new_c[g])]
    return tuple(outs)
```

```python
import functools
import math

import jax
import jax.numpy as jnp
from jax import lax
from jax.experimental import pallas as pl
from jax.experimental.pallas import tpu as pltpu

NORM_EPS = 1e-6
RWKV_GN_EPS = 64e-5
HEAD_B = 64
DK_A = 128
LANES = 128
SUBLANES = 8
VMEM_LIMIT = 56 * 1024 * 1024


def _cparams(*sem):
    return pltpu.CompilerParams(dimension_semantics=sem, vmem_limit_bytes=VMEM_LIMIT)


def _tile(n, cap, mult=SUBLANES):
    best = None
    for d in range(mult, min(n, cap) + 1, mult):
        if n % d == 0:
            best = d
    return n if best is None else best


def _rms(x, gain):
    return x * lax.rsqrt(jnp.mean(x * x, axis=-1, keepdims=True) + NORM_EPS) * gain


def _silu(x):
    return x * jax.nn.sigmoid(x)


def _mxu_dot(a, w):
    return jnp.dot(a.astype(jnp.bfloat16), w.astype(jnp.bfloat16), preferred_element_type=jnp.float32)


def _split_bf16(x):
    hi = x.astype(jnp.bfloat16).astype(jnp.float32)
    return hi, x - hi


def _rms_mm_kernel(x_ref, g_ref, wa_ref, wb_ref, o_ref, xn_ref):
    @pl.when(pl.program_id(1) == 0)
    def _():
        xn_ref[...] = _rms(x_ref[...], g_ref[...]).astype(xn_ref.dtype)

    kh = wa_ref.shape[0]
    o_ref[...] = _mxu_dot(xn_ref[:, :kh], wa_ref[...]) + _mxu_dot(xn_ref[:, kh:], wb_ref[...])


def rms_mm(x, gain, w, tn, col0=0, n_cols=None):
    T, K = x.shape
    N = w.shape[1] - col0 if n_cols is None else n_cols
    assert col0 % tn == 0 and N % tn == 0 and K % (2 * LANES) == 0
    tm = _tile(T, 1088)
    cb0 = col0 // tn
    return pl.pallas_call(
        _rms_mm_kernel,
        grid=(T // tm, N // tn),
        in_specs=[pl.BlockSpec((tm, K), lambda i, j: (i, 0)),
                  pl.BlockSpec((1, K), lambda i, j: (0, 0)),
                  pl.BlockSpec((K // 2, tn), lambda i, j: (0, cb0 + j)),
                  pl.BlockSpec((K // 2, tn), lambda i, j: (1, cb0 + j))],
        out_specs=pl.BlockSpec((tm, tn), lambda i, j: (i, j)),
        out_shape=jax.ShapeDtypeStruct((T, N), jnp.float32),
        scratch_shapes=[pltpu.VMEM((tm, K), jnp.bfloat16)],
        compiler_params=_cparams("parallel", "arbitrary"),
        name="rms_mm",
    )(x, gain.reshape(1, K), w, w)


def _mm_res_kernel(a_ref, w_ref, r_ref, o_ref):
    o_ref[...] = r_ref[...] + _mxu_dot(a_ref[...], w_ref[...])


def mm_res(a, w, res):
    T, K = a.shape
    N = w.shape[1]
    tm = _tile(T, 544, 2 * SUBLANES)
    return pl.pallas_call(
        _mm_res_kernel,
        grid=(T // tm,),
        in_specs=[pl.BlockSpec((tm, K), lambda i: (i, 0)),
                  pl.BlockSpec((K, N), lambda i: (0, 0), pipeline_mode=pl.Buffered(1)),
                  pl.BlockSpec((tm, N), lambda i: (i, 0))],
        out_specs=pl.BlockSpec((tm, N), lambda i: (i, 0)),
        out_shape=jax.ShapeDtypeStruct((T, N), jnp.float32),
        compiler_params=_cparams("parallel"),
        name="mm_res",
    )(a, w, res)


def _rms_only_kernel(x_ref, g_ref, o_ref):
    o_ref[...] = _rms(x_ref[...], g_ref[...])


def rms_only(x, gain):
    T, K = x.shape
    tm = _tile(T, 544)
    return pl.pallas_call(
        _rms_only_kernel,
        grid=(T // tm,),
        in_specs=[pl.BlockSpec((tm, K), lambda i: (i, 0)),
                  pl.BlockSpec((1, K), lambda i: (0, 0))],
        out_specs=pl.BlockSpec((tm, K), lambda i: (i, 0)),
        out_shape=jax.ShapeDtypeStruct((T, K), jnp.float32),
        compiler_params=_cparams("parallel"),
        name="rms_final",
    )(x, gain.reshape(1, K))


def _ffn_kernel(x_ref, g_ref, w1_ref, w3_ref, w2_ref, o_ref, xn_ref):
    j = pl.program_id(1)

    @pl.when(j == 0)
    def _():
        xn_ref[...] = _rms(x_ref[...], g_ref[...]).astype(xn_ref.dtype)
        o_ref[...] = jnp.zeros_like(o_ref)

    xn = xn_ref[...]
    h1 = _mxu_dot(xn, w1_ref[...])
    h3 = _mxu_dot(xn, w3_ref[...])
    o_ref[...] += _mxu_dot(_silu(h1) * h3, w2_ref[...])

    @pl.when(j == pl.num_programs(1) - 1)
    def _():
        o_ref[...] = x_ref[...] + o_ref[...]


def ffn(x, gain, w1, w3, w2, tf=256):
    T, D = x.shape
    F = w1.shape[1]
    tm = _tile(T, 1088)
    one = pl.Buffered(1)
    return pl.pallas_call(
        _ffn_kernel,
        grid=(T // tm, F // tf),
        in_specs=[pl.BlockSpec((tm, D), lambda i, j: (i, 0), pipeline_mode=one),
                  pl.BlockSpec((1, D), lambda i, j: (0, 0)),
                  pl.BlockSpec((D, tf), lambda i, j: (0, j)),
                  pl.BlockSpec((D, tf), lambda i, j: (0, j)),
                  pl.BlockSpec((tf, D), lambda i, j: (j, 0))],
        out_specs=pl.BlockSpec((tm, D), lambda i, j: (i, 0), pipeline_mode=one),
        out_shape=jax.ShapeDtypeStruct((T, D), jnp.float32),
        scratch_shapes=[pltpu.VMEM((tm, D), jnp.bfloat16)],
        compiler_params=_cparams("parallel", "arbitrary"),
        name="ffn",
    )(x, gain.reshape(1, D), w1, w3, w2)


def _router_kernel(x_ref, g_ref, rt_ref, i1_ref, i2_ref, g1_ref, g2_ref, *, n_exp):
    xn = _rms(x_ref[...], g_ref[...])
    logit = [jnp.sum(xn * rt_ref[e:e + 1, :], axis=-1, keepdims=True) for e in range(n_exp)]
    neg = jnp.float32(-jnp.inf)

    def top(vals):
        m = vals[0]
        for v in vals[1:]:
            m = jnp.maximum(m, v)
        idx = jnp.full(m.shape, n_exp, jnp.int32)
        for e in reversed(range(n_exp)):
            idx = jnp.where(vals[e] == m, e, idx)
        return m, idx

    m1, i1 = top(logit)
    rest = [jnp.where(i1 == e, neg, logit[e]) for e in range(n_exp)]
    m2, i2 = top(rest)
    d = jnp.exp(m2 - m1)
    i1_ref[...] = i1
    i2_ref[...] = i2
    g1_ref[...] = 1.0 / (1.0 + d)
    g2_ref[...] = d / (1.0 + d)


def router_top2(x, gain, router):
    T, D = x.shape
    E = router.shape[1]
    tm = _tile(T, 544)
    col = pl.BlockSpec((tm, 1), lambda i: (i, 0))
    return pl.pallas_call(
        functools.partial(_router_kernel, n_exp=E),
        grid=(T // tm,),
        in_specs=[pl.BlockSpec((tm, D), lambda i: (i, 0)),
                  pl.BlockSpec((1, D), lambda i: (0, 0)),
                  pl.BlockSpec((E, D), lambda i: (0, 0))],
        out_specs=[col, col, col, col],
        out_shape=[jax.ShapeDtypeStruct((T, 1), jnp.int32)] * 2 + [jax.ShapeDtypeStruct((T, 1), jnp.float32)] * 2,
        compiler_params=_cparams("parallel"),
        name="router",
    )(x, gain.reshape(1, D), router.T)


MOE_SUB = 256
MOE_UNROLL = 4


def _moe_plan(i1, i2, n_exp, cap):
    T = i1.shape[0]
    n_tiles = (2 * T) // cap + n_exp
    e_a = jnp.concatenate([i1[:, 0], i2[:, 0]])
    tok = jnp.concatenate([jnp.arange(T, dtype=jnp.int32)] * 2)
    onehot = (e_a[:, None] == jnp.arange(n_exp, dtype=jnp.int32)[None, :]).astype(jnp.int32)
    csum = jnp.cumsum(onehot, axis=0)
    rank = jnp.sum((csum - onehot) * onehot, axis=1)
    counts = csum[-1]
    nt_e = (counts + cap - 1) // cap
    t_end = jnp.cumsum(nt_e)
    t_start = t_end - nt_e
    pos = (t_start[e_a] * cap + rank).astype(jnp.int32)
    n_used = t_end[-1]
    tiles = jnp.arange(n_tiles, dtype=jnp.int32)
    t_buf = jnp.minimum(tiles, n_used - 1).astype(jnp.int32)
    t_exp = jnp.minimum(jnp.sum((t_buf[:, None] >= t_end[None, :]).astype(jnp.int32), axis=1), n_exp - 1)
    t_rows = jnp.where(tiles < n_used, jnp.clip(counts[t_exp] - (tiles - t_start[t_exp]) * cap, 0, cap), 0)
    src = jnp.zeros((n_tiles * cap,), jnp.int32).at[pos].set(tok)
    return t_exp.astype(jnp.int32), t_rows.astype(jnp.int32), t_buf, src, pos[:T], pos[T:]


DMA_UNROLL = 8


def _row_copy(src_hbm, row, dst, r, sem):
    return pltpu.make_async_copy(src_hbm.at[pl.ds(row, 1)], dst.at[pl.ds(r, 1)], sem)


def _wait_rows(dst, sem):
    pltpu.make_async_copy(dst, dst, sem).wait()


def _gather_norm_kernel(src_ref, rows_ref, x_hbm, g_ref, o_ref, buf, sem, *, sub_per_tile):
    i = pl.program_id(0)
    n = pl.num_programs(0)
    tg = o_ref.shape[0]

    def active(step):
        return (step % sub_per_tile) * tg < rows_ref[step // sub_per_tile]

    def issue(step):
        slot = step % 2

        def one(r, c):
            _row_copy(x_hbm, src_ref[step * tg + r], buf.at[slot], r, sem.at[slot]).start()
            return c

        lax.fori_loop(0, tg, one, 0, unroll=DMA_UNROLL)

    @pl.when((i == 0) & active(0))
    def _():
        issue(0)

    nxt = jnp.minimum(i + 1, n - 1)

    @pl.when((i + 1 < n) & active(nxt))
    def _():
        issue(nxt)

    @pl.when(active(i))
    def _():
        slot = i % 2
        _wait_rows(buf.at[slot], sem.at[slot])
        o_ref[...] = _rms(buf[slot], g_ref[...]).astype(o_ref.dtype)

    @pl.when(jnp.logical_not(active(i)))
    def _():
        o_ref[...] = jnp.zeros_like(o_ref)


def gather_norm(x, gain, src, t_rows, cap):
    T, D = x.shape
    n_rows = src.shape[0]
    tg = MOE_SUB
    return pl.pallas_call(
        functools.partial(_gather_norm_kernel, sub_per_tile=cap // tg),
        grid_spec=pltpu.PrefetchScalarGridSpec(
            num_scalar_prefetch=2,
            grid=(n_rows // tg,),
            in_specs=[pl.BlockSpec(memory_space=pl.ANY),
                      pl.BlockSpec((1, D), lambda i, s, r: (0, 0))],
            out_specs=pl.BlockSpec((tg, D), lambda i, s, r: (i, 0)),
            scratch_shapes=[pltpu.VMEM((2, tg, D), jnp.float32), pltpu.SemaphoreType.DMA((2,))]),
        out_shape=jax.ShapeDtypeStruct((n_rows, D), jnp.bfloat16),
        compiler_params=_cparams("arbitrary"),
        name="moe_gather",
    )(src, t_rows, x, gain.reshape(1, D))


def _moe_kernel(te_ref, tr_ref, tb_ref, xg_ref, w1_ref, w3_ref, w2_ref, o_ref):
    k = pl.program_id(0)
    j = pl.program_id(1)
    rows = tr_ref[k]

    @pl.when(j == 0)
    def _():
        o_ref[...] = jnp.zeros_like(o_ref)

    @pl.when(rows > 0)
    def _():
        w1 = w1_ref[...].astype(jnp.bfloat16)
        w3 = w3_ref[...].astype(jnp.bfloat16)
        w2 = w2_ref[...].astype(jnp.bfloat16)

        def up(s):
            xs = xg_ref[pl.ds(pl.multiple_of(s * MOE_SUB, MOE_SUB), MOE_SUB), :]
            h1 = jnp.dot(xs, w1, preferred_element_type=jnp.float32)
            h3 = jnp.dot(xs, w3, preferred_element_type=jnp.float32)
            return (_silu(h1) * h3).astype(jnp.bfloat16)

        def down(s, h):
            r0 = pl.multiple_of(s * MOE_SUB, MOE_SUB)
            o_ref[pl.ds(r0, MOE_SUB), :] += jnp.dot(h, w2, preferred_element_type=jnp.float32)

        def body(s, h):
            h_next = up(s + 1)
            down(s, h)
            return h_next

        def body_n(t, h):
            for i in range(MOE_UNROLL):
                h = body(MOE_UNROLL * t + i, h)
            return h

        last = (rows + MOE_SUB - 1) // MOE_SUB - 1
        h = lax.fori_loop(0, last // MOE_UNROLL, body_n, up(0))
        h = lax.fori_loop(MOE_UNROLL * (last // MOE_UNROLL), last, body, h)
        down(last, h)


def moe_experts(xg, t_exp, t_rows, t_buf, w1, w3, w2, cap, tf=256):
    n_rows, D = xg.shape
    E, _, F = w1.shape
    n_tiles = n_rows // cap
    nj = F // tf

    def jj(k, j, tr):
        return jnp.where(tr[k] > 0, j, nj - 1)

    one = pl.Buffered(1)
    return pl.pallas_call(
        _moe_kernel,
        grid_spec=pltpu.PrefetchScalarGridSpec(
            num_scalar_prefetch=3,
            grid=(n_tiles, nj),
            in_specs=[pl.BlockSpec((cap, D), lambda k, j, te, tr, tb: (tb[k], 0), pipeline_mode=one),
                      pl.BlockSpec((None, D, tf), lambda k, j, te, tr, tb: (te[k], 0, jj(k, j, tr))),
                      pl.BlockSpec((None, D, tf), lambda k, j, te, tr, tb: (te[k], 0, jj(k, j, tr))),
                      pl.BlockSpec((None, tf, D), lambda k, j, te, tr, tb: (te[k], jj(k, j, tr), 0))],
            out_specs=pl.BlockSpec((cap, D), lambda k, j, te, tr, tb: (k, 0), pipeline_mode=one)),
        out_shape=jax.ShapeDtypeStruct((n_rows, D), jnp.float32),
        compiler_params=_cparams("arbitrary", "arbitrary"),
        name="moe_experts",
    )(t_exp, t_rows, t_buf, xg, w1, w3, w2)


def _combine_kernel(p1_ref, p2_ref, x_ref, g1_ref, g2_ref, *rest, final_norm, n_first_tiles):
    if final_norm:
        gain_ref, y_hbm, oa_ref, ob_ref, a_buf, b_buf, sem = rest
    else:
        y_hbm, oa_ref, ob_ref, a_buf, b_buf, sem = rest
    i = pl.program_id(0)
    tc = x_ref.shape[0]

    def issue(step):
        slot = step % 2

        def one(r, c):
            _row_copy(y_hbm, p1_ref[step * tc + r], a_buf.at[slot], r, sem.at[0, slot]).start()
            _row_copy(y_hbm, p2_ref[step * tc + r], b_buf.at[slot], r, sem.at[1, slot]).start()
            return c

        lax.fori_loop(0, tc, one, 0, unroll=DMA_UNROLL)

    @pl.when(i == 0)
    def _():
        issue(0)

    @pl.when(i + 1 < pl.num_programs(0))
    def _():
        issue(i + 1)

    slot = i % 2
    _wait_rows(a_buf.at[slot], sem.at[0, slot])
    _wait_rows(b_buf.at[slot], sem.at[1, slot])
    y = x_ref[...] + (g1_ref[...] * a_buf[slot] + g2_ref[...] * b_buf[slot])
    y = _rms(y, gain_ref[...]) if final_norm else y

    @pl.when(i < n_first_tiles)
    def _():
        oa_ref[...] = y

    @pl.when(i >= n_first_tiles)
    def _():
        ob_ref[...] = y


def moe_combine(x, yg, pos1, pos2, g1, g2, n_first, final_gain=None):
    T, D = x.shape
    tc = _tile(math.gcd(n_first, T - n_first), 256)
    nf = n_first // tc
    final_norm = final_gain is not None
    tok = pl.BlockSpec((tc, D), lambda i, a, b: (i, 0))
    col = pl.BlockSpec((tc, 1), lambda i, a, b: (i, 0))
    in_specs = [tok, col, col] + ([pl.BlockSpec((1, D), lambda i, a, b: (0, 0))] if final_norm else [])
    in_specs.append(pl.BlockSpec(memory_space=pl.ANY))
    args = (x, g1, g2) + ((final_gain.reshape(1, D),) if final_norm else ()) + (yg,)
    return pl.pallas_call(
        functools.partial(_combine_kernel, final_norm=final_norm, n_first_tiles=nf),
        grid_spec=pltpu.PrefetchScalarGridSpec(
            num_scalar_prefetch=2,
            grid=(T // tc,),
            in_specs=in_specs,
            out_specs=[pl.BlockSpec((tc, D), lambda i, a, b: (jnp.minimum(i, nf - 1), 0)),
                       pl.BlockSpec((tc, D), lambda i, a, b: (jnp.maximum(i - nf, 0), 0))],
            scratch_shapes=[pltpu.VMEM((2, tc, D), jnp.float32), pltpu.VMEM((2, tc, D), jnp.float32),
                            pltpu.SemaphoreType.DMA((2, 2))]),
        out_shape=[jax.ShapeDtypeStruct((n_first, D), jnp.float32),
                   jax.ShapeDtypeStruct((T - n_first, D), jnp.float32)],
        compiler_params=_cparams("arbitrary"),
        name="moe_combine",
    )(pos1, pos2, *args)


def moe_top2(x, gain, router, w1, w3, w2, n_first, final_gain=None):
    T, D = x.shape
    E = router.shape[1]
    i1, i2, g1, g2 = router_top2(x, gain, router)
    cap = -(-(2 * T * 21 // (20 * E)) // MOE_SUB) * MOE_SUB
    t_exp, t_rows, t_buf, src, pos1, pos2 = _moe_plan(i1, i2, E, cap)
    xg = gather_norm(x, gain, src, t_rows, cap)
    yg = moe_experts(xg, t_exp, t_rows, t_buf, w1, w3, w2, cap)
    return moe_combine(x, yg, pos1, pos2, g1, g2, n_first, final_gain)


def _cumsum_rows(x):
    n = x.shape[0]
    row = lax.broadcasted_iota(jnp.int32, x.shape, 0)
    s = 1
    while s < n:
        x = x + jnp.where(row >= s, pltpu.roll(x, s, 0), 0.0)
        s *= 2
    return x


def _hgrn_kernel(q_ref, f_ref, i_ref, lbl_ref, s0_ref, o_ref, so_ref, s_scr, *, chunk, layer_j, valid_len):
    c = pl.program_id(1)
    tl = q_ref.shape[0]
    n_head, dk, _ = s_scr.shape

    @pl.when(c == 0)
    def _():
        s_scr[...] = s0_ref[...]

    lg = lbl_ref[...]
    ex = jnp.exp(lg - jnp.max(lg, axis=0, keepdims=True))
    lb_all = jnp.sum(ex[0:layer_j + 1, :], axis=0, keepdims=True) / jnp.sum(ex, axis=0, keepdims=True)

    pad = LANES - chunk
    zpad = jnp.zeros((pad, dk), jnp.float32)
    row = lax.broadcasted_iota(jnp.int32, (chunk, LANES), 0)
    col = lax.broadcasted_iota(jnp.int32, (chunk, LANES), 1)
    causal = col <= row
    ones_cd = jnp.ones((chunk, dk), jnp.float32)
    tn_dims = (((0,), (0,)), ((), ()))
    nt_dims = (((1,), (1,)), ((), ()))

    for cc in range(tl // chunk):
        sl = slice(cc * chunk, (cc + 1) * chunk)
        for h in range(n_head):
            hs = slice(h * dk, (h + 1) * dk)
            lb = lb_all[:, hs]
            f = lb + (1.0 - lb) * jax.nn.sigmoid(f_ref[sl, hs])
            q = _silu(q_ref[sl, hs])
            v = i_ref[sl, hs]
            if valid_len is not None:
                t_idx = c * tl + cc * chunk + lax.broadcasted_iota(jnp.int32, (chunk, dk), 0)
                ok = t_idx < valid_len
                f = jnp.where(ok, f, 1.0)
                q = jnp.where(ok, q, 0.0)
            g = jnp.log(f)
            k = 1.0 - f
            b = _cumsum_rows(g)
            mid = chunk // 2 - 1
            bm = b[mid:mid + 1, :]
            bl = b[chunk - 1:chunk, :]
            qe = q * jnp.exp(b - bm)
            ke = jnp.concatenate([k * jnp.exp(bm - b), zpad], axis=0)
            vp = jnp.concatenate([v, zpad], axis=0)
            a = lax.dot_general(qe, ke, nt_dims, preferred_element_type=jnp.float32)
            a = jnp.where(causal, a, 0.0)
            s = s_scr[h]
            o = jnp.dot(a, vp, preferred_element_type=jnp.float32)
            o = o + jnp.dot(q * jnp.exp(b), s, preferred_element_type=jnp.float32)
            o_ref[sl, hs] = o
            kd = jnp.concatenate([k * jnp.exp(bl - b), zpad], axis=0)
            ghi, glo = _split_bf16(g)
            blc = (lax.dot_general(ghi, ones_cd, tn_dims, preferred_element_type=jnp.float32)
                   + lax.dot_general(glo, ones_cd, tn_dims, preferred_element_type=jnp.float32))
            s_scr[h] = jnp.exp(blc) * s + lax.dot_general(kd, vp, tn_dims, preferred_element_type=jnp.float32)

    @pl.when(c == pl.num_programs(1) - 1)
    def _():
        so_ref[...] = s_scr[...]


def hgrn_scan(pa, B, L, tl, chunk, lb_logits, s0, layer_j, valid_len=None):
    H, dk = s0.shape[1], s0.shape[2]
    d_a = H * dk
    nt = L // tl
    kern = functools.partial(_hgrn_kernel, chunk=chunk, layer_j=layer_j, valid_len=valid_len)

    def col(off):
        return pl.BlockSpec((tl, d_a), lambda b, c: (b * nt + c, off))

    st = pl.BlockSpec((None, H, dk, dk), lambda b, c: (b, 0, 0, 0))
    return pl.pallas_call(
        kern,
        grid=(B, nt),
        in_specs=[col(0), col(1), col(2), pl.BlockSpec(lb_logits.shape, lambda b, c: (0, 0)), st],
        out_specs=[pl.BlockSpec((tl, d_a), lambda b, c: (b * nt + c, 0)), st],
        out_shape=[jax.ShapeDtypeStruct((B * L, d_a), jnp.float32),
                   jax.ShapeDtypeStruct(s0.shape, jnp.float32)],
        scratch_shapes=[pltpu.VMEM((H, dk, dk), jnp.float32)],
        compiler_params=_cparams("parallel", "arbitrary"),
        name="hgrn_scan",
    )(pa, pa, pa, lb_logits, s0)


def _seg_sum(x, bones):
    outs = []
    for j in range(x.shape[1] // LANES):
        hi, lo = _split_bf16(x[:, j * LANES:(j + 1) * LANES])
        outs.append(jnp.dot(hi, bones, preferred_element_type=jnp.float32)
                    + jnp.dot(lo, bones, preferred_element_type=jnp.float32))
    return jnp.concatenate(outs, axis=1)


def _rwkv_prep_kernel(u_ref, halo_ref, e_ref, pos_ref, mu_ref, w0_ref, a0_ref, kk_ref, ka_ref, rk_ref,
                      w2_ref, a2_ref, g2_ref, bones_ref,
                      kk_o, wr_o, w_o, kka_o, km_o, vhi_o, vlo_o, g_o, c1_o, c2_o, bon_o,
                      *, d_b, first_state_tile):
    u = u_ref[...]
    has_state = pl.program_id(0) >= first_state_tile
    u_prev = _prev_in_seq(u, halo_ref[...], 1, pos_ref[...], jnp.where(has_state, e_ref[...], 0.0))
    us = u + (u_prev - u) * mu_ref[...]
    r = us[:, 0:d_b]
    k = us[:, d_b:2 * d_b]
    v = us[:, 2 * d_b:3 * d_b]
    wa = us[:, 3 * d_b:3 * d_b + LANES]
    gd = us[:, 3 * d_b + LANES:3 * d_b + 2 * LANES]
    bones = bones_ref[...]
    z = w0_ref[...] + jnp.dot(jnp.tanh(wa), w2_ref[...], preferred_element_type=jnp.float32)
    nz = -z
    softplus = jnp.maximum(nz, 0.0) + jnp.log(1.0 + jnp.exp(-jnp.abs(nz)))
    w_log = -softplus - 0.5
    decay = jnp.exp(-jnp.exp(w_log))
    a = jax.nn.sigmoid(a0_ref[...] + jnp.dot(wa, a2_ref[...], preferred_element_type=jnp.float32))
    g = jnp.dot(jax.nn.sigmoid(gd), g2_ref[...], preferred_element_type=jnp.float32)
    kk = k * kk_ref[...]
    kk = kk / jnp.maximum(jnp.sqrt(_seg_sum(kk * kk, bones)), 1e-12)
    kmod = k * (1.0 + (a - 1.0) * ka_ref[...])
    kka = kk * a
    vhi, vlo = _split_bf16(v)
    kk_o[...] = kk
    wr_o[...] = decay * r
    w_o[...] = decay
    kka_o[...] = kka
    km_o[...] = kmod
    vhi_o[...] = vhi
    vlo_o[...] = vlo
    g_o[...] = g
    c1_o[...] = _seg_sum(kka * r, bones)
    c2_o[...] = _seg_sum(kmod * r, bones)
    bon_o[...] = _seg_sum(r * kmod * rk_ref[...], bones)


def rwkv_prep(u, pos, e1, tm, n_zero_rows, mu, w0, a0, k_k, k_a, r_k, w2p, a2p, g2, bones, d_b):
    T, DU = u.shape
    row = lambda n: pl.BlockSpec((1, n), lambda i: (0, 0))
    full = lambda a: pl.BlockSpec(a.shape, lambda i: (0, 0))
    first, specs = _seq_tiles(tm, n_zero_rows)
    tok_u, halo_u, start_u = specs(DU)
    tok = pl.BlockSpec((tm, d_b), lambda i: (i, 0))
    return pl.pallas_call(
        functools.partial(_rwkv_prep_kernel, d_b=d_b, first_state_tile=first),
        grid=(T // tm,),
        in_specs=[tok_u, halo_u, start_u, pl.BlockSpec((tm, 1), lambda i: (i, 0)),
                  row(DU), row(d_b), row(d_b), row(d_b), row(d_b), row(d_b),
                  full(w2p), full(a2p), full(g2), full(bones)],
        out_specs=[tok] * 11,
        out_shape=[jax.ShapeDtypeStruct((T, d_b), jnp.float32)] * 11,
        compiler_params=_cparams("parallel"),
        name="rwkv_prep",
    )(u, u, e1, pos, mu.reshape(1, DU), w0.reshape(1, d_b), a0.reshape(1, d_b), k_k.reshape(1, d_b),
      k_a.reshape(1, d_b), r_k.reshape(1, d_b), w2p, a2p, g2, bones)


SCAN_GROUP = 32


def _rwkv_scan_kernel(kk_ref, wr_ref, w_ref, kka_ref, k_ref, vhi_ref, vlo_ref, c1_ref, c2_ref,
                      s0_ref, rhs_ref, md_ref, y_ref, so_ref, s_scr, r2_o, sa_o, *, n_pair):
    tb = kk_ref.shape[-2]
    n = HEAD_B

    @pl.when(pl.program_id(1) == 0)
    def _():
        s_scr[...] = s0_ref[...]

    r2_o[...] = jnp.zeros_like(r2_o)
    sa_o[...] = jnp.zeros_like(sa_o)
    rhs = rhs_ref[...]
    md = md_ref[...]
    lane = lax.broadcasted_iota(jnp.int32, (n, LANES), 1) & (n - 1)

    def token_step(row, sel):
        parts = []
        for p in range(n_pair):
            sl = slice(p * LANES, (p + 1) * LANES)
            s = s_scr[p]
            p1 = s * row(kk_ref, sl)
            p2 = s * row(wr_ref, sl)
            dh = md * row(vhi_ref, sl)
            dl = md * row(vlo_ref, sl)
            parts.append(jnp.concatenate([p1, dh], axis=1))
            parts.append(jnp.concatenate([p2, dl], axis=1))
        out = jnp.dot(jnp.concatenate(parts, axis=0), rhs, preferred_element_type=jnp.float32)
        for p in range(n_pair):
            sl = slice(p * LANES, (p + 1) * LANES)
            top = out[2 * p * n:(2 * p + 1) * n]
            bot = out[(2 * p + 1) * n:(2 * p + 2) * n]
            sa_b = top[:, 0:LANES]
            r2_b = bot[:, 0:LANES]
            v_b = top[:, LANES:2 * LANES] + bot[:, LANES:2 * LANES]
            s = s_scr[p]
            s_scr[p] = s * row(w_ref, sl) - sa_b * row(kka_ref, sl) + v_b * row(k_ref, sl)
            r2_o[p] = jnp.where(sel, r2_b, r2_o[p])
            sa_o[p] = jnp.where(sel, sa_b, sa_o[p])

    if tb % SCAN_GROUP == 0:
        def group(gi, carry):
            for sub in range(SCAN_GROUP // SUBLANES):
                t0 = gi * SCAN_GROUP + sub * SUBLANES
                base = pl.multiple_of(t0, SUBLANES)
                for jj in range(SUBLANES):
                    token_step(lambda ref, sl, jj=jj, base=base: ref[pl.ds(base, SUBLANES), sl][jj:jj + 1, :],
                               lane == t0 + jj)
            return carry

        lax.fori_loop(0, tb // SCAN_GROUP, group, 0)
    else:
        for tt in range(tb):
            token_step(lambda ref, sl, tt=tt: ref[tt:tt + 1, sl], lane == tt)

    for p in range(n_pair):
        sl = slice(p * LANES, (p + 1) * LANES)
        zt = jnp.concatenate([r2_o[p], sa_o[p]], axis=0).T
        r2 = jnp.concatenate([zt[0:tb, 0:n], zt[n:n + tb, 0:n]], axis=1)
        sa = jnp.concatenate([zt[0:tb, n:2 * n], zt[n:n + tb, n:2 * n]], axis=1)
        v = vhi_ref[:, sl] + vlo_ref[:, sl]
        y_ref[:, sl] = r2 - sa * c1_ref[:, sl] + v * c2_ref[:, sl]

    @pl.when(pl.program_id(1) == pl.num_programs(1) - 1)
    def _():
        so_ref[...] = s_scr[...]


def rwkv_scan(vecs, tok, y_spec, y_shape, B, nt, s0_pair, rhs, md):
    n_pair = s0_pair.shape[1]
    st = pl.BlockSpec((None, n_pair, HEAD_B, LANES), lambda b, t: (b, 0, 0, 0))
    acc = pltpu.VMEM((n_pair, HEAD_B, LANES), jnp.float32)
    return pl.pallas_call(
        functools.partial(_rwkv_scan_kernel, n_pair=n_pair),
        grid=(B, nt),
        in_specs=[tok] * 9 + [st, pl.BlockSpec(rhs.shape, lambda b, t: (0, 0)),
                              pl.BlockSpec(md.shape, lambda b, t: (0, 0))],
        out_specs=[y_spec, st],
        out_shape=[y_shape, jax.ShapeDtypeStruct(s0_pair.shape, jnp.float32)],
        scratch_shapes=[acc, acc, acc],
        compiler_params=_cparams("parallel", "arbitrary"),
        name="rwkv_scan",
    )(*vecs, s0_pair, rhs, md)


def _mix_out_kernel(oa1_ref, oa2_ref, go_ref, hg_ref, y1_ref, y2_ref, vhi_ref, vlo_ref, g_ref, bon_ref,
                    lnw_ref, lnb_ref, bones_ref, o_ref, *, d_a, n_first_tiles):
    first = pl.program_id(0) < n_first_tiles
    oa = jnp.where(first, oa1_ref[...], oa2_ref[...])
    outs = []
    for h in range(d_a // DK_A):
        x = oa[:, h * DK_A:(h + 1) * DK_A]
        outs.append(x * lax.rsqrt(jnp.mean(x * x, axis=-1, keepdims=True) + NORM_EPS))
    o_a = jnp.concatenate(outs, axis=1) * hg_ref[...] * _silu(go_ref[...])
    bones = bones_ref[...]
    v = vhi_ref[...] + vlo_ref[...]
    y = jnp.where(first, y1_ref[...], y2_ref[...])
    inv_n = 1.0 / HEAD_B
    mean = _seg_sum(y, bones) * inv_n
    d = y - mean
    var = _seg_sum(d * d, bones) * inv_n
    yn = d * lax.rsqrt(var + RWKV_GN_EPS) * lnw_ref[...] + lnb_ref[...]
    o_b = (yn + bon_ref[...] * v) * g_ref[...]
    o_ref[:, 0:d_a] = o_a.astype(o_ref.dtype)
    o_ref[:, d_a:] = o_b.astype(o_ref.dtype)


def mix_out(oa_pair, pa, hg_row, y_pair, vhi, vlo, g, bon, ln_w, ln_b, bones):
    n_first, d_a = oa_pair[0].shape
    n_rest = oa_pair[1].shape[0]
    T = n_first + n_rest
    d_b = y_pair[0].shape[1]
    tm = _tile(math.gcd(n_first, n_rest), 256, 2 * SUBLANES)
    nf = n_first // tm
    tok = lambda n: pl.BlockSpec((tm, n), lambda i: (i, 0))
    row = lambda n: pl.BlockSpec((1, n), lambda i: (0, 0))
    lead = lambda n: pl.BlockSpec((tm, n), lambda i: (jnp.minimum(i, nf - 1), 0))
    rest = lambda n: pl.BlockSpec((tm, n), lambda i: (jnp.maximum(i - nf, 0), 0))
    return pl.pallas_call(
        functools.partial(_mix_out_kernel, d_a=d_a, n_first_tiles=nf),
        grid=(T // tm,),
        in_specs=[lead(d_a), rest(d_a), pl.BlockSpec((tm, d_a), lambda i: (i, 3)), row(d_a),
                  lead(d_b), rest(d_b)]
                 + [tok(d_b)] * 4 + [row(d_b), row(d_b), pl.BlockSpec(bones.shape, lambda i: (0, 0))],
        out_specs=tok(d_a + d_b),
        out_shape=jax.ShapeDtypeStruct((T, d_a + d_b), jnp.bfloat16),
        compiler_params=_cparams("parallel"),
        name="mix_out",
    )(*oa_pair, pa, hg_row, *y_pair, vhi, vlo, g, bon, ln_w.reshape(1, d_b), ln_b.reshape(1, d_b), bones)


def _conv_in_kernel(x_ref, g_ref, wb_ref, wc_ref, wh_ref, gb_ref, u_ref, xn_ref):
    @pl.when(pl.program_id(1) == 0)
    def _():
        xn_ref[...] = _rms(x_ref[...], g_ref[...]).astype(xn_ref.dtype)

    xn = xn_ref[...]
    gb_ref[...] = _mxu_dot(xn, wb_ref[...])
    u_ref[...] = _mxu_dot(xn, wc_ref[...]) * _mxu_dot(xn, wh_ref[...])


def conv_in(x, gain, w, tn=256):
    T, K = x.shape
    d = w.shape[1] // 3
    tm = _tile(T, 1088)
    nb = d // tn
    out = pl.BlockSpec((tm, tn), lambda i, j: (i, j))
    return pl.pallas_call(
        _conv_in_kernel,
        grid=(T // tm, nb),
        in_specs=[pl.BlockSpec((tm, K), lambda i, j: (i, 0)),
                  pl.BlockSpec((1, K), lambda i, j: (0, 0)),
                  pl.BlockSpec((K, tn), lambda i, j: (0, j)),
                  pl.BlockSpec((K, tn), lambda i, j: (0, nb + j)),
                  pl.BlockSpec((K, tn), lambda i, j: (0, 2 * nb + j))],
        out_specs=[out, out],
        out_shape=[jax.ShapeDtypeStruct((T, d), jnp.float32)] * 2,
        scratch_shapes=[pltpu.VMEM((tm, K), jnp.bfloat16)],
        compiler_params=_cparams("parallel", "arbitrary"),
        name="conv_in",
    )(x, gain.reshape(1, K), w, w, w)


def _prev_rows(tile, halo, k):
    rolled = pltpu.roll(tile, k, 0)
    row = lax.broadcasted_iota(jnp.int32, halo.shape, 0)
    head = jnp.where(row < k, pltpu.roll(halo, k, 0), rolled[0:SUBLANES])
    return jnp.concatenate([head, rolled[SUBLANES:]], axis=0)


def _prev_in_seq(tile, halo, k, pos, start_rows):
    return jnp.where(pos < k, start_rows, _prev_rows(tile, halo, k))


def _conv_mm_kernel(gb_ref, u_ref, halo_ref, pos_ref, e1_ref, e2_ref, ck_ref, w_ref, r_ref, o_ref,
                    *, first_state_tile):
    u = u_ref[...]
    pos = pos_ref[...]
    has_state = pl.program_id(0) >= first_state_tile
    u1 = _prev_in_seq(u, halo_ref[...], 1, pos, jnp.where(has_state, e1_ref[...], 0.0))
    u2 = _prev_in_seq(u, halo_ref[...], 2, pos, jnp.where(has_state, e2_ref[...], 0.0))
    y = ck_ref[0:1, :] * u2
    y = y + ck_ref[1:2, :] * u1
    y = y + ck_ref[2:3, :] * u
    o_ref[...] = r_ref[...] + _mxu_dot(gb_ref[...] * y, w_ref[...])


def _seq_tiles(tm, n_zero_rows):
    first = n_zero_rows // tm
    hb = tm // SUBLANES

    def specs(n):
        return (pl.BlockSpec((tm, n), lambda i, *_: (i, 0)),
                pl.BlockSpec((SUBLANES, n), lambda i, *_: (jnp.maximum(i * hb - 1, 0), 0)),
                pl.BlockSpec((tm, n), lambda i, *_: (jnp.maximum(i - first, 0), 0)))

    return first, specs


def conv_mm(gb, u, pos, e1, e2, conv_k, w, res, tm, n_zero_rows):
    T, d = u.shape
    N = w.shape[1]
    assert conv_k.shape[0] == 3
    first, specs = _seq_tiles(tm, n_zero_rows)
    tok, halo, start = specs(d)
    one = pl.Buffered(1)
    start = pl.BlockSpec(start.block_shape, start.index_map, pipeline_mode=one)
    return pl.pallas_call(
        functools.partial(_conv_mm_kernel, first_state_tile=first),
        grid=(T // tm,),
        in_specs=[tok, tok, halo, pl.BlockSpec((tm, 1), lambda i: (i, 0)), start, start,
                  pl.BlockSpec(conv_k.shape, lambda i: (0, 0)),
                  pl.BlockSpec((d, N), lambda i: (0, 0), pipeline_mode=one),
                  pl.BlockSpec((tm, N), lambda i: (i, 0))],
        out_specs=pl.BlockSpec((tm, N), lambda i: (i, 0)),
        out_shape=jax.ShapeDtypeStruct((T, N), jnp.float32),
        compiler_params=_cparams("parallel"),
        name="conv_mm",
    )(gb, u, u, pos, e1, e2, conv_k, w, res)


def _start_rows(state, L, k):
    B, ns, N = state.shape
    assert k <= min(L, ns)
    return jnp.zeros((B, L, N), state.dtype).at[:, :k].set(state[:, ns - k:]).reshape(B * L, N)


def _last_rows(x, groups, k):
    outs = []
    row = 0
    for B, L in groups:
        idx = [row + b * L + L - k + i for b in range(B) for i in range(k)]
        outs.append(jnp.take(x, jnp.asarray(idx, jnp.int32), axis=0).reshape(B, k, -1))
        row += B * L
    return outs


def _pair_state(s):
    B, H, n, _ = s.shape
    return s.reshape(B, H // 2, 2, n, n).transpose(0, 1, 3, 2, 4).reshape(B, H // 2, n, 2 * n)


def _unpair_state(s):
    B, hp, n, _ = s.shape
    return s.reshape(B, hp, n, 2, n).transpose(0, 1, 3, 2, 4).reshape(B, 2 * hp, n, n)


def kernel(x_prompt, x_sample, state_hgrn, state_rwkv, state_shift, state_conv, norm_mix, w_in_ab,
           hgrn_lb_logits, hgrn_norm, rwkv_mu, rwkv_w0, rwkv_w2, rwkv_a0, rwkv_a2, rwkv_g2, rwkv_k_k,
           rwkv_k_a, rwkv_r_k, rwkv_ln_w, rwkv_ln_b, w_out_ab, norm_ffn, ffn_w1, ffn_w3, ffn_w2,
           conv_w_in, conv_k, conv_w_out, moe_router, moe_w1, moe_w3, moe_w2, norm_final):
    f32 = jnp.float32
    Bp, Lp, D = x_prompt.shape
    Bs, Ls, _ = x_sample.shape
    Tp, Ts = Bp * Lp, Bs * Ls
    groups = ((Bp, Lp), (Bs, Ls))
    n_even = w_in_ab.shape[0]
    n_odd = conv_w_in.shape[0]
    depth = n_even + n_odd
    H_A, dk = state_hgrn.shape[2], state_hgrn.shape[3]
    d_a = H_A * dk
    H_B, hb = state_rwkv.shape[2], state_rwkv.shape[3]
    d_b = H_B * hb
    d_shift = state_shift.shape[-1]
    lora_w = rwkv_w2.shape[1]
    lora_a = rwkv_a2.shape[1]
    assert hb == HEAD_B and dk == DK_A and lora_w + lora_a == LANES and H_B % 2 == 0
    assert w_in_ab.shape[2] == 4 * d_a + d_shift

    seg = jnp.arange(LANES) // HEAD_B
    bones = (seg[:, None] == seg[None, :]).astype(f32)
    seg2 = jnp.arange(2 * LANES) // HEAD_B
    rhs_scan = (seg2[:, None] == seg2[None, :]).astype(f32)
    md = (jnp.arange(LANES)[None, :] % HEAD_B == jnp.arange(HEAD_B)[:, None]).astype(f32)

    x = jnp.concatenate([x_prompt.reshape(Tp, D), x_sample.reshape(Ts, D)], axis=0)
    pos = jnp.concatenate([jnp.tile(jnp.arange(Lp, dtype=jnp.int32), Bp),
                           jnp.tile(jnp.arange(Ls, dtype=jnp.int32), Bs)]).reshape(Tp + Ts, 1)
    seq_tile = math.gcd(Tp, Ts)
    assert seq_tile % SUBLANES == 0
    new_h, new_r, new_s, new_c = ([], []), ([], []), ([], []), ([], [])

    for layer in range(depth):
        j = layer // 2
        if layer % 2 == 0:
            w_in = w_in_ab[j]
            tn_u = _tile(d_shift, 512, LANES)
            assert (4 * d_a) % tn_u == 0
            pa = rms_mm(x, norm_mix[layer], w_in, tn=512, n_cols=4 * d_a)
            u = rms_mm(x, norm_mix[layer], w_in, tn=tn_u, col0=4 * d_a)
            zeros_h = jnp.zeros((Bp,) + state_hgrn.shape[2:], f32)
            tl = _tile(Lp, 64)
            oa_p, sh_p = hgrn_scan(pa, Bp, Lp, tl, min(32, tl), hgrn_lb_logits, zeros_h, j)
            lpad = -(-Ls // SUBLANES) * SUBLANES
            pa_s = jnp.pad(pa[Tp:, :3 * d_a].reshape(Bs, Ls, 3 * d_a), ((0, 0), (0, lpad - Ls), (0, 0)))
            oa_s, sh_s = hgrn_scan(pa_s.reshape(Bs * lpad, 3 * d_a), Bs, lpad, lpad, lpad, hgrn_lb_logits,
                                   state_hgrn[j], j, valid_len=Ls)
            oa = (oa_p, oa_s.reshape(Bs, lpad, d_a)[:, :Ls].reshape(Ts, d_a))
            new_h[0].append(sh_p)
            new_h[1].append(sh_s)
            e_shift = _start_rows(state_shift[j][:, None, :], Ls, 1)
            w2p = jnp.concatenate([rwkv_w2[j], jnp.zeros((lora_a, d_b), f32)], axis=0)
            a2p = jnp.concatenate([jnp.zeros((lora_w, d_b), f32), rwkv_a2[j]], axis=0)
            prep = rwkv_prep(u, pos, e_shift, _tile(seq_tile, 256), Tp, rwkv_mu[j], rwkv_w0[j], rwkv_a0[j],
                             rwkv_k_k[j], rwkv_k_a[j], rwkv_r_k[j].reshape(d_b), w2p, a2p, rwkv_g2[j],
                             bones, d_b)
            kk, wr, wdec, kka, kmod, vhi, vlo, gg, c1, c2, bon = prep
            scan_in = (kk, wr, wdec, kka, kmod, vhi, vlo, c1, c2)
            tb_p = _tile(Lp, HEAD_B)
            nt_p = Lp // tb_p
            zeros_r = jnp.zeros((Bp, H_B // 2, hb, 2 * hb), f32)
            tok_p = pl.BlockSpec((tb_p, d_b), lambda b, t: (b * nt_p + t, 0))
            y_p, sr_p = rwkv_scan(scan_in, tok_p, tok_p, jax.ShapeDtypeStruct((Tp, d_b), f32), Bp, nt_p,
                                  zeros_r, rhs_scan, md)
            tok_s = pl.BlockSpec((None, Ls, d_b), lambda b, t: (b, 0, 0))
            y_s, sr_s = rwkv_scan([a[Tp:].reshape(Bs, Ls, d_b) for a in scan_in], tok_s, tok_s,
                                  jax.ShapeDtypeStruct((Bs, Ls, d_b), f32), Bs, 1,
                                  _pair_state(state_rwkv[j]), rhs_scan, md)
            yb = (y_p, y_s.reshape(Ts, d_b))
            new_r[0].append(_unpair_state(sr_p))
            new_r[1].append(_unpair_state(sr_s))
            last_u = _last_rows(u, groups, 1)
            new_s[0].append(last_u[0][:, 0])
            new_s[1].append(last_u[1][:, 0])
            hg_row = jnp.tile(hgrn_norm[j], H_A).reshape(1, d_a)
            o = mix_out(oa, pa, hg_row, yb, vhi, vlo, gg, bon, rwkv_ln_w[j], rwkv_ln_b[j], bones)
            x = mm_res(o, w_out_ab[j], x)
            x = ffn(x, norm_ffn[layer], ffn_w1[j], ffn_w3[j], ffn_w2[j])
        else:
            d_c = state_conv.shape[-1]
            cw = state_conv.shape[2]
            gb, uc = conv_in(x, norm_mix[layer], conv_w_in[j])
            last_c = _last_rows(uc, groups, cw)
            new_c[0].append(last_c[0])
            new_c[1].append(last_c[1])
            x = conv_mm(gb, uc, pos, _start_rows(state_conv[j], Ls, 1), _start_rows(state_conv[j], Ls, 2),
                        conv_k[j], conv_w_out[j], x, _tile(seq_tile, 256), Tp)
            xp, xs = moe_top2(x, norm_ffn[layer], moe_router[j], moe_w1[j], moe_w3[j], moe_w2[j], Tp,
                              final_gain=norm_final if layer == depth - 1 else None)
            if layer < depth - 1:
                x = jnp.concatenate([xp, xs], axis=0)

    if depth % 2 == 1:
        y = rms_only(x, norm_final)
        xp, xs = y[:Tp], y[Tp:]
    outs = [xp.reshape(Bp, Lp, D), xs.reshape(Bs, Ls, D)]
    for g in (0, 1):
        outs += [jnp.stack(new_h[g]), jnp.stack(new_r[g]), jnp.stack(new_s[g]), jnp.stack(new_c[g])]
    return tuple(outs)
```

```python
import functools
import math

import jax
import jax.numpy as jnp
from jax import lax
from jax.experimental import pallas as pl
from jax.experimental.pallas import tpu as pltpu

NORM_EPS = 1e-6
RWKV_GN_EPS = 64e-5
HEAD_B = 64
DK_A = 128
LANES = 128
SUBLANES = 8
VMEM_LIMIT = 56 * 1024 * 1024


def _cparams(*sem):
    return pltpu.CompilerParams(dimension_semantics=sem, vmem_limit_bytes=VMEM_LIMIT)


def _tile(n, cap, mult=SUBLANES):
    best = None
    for d in range(mult, min(n, cap) + 1, mult):
        if n % d == 0:
            best = d
    return n if best is None else best


def _rms(x, gain):
    return x * lax.rsqrt(jnp.mean(x * x, axis=-1, keepdims=True) + NORM_EPS) * gain


def _silu(x):
    return x * jax.nn.sigmoid(x)


def _mxu_dot(a, w):
    return jnp.dot(a.astype(jnp.bfloat16), w.astype(jnp.bfloat16), preferred_element_type=jnp.float32)


def _split_bf16(x):
    hi = x.astype(jnp.bfloat16).astype(jnp.float32)
    return hi, x - hi


def _rms_mm_kernel(x_ref, g_ref, w_ref, o_ref, xn_ref):
    @pl.when(pl.program_id(1) == 0)
    def _():
        xn_ref[...] = _rms(x_ref[...], g_ref[...]).astype(xn_ref.dtype)

    o_ref[...] = _mxu_dot(xn_ref[...], w_ref[...])


def rms_mm(x, gain, w, tn, col0=0, n_cols=None):
    T, K = x.shape
    N = w.shape[1] - col0 if n_cols is None else n_cols
    assert col0 % tn == 0 and N % tn == 0
    tm = _tile(T, 1088)
    cb0 = col0 // tn
    return pl.pallas_call(
        _rms_mm_kernel,
        grid=(T // tm, N // tn),
        in_specs=[pl.BlockSpec((tm, K), lambda i, j: (i, 0)),
                  pl.BlockSpec((1, K), lambda i, j: (0, 0)),
                  pl.BlockSpec((K, tn), lambda i, j: (0, cb0 + j))],
        out_specs=pl.BlockSpec((tm, tn), lambda i, j: (i, j)),
        out_shape=jax.ShapeDtypeStruct((T, N), jnp.float32),
        scratch_shapes=[pltpu.VMEM((tm, K), jnp.bfloat16)],
        compiler_params=_cparams("parallel", "arbitrary"),
        name="rms_mm",
    )(x, gain.reshape(1, K), w)


def _mm_res_kernel(a_ref, w_ref, r_ref, o_ref):
    o_ref[...] = r_ref[...] + _mxu_dot(a_ref[...], w_ref[...])


def mm_res(a, w, res):
    T, K = a.shape
    N = w.shape[1]
    tm = _tile(T, 544, 2 * SUBLANES)
    return pl.pallas_call(
        _mm_res_kernel,
        grid=(T // tm,),
        in_specs=[pl.BlockSpec((tm, K), lambda i: (i, 0)),
                  pl.BlockSpec((K, N), lambda i: (0, 0), pipeline_mode=pl.Buffered(1)),
                  pl.BlockSpec((tm, N), lambda i: (i, 0))],
        out_specs=pl.BlockSpec((tm, N), lambda i: (i, 0)),
        out_shape=jax.ShapeDtypeStruct((T, N), jnp.float32),
        compiler_params=_cparams("parallel"),
        name="mm_res",
    )(a, w, res)


def _rms_only_kernel(x_ref, g_ref, o_ref):
    o_ref[...] = _rms(x_ref[...], g_ref[...])


def rms_only(x, gain):
    T, K = x.shape
    tm = _tile(T, 544)
    return pl.pallas_call(
        _rms_only_kernel,
        grid=(T // tm,),
        in_specs=[pl.BlockSpec((tm, K), lambda i: (i, 0)),
                  pl.BlockSpec((1, K), lambda i: (0, 0))],
        out_specs=pl.BlockSpec((tm, K), lambda i: (i, 0)),
        out_shape=jax.ShapeDtypeStruct((T, K), jnp.float32),
        compiler_params=_cparams("parallel"),
        name="rms_final",
    )(x, gain.reshape(1, K))


def _ffn_kernel(x_ref, g_ref, w1_ref, w3_ref, w2_ref, o_ref, xn_ref):
    j = pl.program_id(1)

    @pl.when(j == 0)
    def _():
        xn_ref[...] = _rms(x_ref[...], g_ref[...]).astype(xn_ref.dtype)
        o_ref[...] = jnp.zeros_like(o_ref)

    xn = xn_ref[...]
    h1 = _mxu_dot(xn, w1_ref[...])
    h3 = _mxu_dot(xn, w3_ref[...])
    o_ref[...] += _mxu_dot(_silu(h1) * h3, w2_ref[...])

    @pl.when(j == pl.num_programs(1) - 1)
    def _():
        o_ref[...] = x_ref[...] + o_ref[...]


def ffn(x, gain, w1, w3, w2, tf=256):
    T, D = x.shape
    F = w1.shape[1]
    tm = _tile(T, 1088)
    one = pl.Buffered(1)
    return pl.pallas_call(
        _ffn_kernel,
        grid=(T // tm, F // tf),
        in_specs=[pl.BlockSpec((tm, D), lambda i, j: (i, 0), pipeline_mode=one),
                  pl.BlockSpec((1, D), lambda i, j: (0, 0)),
                  pl.BlockSpec((D, tf), lambda i, j: (0, j)),
                  pl.BlockSpec((D, tf), lambda i, j: (0, j)),
                  pl.BlockSpec((tf, D), lambda i, j: (j, 0))],
        out_specs=pl.BlockSpec((tm, D), lambda i, j: (i, 0), pipeline_mode=one),
        out_shape=jax.ShapeDtypeStruct((T, D), jnp.float32),
        scratch_shapes=[pltpu.VMEM((tm, D), jnp.bfloat16)],
        compiler_params=_cparams("parallel", "arbitrary"),
        name="ffn",
    )(x, gain.reshape(1, D), w1, w3, w2)


def _router_kernel(x_ref, g_ref, rt_ref, i1_ref, i2_ref, g1_ref, g2_ref, *, n_exp):
    xn = _rms(x_ref[...], g_ref[...])
    logit = [jnp.sum(xn * rt_ref[e:e + 1, :], axis=-1, keepdims=True) for e in range(n_exp)]
    neg = jnp.float32(-jnp.inf)

    def top(vals):
        m = vals[0]
        for v in vals[1:]:
            m = jnp.maximum(m, v)
        idx = jnp.full(m.shape, n_exp, jnp.int32)
        for e in reversed(range(n_exp)):
            idx = jnp.where(vals[e] == m, e, idx)
        return m, idx

    m1, i1 = top(logit)
    rest = [jnp.where(i1 == e, neg, logit[e]) for e in range(n_exp)]
    m2, i2 = top(rest)
    d = jnp.exp(m2 - m1)
    i1_ref[...] = i1
    i2_ref[...] = i2
    g1_ref[...] = 1.0 / (1.0 + d)
    g2_ref[...] = d / (1.0 + d)


def router_top2(x, gain, router):
    T, D = x.shape
    E = router.shape[1]
    tm = _tile(T, 544)
    col = pl.BlockSpec((tm, 1), lambda i: (i, 0))
    return pl.pallas_call(
        functools.partial(_router_kernel, n_exp=E),
        grid=(T // tm,),
        in_specs=[pl.BlockSpec((tm, D), lambda i: (i, 0)),
                  pl.BlockSpec((1, D), lambda i: (0, 0)),
                  pl.BlockSpec((E, D), lambda i: (0, 0))],
        out_specs=[col, col, col, col],
        out_shape=[jax.ShapeDtypeStruct((T, 1), jnp.int32)] * 2 + [jax.ShapeDtypeStruct((T, 1), jnp.float32)] * 2,
        compiler_params=_cparams("parallel"),
        name="router",
    )(x, gain.reshape(1, D), router.T)


MOE_SUB = 256
MOE_UNROLL = 8


def _moe_plan(i1, i2, n_exp, cap):
    T = i1.shape[0]
    n_tiles = (2 * T) // cap + n_exp
    e_a = jnp.concatenate([i1[:, 0], i2[:, 0]])
    tok = jnp.concatenate([jnp.arange(T, dtype=jnp.int32)] * 2)
    onehot = (e_a[:, None] == jnp.arange(n_exp, dtype=jnp.int32)[None, :]).astype(jnp.int32)
    csum = jnp.cumsum(onehot, axis=0)
    rank = jnp.sum((csum - onehot) * onehot, axis=1)
    counts = csum[-1]
    nt_e = (counts + cap - 1) // cap
    t_end = jnp.cumsum(nt_e)
    t_start = t_end - nt_e
    pos = (t_start[e_a] * cap + rank).astype(jnp.int32)
    n_used = t_end[-1]
    tiles = jnp.arange(n_tiles, dtype=jnp.int32)
    t_buf = jnp.minimum(tiles, n_used - 1).astype(jnp.int32)
    t_exp = jnp.minimum(jnp.sum((t_buf[:, None] >= t_end[None, :]).astype(jnp.int32), axis=1), n_exp - 1)
    t_rows = jnp.where(tiles < n_used, jnp.clip(counts[t_exp] - (tiles - t_start[t_exp]) * cap, 0, cap), 0)
    src = jnp.zeros((n_tiles * cap,), jnp.int32).at[pos].set(tok)
    return t_exp.astype(jnp.int32), t_rows.astype(jnp.int32), t_buf, src, pos[:T], pos[T:]


DMA_UNROLL = 8


def _row_copy(src_hbm, row, dst, r, sem):
    return pltpu.make_async_copy(src_hbm.at[pl.ds(row, 1)], dst.at[pl.ds(r, 1)], sem)


def _wait_rows(dst, sem):
    pltpu.make_async_copy(dst, dst, sem).wait()


def _gather_norm_kernel(src_ref, rows_ref, x_hbm, g_ref, o_ref, buf, sem, *, sub_per_tile):
    i = pl.program_id(0)
    n = pl.num_programs(0)
    tg = o_ref.shape[0]

    def active(step):
        return (step % sub_per_tile) * tg < rows_ref[step // sub_per_tile]

    def issue(step):
        slot = step % 2

        def one(r, c):
            _row_copy(x_hbm, src_ref[step * tg + r], buf.at[slot], r, sem.at[slot]).start()
            return c

        lax.fori_loop(0, tg, one, 0, unroll=DMA_UNROLL)

    @pl.when((i == 0) & active(0))
    def _():
        issue(0)

    nxt = jnp.minimum(i + 1, n - 1)

    @pl.when((i + 1 < n) & active(nxt))
    def _():
        issue(nxt)

    @pl.when(active(i))
    def _():
        slot = i % 2
        _wait_rows(buf.at[slot], sem.at[slot])
        o_ref[...] = _rms(buf[slot], g_ref[...]).astype(o_ref.dtype)

    @pl.when(jnp.logical_not(active(i)))
    def _():
        o_ref[...] = jnp.zeros_like(o_ref)


def gather_norm(x, gain, src, t_rows, cap):
    T, D = x.shape
    n_rows = src.shape[0]
    tg = MOE_SUB
    return pl.pallas_call(
        functools.partial(_gather_norm_kernel, sub_per_tile=cap // tg),
        grid_spec=pltpu.PrefetchScalarGridSpec(
            num_scalar_prefetch=2,
            grid=(n_rows // tg,),
            in_specs=[pl.BlockSpec(memory_space=pl.ANY),
                      pl.BlockSpec((1, D), lambda i, s, r: (0, 0))],
            out_specs=pl.BlockSpec((tg, D), lambda i, s, r: (i, 0)),
            scratch_shapes=[pltpu.VMEM((2, tg, D), jnp.float32), pltpu.SemaphoreType.DMA((2,))]),
        out_shape=jax.ShapeDtypeStruct((n_rows, D), jnp.bfloat16),
        compiler_params=_cparams("arbitrary"),
        name="moe_gather",
    )(src, t_rows, x, gain.reshape(1, D))


def _moe_kernel(te_ref, tr_ref, tb_ref, xg_ref, w1_ref, w3_ref, w2_ref, o_ref):
    k = pl.program_id(0)
    j = pl.program_id(1)
    rows = tr_ref[k]

    @pl.when(j == 0)
    def _():
        o_ref[...] = jnp.zeros_like(o_ref)

    @pl.when(rows > 0)
    def _():
        w1 = w1_ref[...].astype(jnp.bfloat16)
        w3 = w3_ref[...].astype(jnp.bfloat16)
        w2 = w2_ref[...].astype(jnp.bfloat16)

        def up(s):
            xs = xg_ref[pl.ds(pl.multiple_of(s * MOE_SUB, MOE_SUB), MOE_SUB), :]
            h1 = jnp.dot(xs, w1, preferred_element_type=jnp.float32)
            h3 = jnp.dot(xs, w3, preferred_element_type=jnp.float32)
            return (_silu(h1) * h3).astype(jnp.bfloat16)

        def down(s, h):
            r0 = pl.multiple_of(s * MOE_SUB, MOE_SUB)
            o_ref[pl.ds(r0, MOE_SUB), :] += jnp.dot(h, w2, preferred_element_type=jnp.float32)

        def body(s, h):
            h_next = up(s + 1)
            down(s, h)
            return h_next

        def body_n(t, h):
            for i in range(MOE_UNROLL):
                h = body(MOE_UNROLL * t + i, h)
            return h

        last = (rows + MOE_SUB - 1) // MOE_SUB - 1
        h = lax.fori_loop(0, last // MOE_UNROLL, body_n, up(0))
        h = lax.fori_loop(MOE_UNROLL * (last // MOE_UNROLL), last, body, h)
        down(last, h)


def moe_experts(xg, t_exp, t_rows, t_buf, w1, w3, w2, cap, tf=256):
    n_rows, D = xg.shape
    E, _, F = w1.shape
    n_tiles = n_rows // cap
    nj = F // tf

    def jj(k, j, tr):
        return jnp.where(tr[k] > 0, j, nj - 1)

    one = pl.Buffered(1)
    return pl.pallas_call(
        _moe_kernel,
        grid_spec=pltpu.PrefetchScalarGridSpec(
            num_scalar_prefetch=3,
            grid=(n_tiles, nj),
            in_specs=[pl.BlockSpec((cap, D), lambda k, j, te, tr, tb: (tb[k], 0), pipeline_mode=one),
                      pl.BlockSpec((None, D, tf), lambda k, j, te, tr, tb: (te[k], 0, jj(k, j, tr))),
                      pl.BlockSpec((None, D, tf), lambda k, j, te, tr, tb: (te[k], 0, jj(k, j, tr))),
                      pl.BlockSpec((None, tf, D), lambda k, j, te, tr, tb: (te[k], jj(k, j, tr), 0))],
            out_specs=pl.BlockSpec((cap, D), lambda k, j, te, tr, tb: (k, 0), pipeline_mode=one)),
        out_shape=jax.ShapeDtypeStruct((n_rows, D), jnp.float32),
        compiler_params=_cparams("arbitrary", "arbitrary"),
        name="moe_experts",
    )(t_exp, t_rows, t_buf, xg, w1, w3, w2)


def _combine_kernel(p1_ref, p2_ref, x_ref, g1_ref, g2_ref, *rest, final_norm, n_first_tiles):
    if final_norm:
        gain_ref, y_hbm, oa_ref, ob_ref, a_buf, b_buf, sem = rest
    else:
        y_hbm, oa_ref, ob_ref, a_buf, b_buf, sem = rest
    i = pl.program_id(0)
    tc = x_ref.shape[0]

    def issue(step):
        slot = step % 2

        def one(r, c):
            _row_copy(y_hbm, p1_ref[step * tc + r], a_buf.at[slot], r, sem.at[0, slot]).start()
            _row_copy(y_hbm, p2_ref[step * tc + r], b_buf.at[slot], r, sem.at[1, slot]).start()
            return c

        lax.fori_loop(0, tc, one, 0, unroll=DMA_UNROLL)

    @pl.when(i == 0)
    def _():
        issue(0)

    @pl.when(i + 1 < pl.num_programs(0))
    def _():
        issue(i + 1)

    slot = i % 2
    _wait_rows(a_buf.at[slot], sem.at[0, slot])
    _wait_rows(b_buf.at[slot], sem.at[1, slot])
    y = x_ref[...] + (g1_ref[...] * a_buf[slot] + g2_ref[...] * b_buf[slot])
    y = _rms(y, gain_ref[...]) if final_norm else y

    @pl.when(i < n_first_tiles)
    def _():
        oa_ref[...] = y

    @pl.when(i >= n_first_tiles)
    def _():
        ob_ref[...] = y


def moe_combine(x, yg, pos1, pos2, g1, g2, n_first, final_gain=None):
    T, D = x.shape
    tc = _tile(math.gcd(n_first, T - n_first), 256)
    nf = n_first // tc
    final_norm = final_gain is not None
    tok = pl.BlockSpec((tc, D), lambda i, a, b: (i, 0))
    col = pl.BlockSpec((tc, 1), lambda i, a, b: (i, 0))
    in_specs = [tok, col, col] + ([pl.BlockSpec((1, D), lambda i, a, b: (0, 0))] if final_norm else [])
    in_specs.append(pl.BlockSpec(memory_space=pl.ANY))
    args = (x, g1, g2) + ((final_gain.reshape(1, D),) if final_norm else ()) + (yg,)
    return pl.pallas_call(
        functools.partial(_combine_kernel, final_norm=final_norm, n_first_tiles=nf),
        grid_spec=pltpu.PrefetchScalarGridSpec(
            num_scalar_prefetch=2,
            grid=(T // tc,),
            in_specs=in_specs,
            out_specs=[pl.BlockSpec((tc, D), lambda i, a, b: (jnp.minimum(i, nf - 1), 0)),
                       pl.BlockSpec((tc, D), lambda i, a, b: (jnp.maximum(i - nf, 0), 0))],
            scratch_shapes=[pltpu.VMEM((2, tc, D), jnp.float32), pltpu.VMEM((2, tc, D), jnp.float32),
                            pltpu.SemaphoreType.DMA((2, 2))]),
        out_shape=[jax.ShapeDtypeStruct((n_first, D), jnp.float32),
                   jax.ShapeDtypeStruct((T - n_first, D), jnp.float32)],
        compiler_params=_cparams("arbitrary"),
        name="moe_combine",
    )(pos1, pos2, *args)


def moe_top2(x, gain, router, w1, w3, w2, n_first, final_gain=None):
    T, D = x.shape
    E = router.shape[1]
    i1, i2, g1, g2 = router_top2(x, gain, router)
    cap = -(-(2 * T * 21 // (20 * E)) // MOE_SUB) * MOE_SUB
    t_exp, t_rows, t_buf, src, pos1, pos2 = _moe_plan(i1, i2, E, cap)
    xg = gather_norm(x, gain, src, t_rows, cap)
    yg = moe_experts(xg, t_exp, t_rows, t_buf, w1, w3, w2, cap)
    return moe_combine(x, yg, pos1, pos2, g1, g2, n_first, final_gain)


def _cumsum_rows(x):
    n = x.shape[0]
    row = lax.broadcasted_iota(jnp.int32, x.shape, 0)
    s = 1
    while s < n:
        x = x + jnp.where(row >= s, pltpu.roll(x, s, 0), 0.0)
        s *= 2
    return x


def _hgrn_kernel(q_ref, f_ref, i_ref, lbl_ref, s0_ref, o_ref, so_ref, s_scr, *, chunk, layer_j, valid_len):
    c = pl.program_id(1)
    tl = q_ref.shape[0]
    n_head, dk, _ = s_scr.shape

    @pl.when(c == 0)
    def _():
        s_scr[...] = s0_ref[...]

    lg = lbl_ref[...]
    ex = jnp.exp(lg - jnp.max(lg, axis=0, keepdims=True))
    lb_all = jnp.sum(ex[0:layer_j + 1, :], axis=0, keepdims=True) / jnp.sum(ex, axis=0, keepdims=True)

    pad = LANES - chunk
    zpad = jnp.zeros((pad, dk), jnp.float32)
    row = lax.broadcasted_iota(jnp.int32, (chunk, LANES), 0)
    col = lax.broadcasted_iota(jnp.int32, (chunk, LANES), 1)
    causal = col <= row
    ones_cd = jnp.ones((chunk, dk), jnp.float32)
    tn_dims = (((0,), (0,)), ((), ()))
    nt_dims = (((1,), (1,)), ((), ()))

    for cc in range(tl // chunk):
        sl = slice(cc * chunk, (cc + 1) * chunk)
        for h in range(n_head):
            hs = slice(h * dk, (h + 1) * dk)
            lb = lb_all[:, hs]
            f = lb + (1.0 - lb) * jax.nn.sigmoid(f_ref[sl, hs])
            q = _silu(q_ref[sl, hs])
            v = i_ref[sl, hs]
            if valid_len is not None:
                t_idx = c * tl + cc * chunk + lax.broadcasted_iota(jnp.int32, (chunk, dk), 0)
                ok = t_idx < valid_len
                f = jnp.where(ok, f, 1.0)
                q = jnp.where(ok, q, 0.0)
            g = jnp.log(f)
            k = 1.0 - f
            b = _cumsum_rows(g)
            mid = chunk // 2 - 1
            bm = b[mid:mid + 1, :]
            bl = b[chunk - 1:chunk, :]
            qe = q * jnp.exp(b - bm)
            ke = jnp.concatenate([k * jnp.exp(bm - b), zpad], axis=0)
            vp = jnp.concatenate([v, zpad], axis=0)
            a = lax.dot_general(qe, ke, nt_dims, preferred_element_type=jnp.float32)
            a = jnp.where(causal, a, 0.0)
            s = s_scr[h]
            o = jnp.dot(a, vp, preferred_element_type=jnp.float32)
            o = o + jnp.dot(q * jnp.exp(b), s, preferred_element_type=jnp.float32)
            o_ref[sl, hs] = o
            kd = jnp.concatenate([k * jnp.exp(bl - b), zpad], axis=0)
            ghi, glo = _split_bf16(g)
            blc = (lax.dot_general(ghi, ones_cd, tn_dims, preferred_element_type=jnp.float32)
                   + lax.dot_general(glo, ones_cd, tn_dims, preferred_element_type=jnp.float32))
            s_scr[h] = jnp.exp(blc) * s + lax.dot_general(kd, vp, tn_dims, preferred_element_type=jnp.float32)

    @pl.when(c == pl.num_programs(1) - 1)
    def _():
        so_ref[...] = s_scr[...]


def hgrn_scan(pa, B, L, tl, chunk, lb_logits, s0, layer_j, valid_len=None):
    H, dk = s0.shape[1], s0.shape[2]
    d_a = H * dk
    nt = L // tl
    kern = functools.partial(_hgrn_kernel, chunk=chunk, layer_j=layer_j, valid_len=valid_len)

    def col(off):
        return pl.BlockSpec((tl, d_a), lambda b, c: (b * nt + c, off))

    st = pl.BlockSpec((None, H, dk, dk), lambda b, c: (b, 0, 0, 0))
    return pl.pallas_call(
        kern,
        grid=(B, nt),
        in_specs=[col(0), col(1), col(2), pl.BlockSpec(lb_logits.shape, lambda b, c: (0, 0)), st],
        out_specs=[pl.BlockSpec((tl, d_a), lambda b, c: (b * nt + c, 0)), st],
        out_shape=[jax.ShapeDtypeStruct((B * L, d_a), jnp.float32),
                   jax.ShapeDtypeStruct(s0.shape, jnp.float32)],
        scratch_shapes=[pltpu.VMEM((H, dk, dk), jnp.float32)],
        compiler_params=_cparams("parallel", "arbitrary"),
        name="hgrn_scan",
    )(pa, pa, pa, lb_logits, s0)


def _seg_sum(x, bones):
    outs = []
    for j in range(x.shape[1] // LANES):
        hi, lo = _split_bf16(x[:, j * LANES:(j + 1) * LANES])
        outs.append(jnp.dot(hi, bones, preferred_element_type=jnp.float32)
                    + jnp.dot(lo, bones, preferred_element_type=jnp.float32))
    return jnp.concatenate(outs, axis=1)


def _rwkv_prep_kernel(u_ref, halo_ref, e_ref, pos_ref, mu_ref, w0_ref, a0_ref, kk_ref, ka_ref, rk_ref,
                      w2_ref, a2_ref, g2_ref, bones_ref,
                      kk_o, wr_o, w_o, kka_o, km_o, vhi_o, vlo_o, g_o, c1_o, c2_o, bon_o,
                      *, d_b, first_state_tile):
    u = u_ref[...]
    has_state = pl.program_id(0) >= first_state_tile
    u_prev = _prev_in_seq(u, halo_ref[...], 1, pos_ref[...], jnp.where(has_state, e_ref[...], 0.0))
    us = u + (u_prev - u) * mu_ref[...]
    r = us[:, 0:d_b]
    k = us[:, d_b:2 * d_b]
    v = us[:, 2 * d_b:3 * d_b]
    wa = us[:, 3 * d_b:3 * d_b + LANES]
    gd = us[:, 3 * d_b + LANES:3 * d_b + 2 * LANES]
    bones = bones_ref[...]
    z = w0_ref[...] + jnp.dot(jnp.tanh(wa), w2_ref[...], preferred_element_type=jnp.float32)
    nz = -z
    softplus = jnp.maximum(nz, 0.0) + jnp.log(1.0 + jnp.exp(-jnp.abs(nz)))
    w_log = -softplus - 0.5
    decay = jnp.exp(-jnp.exp(w_log))
    a = jax.nn.sigmoid(a0_ref[...] + jnp.dot(wa, a2_ref[...], preferred_element_type=jnp.float32))
    g = jnp.dot(jax.nn.sigmoid(gd), g2_ref[...], preferred_element_type=jnp.float32)
    kk = k * kk_ref[...]
    kk = kk / jnp.maximum(jnp.sqrt(_seg_sum(kk * kk, bones)), 1e-12)
    kmod = k * (1.0 + (a - 1.0) * ka_ref[...])
    kka = kk * a
    vhi, vlo = _split_bf16(v)
    kk_o[...] = kk
    wr_o[...] = decay * r
    w_o[...] = decay
    kka_o[...] = kka
    km_o[...] = kmod
    vhi_o[...] = vhi
    vlo_o[...] = vlo
    g_o[...] = g
    c1_o[...] = _seg_sum(kka * r, bones)
    c2_o[...] = _seg_sum(kmod * r, bones)
    bon_o[...] = _seg_sum(r * kmod * rk_ref[...], bones)


def rwkv_prep(u, pos, e1, tm, n_zero_rows, mu, w0, a0, k_k, k_a, r_k, w2p, a2p, g2, bones, d_b):
    T, DU = u.shape
    row = lambda n: pl.BlockSpec((1, n), lambda i: (0, 0))
    full = lambda a: pl.BlockSpec(a.shape, lambda i: (0, 0))
    first, specs = _seq_tiles(tm, n_zero_rows)
    tok_u, halo_u, start_u = specs(DU)
    tok = pl.BlockSpec((tm, d_b), lambda i: (i, 0))
    return pl.pallas_call(
        functools.partial(_rwkv_prep_kernel, d_b=d_b, first_state_tile=first),
        grid=(T // tm,),
        in_specs=[tok_u, halo_u, start_u, pl.BlockSpec((tm, 1), lambda i: (i, 0)),
                  row(DU), row(d_b), row(d_b), row(d_b), row(d_b), row(d_b),
                  full(w2p), full(a2p), full(g2), full(bones)],
        out_specs=[tok] * 11,
        out_shape=[jax.ShapeDtypeStruct((T, d_b), jnp.float32)] * 11,
        compiler_params=_cparams("parallel"),
        name="rwkv_prep",
    )(u, u, e1, pos, mu.reshape(1, DU), w0.reshape(1, d_b), a0.reshape(1, d_b), k_k.reshape(1, d_b),
      k_a.reshape(1, d_b), r_k.reshape(1, d_b), w2p, a2p, g2, bones)


SCAN_GROUP = 64


def _rwkv_scan_kernel(kk_ref, wr_ref, w_ref, kka_ref, k_ref, vhi_ref, vlo_ref, c1_ref, c2_ref,
                      s0_ref, rhs_ref, md_ref, y_ref, so_ref, s_scr, r2_o, sa_o, *, n_pair):
    tb = kk_ref.shape[-2]
    n = HEAD_B

    @pl.when(pl.program_id(1) == 0)
    def _():
        s_scr[...] = s0_ref[...]

    r2_o[...] = jnp.zeros_like(r2_o)
    sa_o[...] = jnp.zeros_like(sa_o)
    rhs = rhs_ref[...]
    md = md_ref[...]
    lane = lax.broadcasted_iota(jnp.int32, (n, LANES), 1) & (n - 1)

    def token_step(row, sel):
        parts = []
        for p in range(n_pair):
            sl = slice(p * LANES, (p + 1) * LANES)
            s = s_scr[p]
            p1 = s * row(kk_ref, sl)
            p2 = s * row(wr_ref, sl)
            dh = md * row(vhi_ref, sl)
            dl = md * row(vlo_ref, sl)
            parts.append(jnp.concatenate([p1, dh], axis=1))
            parts.append(jnp.concatenate([p2, dl], axis=1))
        out = jnp.dot(jnp.concatenate(parts, axis=0), rhs, preferred_element_type=jnp.float32)
        for p in range(n_pair):
            sl = slice(p * LANES, (p + 1) * LANES)
            top = out[2 * p * n:(2 * p + 1) * n]
            bot = out[(2 * p + 1) * n:(2 * p + 2) * n]
            sa_b = top[:, 0:LANES]
            r2_b = bot[:, 0:LANES]
            v_b = top[:, LANES:2 * LANES] + bot[:, LANES:2 * LANES]
            s = s_scr[p]
            s_scr[p] = s * row(w_ref, sl) - sa_b * row(kka_ref, sl) + v_b * row(k_ref, sl)
            r2_o[p] = jnp.where(sel, r2_b, r2_o[p])
            sa_o[p] = jnp.where(sel, sa_b, sa_o[p])

    if tb % SCAN_GROUP == 0:
        def group(gi, carry):
            for sub in range(SCAN_GROUP // SUBLANES):
                t0 = gi * SCAN_GROUP + sub * SUBLANES
                base = pl.multiple_of(t0, SUBLANES)
                for jj in range(SUBLANES):
                    token_step(lambda ref, sl, jj=jj, base=base: ref[pl.ds(base, SUBLANES), sl][jj:jj + 1, :],
                               lane == t0 + jj)
            return carry

        lax.fori_loop(0, tb // SCAN_GROUP, group, 0)
    else:
        for tt in range(tb):
            token_step(lambda ref, sl, tt=tt: ref[tt:tt + 1, sl], lane == tt)

    for p in range(n_pair):
        sl = slice(p * LANES, (p + 1) * LANES)
        zt = jnp.concatenate([r2_o[p], sa_o[p]], axis=0).T
        r2 = jnp.concatenate([zt[0:tb, 0:n], zt[n:n + tb, 0:n]], axis=1)
        sa = jnp.concatenate([zt[0:tb, n:2 * n], zt[n:n + tb, n:2 * n]], axis=1)
        v = vhi_ref[:, sl] + vlo_ref[:, sl]
        y_ref[:, sl] = r2 - sa * c1_ref[:, sl] + v * c2_ref[:, sl]

    @pl.when(pl.program_id(1) == pl.num_programs(1) - 1)
    def _():
        so_ref[...] = s_scr[...]


def rwkv_scan(vecs, tok, y_spec, y_shape, B, nt, s0_pair, rhs, md):
    n_pair = s0_pair.shape[1]
    st = pl.BlockSpec((None, n_pair, HEAD_B, LANES), lambda b, t: (b, 0, 0, 0))
    acc = pltpu.VMEM((n_pair, HEAD_B, LANES), jnp.float32)
    return pl.pallas_call(
        functools.partial(_rwkv_scan_kernel, n_pair=n_pair),
        grid=(B, nt),
        in_specs=[tok] * 9 + [st, pl.BlockSpec(rhs.shape, lambda b, t: (0, 0)),
                              pl.BlockSpec(md.shape, lambda b, t: (0, 0))],
        out_specs=[y_spec, st],
        out_shape=[y_shape, jax.ShapeDtypeStruct(s0_pair.shape, jnp.float32)],
        scratch_shapes=[acc, acc, acc],
        compiler_params=_cparams("parallel", "arbitrary"),
        name="rwkv_scan",
    )(*vecs, s0_pair, rhs, md)


def _mix_out_kernel(oa1_ref, oa2_ref, go_ref, hg_ref, y1_ref, y2_ref, vhi_ref, vlo_ref, g_ref, bon_ref,
                    lnw_ref, lnb_ref, bones_ref, o_ref, *, d_a, n_first_tiles):
    first = pl.program_id(0) < n_first_tiles
    oa = jnp.where(first, oa1_ref[...], oa2_ref[...])
    outs = []
    for h in range(d_a // DK_A):
        x = oa[:, h * DK_A:(h + 1) * DK_A]
        outs.append(x * lax.rsqrt(jnp.mean(x * x, axis=-1, keepdims=True) + NORM_EPS))
    o_a = jnp.concatenate(outs, axis=1) * hg_ref[...] * _silu(go_ref[...])
    bones = bones_ref[...]
    v = vhi_ref[...] + vlo_ref[...]
    y = jnp.where(first, y1_ref[...], y2_ref[...])
    inv_n = 1.0 / HEAD_B
    mean = _seg_sum(y, bones) * inv_n
    d = y - mean
    var = _seg_sum(d * d, bones) * inv_n
    yn = d * lax.rsqrt(var + RWKV_GN_EPS) * lnw_ref[...] + lnb_ref[...]
    o_b = (yn + bon_ref[...] * v) * g_ref[...]
    o_ref[:, 0:d_a] = o_a.astype(o_ref.dtype)
    o_ref[:, d_a:] = o_b.astype(o_ref.dtype)


def mix_out(oa_pair, pa, hg_row, y_pair, vhi, vlo, g, bon, ln_w, ln_b, bones):
    n_first, d_a = oa_pair[0].shape
    n_rest = oa_pair[1].shape[0]
    T = n_first + n_rest
    d_b = y_pair[0].shape[1]
    tm = _tile(math.gcd(n_first, n_rest), 256, 2 * SUBLANES)
    nf = n_first // tm
    tok = lambda n: pl.BlockSpec((tm, n), lambda i: (i, 0))
    row = lambda n: pl.BlockSpec((1, n), lambda i: (0, 0))
    lead = lambda n: pl.BlockSpec((tm, n), lambda i: (jnp.minimum(i, nf - 1), 0))
    rest = lambda n: pl.BlockSpec((tm, n), lambda i: (jnp.maximum(i - nf, 0), 0))
    return pl.pallas_call(
        functools.partial(_mix_out_kernel, d_a=d_a, n_first_tiles=nf),
        grid=(T // tm,),
        in_specs=[lead(d_a), rest(d_a), pl.BlockSpec((tm, d_a), lambda i: (i, 3)), row(d_a),
                  lead(d_b), rest(d_b)]
                 + [tok(d_b)] * 4 + [row(d_b), row(d_b), pl.BlockSpec(bones.shape, lambda i: (0, 0))],
        out_specs=tok(d_a + d_b),
        out_shape=jax.ShapeDtypeStruct((T, d_a + d_b), jnp.bfloat16),
        compiler_params=_cparams("parallel"),
        name="mix_out",
    )(*oa_pair, pa, hg_row, *y_pair, vhi, vlo, g, bon, ln_w.reshape(1, d_b), ln_b.reshape(1, d_b), bones)


def _conv_in_kernel(x_ref, g_ref, wb_ref, wc_ref, wh_ref, gb_ref, u_ref, xn_ref):
    @pl.when(pl.program_id(1) == 0)
    def _():
        xn_ref[...] = _rms(x_ref[...], g_ref[...]).astype(xn_ref.dtype)

    xn = xn_ref[...]
    gb_ref[...] = _mxu_dot(xn, wb_ref[...])
    u_ref[...] = _mxu_dot(xn, wc_ref[...]) * _mxu_dot(xn, wh_ref[...])


def conv_in(x, gain, w, tn=256):
    T, K = x.shape
    d = w.shape[1] // 3
    tm = _tile(T, 1088)
    nb = d // tn
    out = pl.BlockSpec((tm, tn), lambda i, j: (i, j))
    return pl.pallas_call(
        _conv_in_kernel,
        grid=(T // tm, nb),
        in_specs=[pl.BlockSpec((tm, K), lambda i, j: (i, 0)),
                  pl.BlockSpec((1, K), lambda i, j: (0, 0)),
                  pl.BlockSpec((K, tn), lambda i, j: (0, j)),
                  pl.BlockSpec((K, tn), lambda i, j: (0, nb + j)),
                  pl.BlockSpec((K, tn), lambda i, j: (0, 2 * nb + j))],
        out_specs=[out, out],
        out_shape=[jax.ShapeDtypeStruct((T, d), jnp.float32)] * 2,
        scratch_shapes=[pltpu.VMEM((tm, K), jnp.bfloat16)],
        compiler_params=_cparams("parallel", "arbitrary"),
        name="conv_in",
    )(x, gain.reshape(1, K), w, w, w)


def _prev_rows(tile, halo, k):
    rolled = pltpu.roll(tile, k, 0)
    row = lax.broadcasted_iota(jnp.int32, halo.shape, 0)
    head = jnp.where(row < k, pltpu.roll(halo, k, 0), rolled[0:SUBLANES])
    return jnp.concatenate([head, rolled[SUBLANES:]], axis=0)


def _prev_in_seq(tile, halo, k, pos, start_rows):
    return jnp.where(pos < k, start_rows, _prev_rows(tile, halo, k))


def _conv_mm_kernel(gb_ref, u_ref, halo_ref, pos_ref, e1_ref, e2_ref, ck_ref, w_ref, r_ref, o_ref,
                    *, first_state_tile):
    u = u_ref[...]
    pos = pos_ref[...]
    has_state = pl.program_id(0) >= first_state_tile
    u1 = _prev_in_seq(u, halo_ref[...], 1, pos, jnp.where(has_state, e1_ref[...], 0.0))
    u2 = _prev_in_seq(u, halo_ref[...], 2, pos, jnp.where(has_state, e2_ref[...], 0.0))
    y = ck_ref[0:1, :] * u2
    y = y + ck_ref[1:2, :] * u1
    y = y + ck_ref[2:3, :] * u
    o_ref[...] = r_ref[...] + _mxu_dot(gb_ref[...] * y, w_ref[...])


def _seq_tiles(tm, n_zero_rows):
    first = n_zero_rows // tm
    hb = tm // SUBLANES

    def specs(n):
        return (pl.BlockSpec((tm, n), lambda i, *_: (i, 0)),
                pl.BlockSpec((SUBLANES, n), lambda i, *_: (jnp.maximum(i * hb - 1, 0), 0)),
                pl.BlockSpec((tm, n), lambda i, *_: (jnp.maximum(i - first, 0), 0)))

    return first, specs


def conv_mm(gb, u, pos, e1, e2, conv_k, w, res, tm, n_zero_rows):
    T, d = u.shape
    N = w.shape[1]
    assert conv_k.shape[0] == 3
    first, specs = _seq_tiles(tm, n_zero_rows)
    tok, halo, start = specs(d)
    one = pl.Buffered(1)
    start = pl.BlockSpec(start.block_shape, start.index_map, pipeline_mode=one)
    return pl.pallas_call(
        functools.partial(_conv_mm_kernel, first_state_tile=first),
        grid=(T // tm,),
        in_specs=[tok, tok, halo, pl.BlockSpec((tm, 1), lambda i: (i, 0)), start, start,
                  pl.BlockSpec(conv_k.shape, lambda i: (0, 0)),
                  pl.BlockSpec((d, N), lambda i: (0, 0), pipeline_mode=one),
                  pl.BlockSpec((tm, N), lambda i: (i, 0))],
        out_specs=pl.BlockSpec((tm, N), lambda i: (i, 0)),
        out_shape=jax.ShapeDtypeStruct((T, N), jnp.float32),
        compiler_params=_cparams("parallel"),
        name="conv_mm",
    )(gb, u, u, pos, e1, e2, conv_k, w, res)


def _start_rows(state, L, k):
    B, ns, N = state.shape
    assert k <= min(L, ns)
    return jnp.zeros((B, L, N), state.dtype).at[:, :k].set(state[:, ns - k:]).reshape(B * L, N)


def _last_rows(x, groups, k):
    outs = []
    row = 0
    for B, L in groups:
        idx = [row + b * L + L - k + i for b in range(B) for i in range(k)]
        outs.append(jnp.take(x, jnp.asarray(idx, jnp.int32), axis=0).reshape(B, k, -1))
        row += B * L
    return outs


def _pair_state(s):
    B, H, n, _ = s.shape
    return s.reshape(B, H // 2, 2, n, n).transpose(0, 1, 3, 2, 4).reshape(B, H // 2, n, 2 * n)


def _unpair_state(s):
    B, hp, n, _ = s.shape
    return s.reshape(B, hp, n, 2, n).transpose(0, 1, 3, 2, 4).reshape(B, 2 * hp, n, n)


def kernel(x_prompt, x_sample, state_hgrn, state_rwkv, state_shift, state_conv, norm_mix, w_in_ab,
           hgrn_lb_logits, hgrn_norm, rwkv_mu, rwkv_w0, rwkv_w2, rwkv_a0, rwkv_a2, rwkv_g2, rwkv_k_k,
           rwkv_k_a, rwkv_r_k, rwkv_ln_w, rwkv_ln_b, w_out_ab, norm_ffn, ffn_w1, ffn_w3, ffn_w2,
           conv_w_in, conv_k, conv_w_out, moe_router, moe_w1, moe_w3, moe_w2, norm_final):
    f32 = jnp.float32
    Bp, Lp, D = x_prompt.shape
    Bs, Ls, _ = x_sample.shape
    Tp, Ts = Bp * Lp, Bs * Ls
    groups = ((Bp, Lp), (Bs, Ls))
    n_even = w_in_ab.shape[0]
    n_odd = conv_w_in.shape[0]
    depth = n_even + n_odd
    H_A, dk = state_hgrn.shape[2], state_hgrn.shape[3]
    d_a = H_A * dk
    H_B, hb = state_rwkv.shape[2], state_rwkv.shape[3]
    d_b = H_B * hb
    d_shift = state_shift.shape[-1]
    lora_w = rwkv_w2.shape[1]
    lora_a = rwkv_a2.shape[1]
    assert hb == HEAD_B and dk == DK_A and lora_w + lora_a == LANES and H_B % 2 == 0
    assert w_in_ab.shape[2] == 4 * d_a + d_shift

    seg = jnp.arange(LANES) // HEAD_B
    bones = (seg[:, None] == seg[None, :]).astype(f32)
    seg2 = jnp.arange(2 * LANES) // HEAD_B
    rhs_scan = (seg2[:, None] == seg2[None, :]).astype(f32)
    md = (jnp.arange(LANES)[None, :] % HEAD_B == jnp.arange(HEAD_B)[:, None]).astype(f32)

    x = jnp.concatenate([x_prompt.reshape(Tp, D), x_sample.reshape(Ts, D)], axis=0)
    pos = jnp.concatenate([jnp.tile(jnp.arange(Lp, dtype=jnp.int32), Bp),
                           jnp.tile(jnp.arange(Ls, dtype=jnp.int32), Bs)]).reshape(Tp + Ts, 1)
    seq_tile = math.gcd(Tp, Ts)
    assert seq_tile % SUBLANES == 0
    new_h, new_r, new_s, new_c = ([], []), ([], []), ([], []), ([], [])

    for layer in range(depth):
        j = layer // 2
        if layer % 2 == 0:
            w_in = w_in_ab[j]
            tn_u = _tile(d_shift, 512, LANES)
            assert (4 * d_a) % tn_u == 0
            pa = rms_mm(x, norm_mix[layer], w_in, tn=512, n_cols=4 * d_a)
            u = rms_mm(x, norm_mix[layer], w_in, tn=tn_u, col0=4 * d_a)
            zeros_h = jnp.zeros((Bp,) + state_hgrn.shape[2:], f32)
            tl = _tile(Lp, 128)
            oa_p, sh_p = hgrn_scan(pa, Bp, Lp, tl, min(32, tl), hgrn_lb_logits, zeros_h, j)
            lpad = -(-Ls // SUBLANES) * SUBLANES
            pa_s = jnp.pad(pa[Tp:, :3 * d_a].reshape(Bs, Ls, 3 * d_a), ((0, 0), (0, lpad - Ls), (0, 0)))
            oa_s, sh_s = hgrn_scan(pa_s.reshape(Bs * lpad, 3 * d_a), Bs, lpad, lpad, lpad, hgrn_lb_logits,
                                   state_hgrn[j], j, valid_len=Ls)
            oa = (oa_p, oa_s.reshape(Bs, lpad, d_a)[:, :Ls].reshape(Ts, d_a))
            new_h[0].append(sh_p)
            new_h[1].append(sh_s)
            e_shift = _start_rows(state_shift[j][:, None, :], Ls, 1)
            w2p = jnp.concatenate([rwkv_w2[j], jnp.zeros((lora_a, d_b), f32)], axis=0)
            a2p = jnp.concatenate([jnp.zeros((lora_w, d_b), f32), rwkv_a2[j]], axis=0)
            prep = rwkv_prep(u, pos, e_shift, _tile(seq_tile, 256), Tp, rwkv_mu[j], rwkv_w0[j], rwkv_a0[j],
                             rwkv_k_k[j], rwkv_k_a[j], rwkv_r_k[j].reshape(d_b), w2p, a2p, rwkv_g2[j],
                             bones, d_b)
            kk, wr, wdec, kka, kmod, vhi, vlo, gg, c1, c2, bon = prep
            scan_in = (kk, wr, wdec, kka, kmod, vhi, vlo, c1, c2)
            tb_p = _tile(Lp, HEAD_B)
            nt_p = Lp // tb_p
            zeros_r = jnp.zeros((Bp, H_B // 2, hb, 2 * hb), f32)
            tok_p = pl.BlockSpec((tb_p, d_b), lambda b, t: (b * nt_p + t, 0))
            y_p, sr_p = rwkv_scan(scan_in, tok_p, tok_p, jax.ShapeDtypeStruct((Tp, d_b), f32), Bp, nt_p,
                                  zeros_r, rhs_scan, md)
            tok_s = pl.BlockSpec((None, Ls, d_b), lambda b, t: (b, 0, 0))
            y_s, sr_s = rwkv_scan([a[Tp:].reshape(Bs, Ls, d_b) for a in scan_in], tok_s, tok_s,
                                  jax.ShapeDtypeStruct((Bs, Ls, d_b), f32), Bs, 1,
                                  _pair_state(state_rwkv[j]), rhs_scan, md)
            yb = (y_p, y_s.reshape(Ts, d_b))
            new_r[0].append(_unpair_state(sr_p))
            new_r[1].append(_unpair_state(sr_s))
            last_u = _last_rows(u, groups, 1)
            new_s[0].append(last_u[0][:, 0])
            new_s[1].append(last_u[1][:, 0])
            hg_row = jnp.tile(hgrn_norm[j], H_A).reshape(1, d_a)
            o = mix_out(oa, pa, hg_row, yb, vhi, vlo, gg, bon, rwkv_ln_w[j], rwkv_ln_b[j], bones)
            x = mm_res(o, w_out_ab[j], x)
            x = ffn(x, norm_ffn[layer], ffn_w1[j], ffn_w3[j], ffn_w2[j])
        else:
            d_c = state_conv.shape[-1]
            cw = state_conv.shape[2]
            gb, uc = conv_in(x, norm_mix[layer], conv_w_in[j])
            last_c = _last_rows(uc, groups, cw)
            new_c[0].append(last_c[0])
            new_c[1].append(last_c[1])
            x = conv_mm(gb, uc, pos, _start_rows(state_conv[j], Ls, 1), _start_rows(state_conv[j], Ls, 2),
                        conv_k[j], conv_w_out[j], x, _tile(seq_tile, 256), Tp)
            xp, xs = moe_top2(x, norm_ffn[layer], moe_router[j], moe_w1[j], moe_w3[j], moe_w2[j], Tp,
                              final_gain=norm_final if layer == depth - 1 else None)
            if layer < depth - 1:
                x = jnp.concatenate([xp, xs], axis=0)

    if depth % 2 == 1:
        y = rms_only(x, norm_final)
        xp, xs = y[:Tp], y[Tp:]
    outs = [xp.reshape(Bp, Lp, D), xs.reshape(Bs, Ls, D)]
    for g in (0, 1):
        outs += [jnp.stack(new_h[g]), jnp.stack(new_r[g]), jnp.stack(new_s[g]), jnp.stack(new_c[g])]
    return tuple(outs)
```

```python
import functools
import math

import jax
import jax.numpy as jnp
from jax import lax
from jax.experimental import pallas as pl
from jax.experimental.pallas import tpu as pltpu

NORM_EPS = 1e-6
RWKV_GN_EPS = 64e-5
HEAD_B = 64
DK_A = 128
LANES = 128
SUBLANES = 8
VMEM_LIMIT = 56 * 1024 * 1024


def _cparams(*sem):
    return pltpu.CompilerParams(dimension_semantics=sem, vmem_limit_bytes=VMEM_LIMIT)


def _tile(n, cap, mult=SUBLANES):
    best = None
    for d in range(mult, min(n, cap) + 1, mult):
        if n % d == 0:
            best = d
    return n if best is None else best


def _rms(x, gain):
    return x * lax.rsqrt(jnp.mean(x * x, axis=-1, keepdims=True) + NORM_EPS) * gain


def _silu(x):
    return x * jax.nn.sigmoid(x)


def _mxu_dot(a, w):
    return jnp.dot(a.astype(jnp.bfloat16), w.astype(jnp.bfloat16), preferred_element_type=jnp.float32)


def _split_bf16(x):
    hi = x.astype(jnp.bfloat16).astype(jnp.float32)
    return hi, x - hi


def _rms_mm_kernel(x_ref, g_ref, w_ref, o_ref, xn_ref):
    @pl.when(pl.program_id(1) == 0)
    def _():
        xn_ref[...] = _rms(x_ref[...], g_ref[...]).astype(xn_ref.dtype)

    o_ref[...] = _mxu_dot(xn_ref[...], w_ref[...])


def rms_mm(x, gain, w, tn, col0=0, n_cols=None):
    T, K = x.shape
    N = w.shape[1] - col0 if n_cols is None else n_cols
    assert col0 % tn == 0 and N % tn == 0
    tm = _tile(T, 1088)
    cb0 = col0 // tn
    return pl.pallas_call(
        _rms_mm_kernel,
        grid=(T // tm, N // tn),
        in_specs=[pl.BlockSpec((tm, K), lambda i, j: (i, 0)),
                  pl.BlockSpec((1, K), lambda i, j: (0, 0)),
                  pl.BlockSpec((K, tn), lambda i, j: (0, cb0 + j))],
        out_specs=pl.BlockSpec((tm, tn), lambda i, j: (i, j)),
        out_shape=jax.ShapeDtypeStruct((T, N), jnp.float32),
        scratch_shapes=[pltpu.VMEM((tm, K), jnp.bfloat16)],
        compiler_params=_cparams("parallel", "arbitrary"),
        name="rms_mm",
    )(x, gain.reshape(1, K), w)


def _mm_res_kernel(a_ref, w_ref, r_ref, o_ref):
    o_ref[...] = r_ref[...] + _mxu_dot(a_ref[...], w_ref[...])


def mm_res(a, w, res):
    T, K = a.shape
    N = w.shape[1]
    tm = _tile(T, 544, 2 * SUBLANES)
    return pl.pallas_call(
        _mm_res_kernel,
        grid=(T // tm,),
        in_specs=[pl.BlockSpec((tm, K), lambda i: (i, 0)),
                  pl.BlockSpec((K, N), lambda i: (0, 0), pipeline_mode=pl.Buffered(1)),
                  pl.BlockSpec((tm, N), lambda i: (i, 0))],
        out_specs=pl.BlockSpec((tm, N), lambda i: (i, 0)),
        out_shape=jax.ShapeDtypeStruct((T, N), jnp.float32),
        compiler_params=_cparams("parallel"),
        name="mm_res",
    )(a, w, res)


def _rms_only_kernel(x_ref, g_ref, o_ref):
    o_ref[...] = _rms(x_ref[...], g_ref[...])


def rms_only(x, gain):
    T, K = x.shape
    tm = _tile(T, 544)
    return pl.pallas_call(
        _rms_only_kernel,
        grid=(T // tm,),
        in_specs=[pl.BlockSpec((tm, K), lambda i: (i, 0)),
                  pl.BlockSpec((1, K), lambda i: (0, 0))],
        out_specs=pl.BlockSpec((tm, K), lambda i: (i, 0)),
        out_shape=jax.ShapeDtypeStruct((T, K), jnp.float32),
        compiler_params=_cparams("parallel"),
        name="rms_final",
    )(x, gain.reshape(1, K))


def _ffn_kernel(x_ref, g_ref, w1_ref, w3_ref, w2_ref, o_ref, xn_ref):
    j = pl.program_id(1)

    @pl.when(j == 0)
    def _():
        xn_ref[...] = _rms(x_ref[...], g_ref[...]).astype(xn_ref.dtype)
        o_ref[...] = jnp.zeros_like(o_ref)

    xn = xn_ref[...]
    h1 = _mxu_dot(xn, w1_ref[...])
    h3 = _mxu_dot(xn, w3_ref[...])
    o_ref[...] += _mxu_dot(_silu(h1) * h3, w2_ref[...])

    @pl.when(j == pl.num_programs(1) - 1)
    def _():
        o_ref[...] = x_ref[...] + o_ref[...]


def ffn(x, gain, w1, w3, w2, tf=256):
    T, D = x.shape
    F = w1.shape[1]
    tm = _tile(T, 1088)
    one = pl.Buffered(1)
    return pl.pallas_call(
        _ffn_kernel,
        grid=(T // tm, F // tf),
        in_specs=[pl.BlockSpec((tm, D), lambda i, j: (i, 0), pipeline_mode=one),
                  pl.BlockSpec((1, D), lambda i, j: (0, 0)),
                  pl.BlockSpec((D, tf), lambda i, j: (0, j)),
                  pl.BlockSpec((D, tf), lambda i, j: (0, j)),
                  pl.BlockSpec((tf, D), lambda i, j: (j, 0))],
        out_specs=pl.BlockSpec((tm, D), lambda i, j: (i, 0), pipeline_mode=one),
        out_shape=jax.ShapeDtypeStruct((T, D), jnp.float32),
        scratch_shapes=[pltpu.VMEM((tm, D), jnp.bfloat16)],
        compiler_params=_cparams("parallel", "arbitrary"),
        name="ffn",
    )(x, gain.reshape(1, D), w1, w3, w2)


def _router_kernel(x_ref, g_ref, rt_ref, i1_ref, i2_ref, g1_ref, g2_ref, *, n_exp):
    xn = _rms(x_ref[...], g_ref[...])
    logit = [jnp.sum(xn * rt_ref[e:e + 1, :], axis=-1, keepdims=True) for e in range(n_exp)]
    neg = jnp.float32(-jnp.inf)

    def top(vals):
        m = vals[0]
        for v in vals[1:]:
            m = jnp.maximum(m, v)
        idx = jnp.full(m.shape, n_exp, jnp.int32)
        for e in reversed(range(n_exp)):
            idx = jnp.where(vals[e] == m, e, idx)
        return m, idx

    m1, i1 = top(logit)
    rest = [jnp.where(i1 == e, neg, logit[e]) for e in range(n_exp)]
    m2, i2 = top(rest)
    d = jnp.exp(m2 - m1)
    i1_ref[...] = i1
    i2_ref[...] = i2
    g1_ref[...] = 1.0 / (1.0 + d)
    g2_ref[...] = d / (1.0 + d)


def router_top2(x, gain, router):
    T, D = x.shape
    E = router.shape[1]
    tm = _tile(T, 544)
    col = pl.BlockSpec((tm, 1), lambda i: (i, 0))
    return pl.pallas_call(
        functools.partial(_router_kernel, n_exp=E),
        grid=(T // tm,),
        in_specs=[pl.BlockSpec((tm, D), lambda i: (i, 0)),
                  pl.BlockSpec((1, D), lambda i: (0, 0)),
                  pl.BlockSpec((E, D), lambda i: (0, 0))],
        out_specs=[col, col, col, col],
        out_shape=[jax.ShapeDtypeStruct((T, 1), jnp.int32)] * 2 + [jax.ShapeDtypeStruct((T, 1), jnp.float32)] * 2,
        compiler_params=_cparams("parallel"),
        name="router",
    )(x, gain.reshape(1, D), router.T)


MOE_SUB = 256
MOE_UNROLL = 8


def _moe_plan(i1, i2, n_exp, cap):
    T = i1.shape[0]
    n_tiles = (2 * T) // cap + n_exp
    e_a = jnp.concatenate([i1[:, 0], i2[:, 0]])
    tok = jnp.concatenate([jnp.arange(T, dtype=jnp.int32)] * 2)
    onehot = (e_a[:, None] == jnp.arange(n_exp, dtype=jnp.int32)[None, :]).astype(jnp.int32)
    csum = jnp.cumsum(onehot, axis=0)
    rank = jnp.sum((csum - onehot) * onehot, axis=1)
    counts = csum[-1]
    nt_e = (counts + cap - 1) // cap
    t_end = jnp.cumsum(nt_e)
    t_start = t_end - nt_e
    pos = (t_start[e_a] * cap + rank).astype(jnp.int32)
    n_used = t_end[-1]
    tiles = jnp.arange(n_tiles, dtype=jnp.int32)
    t_buf = jnp.minimum(tiles, n_used - 1).astype(jnp.int32)
    t_exp = jnp.minimum(jnp.sum((t_buf[:, None] >= t_end[None, :]).astype(jnp.int32), axis=1), n_exp - 1)
    t_rows = jnp.where(tiles < n_used, jnp.clip(counts[t_exp] - (tiles - t_start[t_exp]) * cap, 0, cap), 0)
    src = jnp.zeros((n_tiles * cap,), jnp.int32).at[pos].set(tok)
    return t_exp.astype(jnp.int32), t_rows.astype(jnp.int32), t_buf, src, pos[:T], pos[T:]


DMA_UNROLL = 8


def _row_copy(src_hbm, row, dst, r, sem):
    return pltpu.make_async_copy(src_hbm.at[pl.ds(row, 1)], dst.at[pl.ds(r, 1)], sem)


def _wait_rows(dst, sem):
    pltpu.make_async_copy(dst, dst, sem).wait()


def _gather_norm_kernel(src_ref, rows_ref, x_hbm, g_ref, o_ref, buf, sem, *, sub_per_tile):
    i = pl.program_id(0)
    n = pl.num_programs(0)
    tg = o_ref.shape[0]

    def active(step):
        return (step % sub_per_tile) * tg < rows_ref[step // sub_per_tile]

    def issue(step):
        slot = step % 2

        def one(r, c):
            _row_copy(x_hbm, src_ref[step * tg + r], buf.at[slot], r, sem.at[slot]).start()
            return c

        lax.fori_loop(0, tg, one, 0, unroll=DMA_UNROLL)

    @pl.when((i == 0) & active(0))
    def _():
        issue(0)

    nxt = jnp.minimum(i + 1, n - 1)

    @pl.when((i + 1 < n) & active(nxt))
    def _():
        issue(nxt)

    @pl.when(active(i))
    def _():
        slot = i % 2
        _wait_rows(buf.at[slot], sem.at[slot])
        o_ref[...] = _rms(buf[slot], g_ref[...]).astype(o_ref.dtype)

    @pl.when(jnp.logical_not(active(i)))
    def _():
        o_ref[...] = jnp.zeros_like(o_ref)


def gather_norm(x, gain, src, t_rows, cap):
    T, D = x.shape
    n_rows = src.shape[0]
    tg = MOE_SUB
    return pl.pallas_call(
        functools.partial(_gather_norm_kernel, sub_per_tile=cap // tg),
        grid_spec=pltpu.PrefetchScalarGridSpec(
            num_scalar_prefetch=2,
            grid=(n_rows // tg,),
            in_specs=[pl.BlockSpec(memory_space=pl.ANY),
                      pl.BlockSpec((1, D), lambda i, s, r: (0, 0))],
            out_specs=pl.BlockSpec((tg, D), lambda i, s, r: (i, 0)),
            scratch_shapes=[pltpu.VMEM((2, tg, D), jnp.float32), pltpu.SemaphoreType.DMA((2,))]),
        out_shape=jax.ShapeDtypeStruct((n_rows, D), jnp.bfloat16),
        compiler_params=_cparams("arbitrary"),
        name="moe_gather",
    )(src, t_rows, x, gain.reshape(1, D))


def _moe_kernel(te_ref, tr_ref, tb_ref, xg_ref, w1_ref, w3_ref, w2_ref, o_ref):
    k = pl.program_id(0)
    j = pl.program_id(1)
    rows = tr_ref[k]

    @pl.when(j == 0)
    def _():
        o_ref[...] = jnp.zeros_like(o_ref)

    @pl.when(rows > 0)
    def _():
        w1 = w1_ref[...].astype(jnp.bfloat16)
        w3 = w3_ref[...].astype(jnp.bfloat16)
        w2 = w2_ref[...].astype(jnp.bfloat16)

        def up(s):
            xs = xg_ref[pl.ds(pl.multiple_of(s * MOE_SUB, MOE_SUB), MOE_SUB), :]
            h1 = jnp.dot(xs, w1, preferred_element_type=jnp.float32)
            h3 = jnp.dot(xs, w3, preferred_element_type=jnp.float32)
            return (_silu(h1) * h3).astype(jnp.bfloat16)

        def down(s, h):
            r0 = pl.multiple_of(s * MOE_SUB, MOE_SUB)
            o_ref[pl.ds(r0, MOE_SUB), :] += jnp.dot(h, w2, preferred_element_type=jnp.float32)

        def body(s, h):
            h_next = up(s + 1)
            down(s, h)
            return h_next

        def body_n(t, h):
            for i in range(MOE_UNROLL):
                h = body(MOE_UNROLL * t + i, h)
            return h

        last = (rows + MOE_SUB - 1) // MOE_SUB - 1
        h = lax.fori_loop(0, last // MOE_UNROLL, body_n, up(0))
        h = lax.fori_loop(MOE_UNROLL * (last // MOE_UNROLL), last, body, h)
        down(last, h)


def moe_experts(xg, t_exp, t_rows, t_buf, w1, w3, w2, cap, tf=256):
    n_rows, D = xg.shape
    E, _, F = w1.shape
    n_tiles = n_rows // cap
    nj = F // tf

    def jj(k, j, tr):
        return jnp.where(tr[k] > 0, j, nj - 1)

    one = pl.Buffered(1)
    return pl.pallas_call(
        _moe_kernel,
        grid_spec=pltpu.PrefetchScalarGridSpec(
            num_scalar_prefetch=3,
            grid=(n_tiles, nj),
            in_specs=[pl.BlockSpec((cap, D), lambda k, j, te, tr, tb: (tb[k], 0), pipeline_mode=one),
                      pl.BlockSpec((None, D, tf), lambda k, j, te, tr, tb: (te[k], 0, jj(k, j, tr))),
                      pl.BlockSpec((None, D, tf), lambda k, j, te, tr, tb: (te[k], 0, jj(k, j, tr))),
                      pl.BlockSpec((None, tf, D), lambda k, j, te, tr, tb: (te[k], jj(k, j, tr), 0))],
            out_specs=pl.BlockSpec((cap, D), lambda k, j, te, tr, tb: (k, 0), pipeline_mode=one)),
        out_shape=jax.ShapeDtypeStruct((n_rows, D), jnp.float32),
        compiler_params=_cparams("arbitrary", "arbitrary"),
        name="moe_experts",
    )(t_exp, t_rows, t_buf, xg, w1, w3, w2)


def _combine_kernel(p1_ref, p2_ref, x_ref, g1_ref, g2_ref, *rest, final_norm, n_first_tiles):
    if final_norm:
        gain_ref, y_hbm, oa_ref, ob_ref, a_buf, b_buf, sem = rest
    else:
        y_hbm, oa_ref, ob_ref, a_buf, b_buf, sem = rest
    i = pl.program_id(0)
    tc = x_ref.shape[0]

    def issue(step):
        slot = step % 2

        def one(r, c):
            _row_copy(y_hbm, p1_ref[step * tc + r], a_buf.at[slot], r, sem.at[0, slot]).start()
            _row_copy(y_hbm, p2_ref[step * tc + r], b_buf.at[slot], r, sem.at[1, slot]).start()
            return c

        lax.fori_loop(0, tc, one, 0, unroll=DMA_UNROLL)

    @pl.when(i == 0)
    def _():
        issue(0)

    @pl.when(i + 1 < pl.num_programs(0))
    def _():
        issue(i + 1)

    slot = i % 2
    _wait_rows(a_buf.at[slot], sem.at[0, slot])
    _wait_rows(b_buf.at[slot], sem.at[1, slot])
    y = x_ref[...] + (g1_ref[...] * a_buf[slot] + g2_ref[...] * b_buf[slot])
    y = _rms(y, gain_ref[...]) if final_norm else y

    @pl.when(i < n_first_tiles)
    def _():
        oa_ref[...] = y

    @pl.when(i >= n_first_tiles)
    def _():
        ob_ref[...] = y


def moe_combine(x, yg, pos1, pos2, g1, g2, n_first, final_gain=None):
    T, D = x.shape
    tc = _tile(math.gcd(n_first, T - n_first), 256)
    nf = n_first // tc
    final_norm = final_gain is not None
    tok = pl.BlockSpec((tc, D), lambda i, a, b: (i, 0))
    col = pl.BlockSpec((tc, 1), lambda i, a, b: (i, 0))
    in_specs = [tok, col, col] + ([pl.BlockSpec((1, D), lambda i, a, b: (0, 0))] if final_norm else [])
    in_specs.append(pl.BlockSpec(memory_space=pl.ANY))
    args = (x, g1, g2) + ((final_gain.reshape(1, D),) if final_norm else ()) + (yg,)
    return pl.pallas_call(
        functools.partial(_combine_kernel, final_norm=final_norm, n_first_tiles=nf),
        grid_spec=pltpu.PrefetchScalarGridSpec(
            num_scalar_prefetch=2,
            grid=(T // tc,),
            in_specs=in_specs,
            out_specs=[pl.BlockSpec((tc, D), lambda i, a, b: (jnp.minimum(i, nf - 1), 0)),
                       pl.BlockSpec((tc, D), lambda i, a, b: (jnp.maximum(i - nf, 0), 0))],
            scratch_shapes=[pltpu.VMEM((2, tc, D), jnp.float32), pltpu.VMEM((2, tc, D), jnp.float32),
                            pltpu.SemaphoreType.DMA((2, 2))]),
        out_shape=[jax.ShapeDtypeStruct((n_first, D), jnp.float32),
                   jax.ShapeDtypeStruct((T - n_first, D), jnp.float32)],
        compiler_params=_cparams("arbitrary"),
        name="moe_combine",
    )(pos1, pos2, *args)


def moe_top2(x, gain, router, w1, w3, w2, n_first, final_gain=None):
    T, D = x.shape
    E = router.shape[1]
    i1, i2, g1, g2 = router_top2(x, gain, router)
    cap = -(-(2 * T * 21 // (20 * E)) // MOE_SUB) * MOE_SUB
    t_exp, t_rows, t_buf, src, pos1, pos2 = _moe_plan(i1, i2, E, cap)
    xg = gather_norm(x, gain, src, t_rows, cap)
    yg = moe_experts(xg, t_exp, t_rows, t_buf, w1, w3, w2, cap)
    return moe_combine(x, yg, pos1, pos2, g1, g2, n_first, final_gain)


def _cumsum_rows(x):
    n = x.shape[0]
    row = lax.broadcasted_iota(jnp.int32, x.shape, 0)
    s = 1
    while s < n:
        x = x + jnp.where(row >= s, pltpu.roll(x, s, 0), 0.0)
        s *= 2
    return x


def _hgrn_kernel(q_ref, f_ref, i_ref, lbl_ref, s0_ref, o_ref, so_ref, s_scr, *, chunk, layer_j, valid_len):
    c = pl.program_id(1)
    tl = q_ref.shape[0]
    n_head, dk, _ = s_scr.shape

    @pl.when(c == 0)
    def _():
        s_scr[...] = s0_ref[...]

    lg = lbl_ref[...]
    ex = jnp.exp(lg - jnp.max(lg, axis=0, keepdims=True))
    lb_all = jnp.sum(ex[0:layer_j + 1, :], axis=0, keepdims=True) / jnp.sum(ex, axis=0, keepdims=True)

    pad = LANES - chunk
    zpad = jnp.zeros((pad, dk), jnp.float32)
    row = lax.broadcasted_iota(jnp.int32, (chunk, LANES), 0)
    col = lax.broadcasted_iota(jnp.int32, (chunk, LANES), 1)
    causal = col <= row
    ones_cd = jnp.ones((chunk, dk), jnp.float32)
    tn_dims = (((0,), (0,)), ((), ()))
    nt_dims = (((1,), (1,)), ((), ()))

    for cc in range(tl // chunk):
        sl = slice(cc * chunk, (cc + 1) * chunk)
        for h in range(n_head):
            hs = slice(h * dk, (h + 1) * dk)
            lb = lb_all[:, hs]
            f = lb + (1.0 - lb) * jax.nn.sigmoid(f_ref[sl, hs])
            q = _silu(q_ref[sl, hs])
            v = i_ref[sl, hs]
            if valid_len is not None:
                t_idx = c * tl + cc * chunk + lax.broadcasted_iota(jnp.int32, (chunk, dk), 0)
                ok = t_idx < valid_len
                f = jnp.where(ok, f, 1.0)
                q = jnp.where(ok, q, 0.0)
            g = jnp.log(f)
            k = 1.0 - f
            b = _cumsum_rows(g)
            mid = chunk // 2 - 1
            bm = b[mid:mid + 1, :]
            bl = b[chunk - 1:chunk, :]
            qe = q * jnp.exp(b - bm)
            ke = jnp.concatenate([k * jnp.exp(bm - b), zpad], axis=0)
            vp = jnp.concatenate([v, zpad], axis=0)
            a = lax.dot_general(qe, ke, nt_dims, preferred_element_type=jnp.float32)
            a = jnp.where(causal, a, 0.0)
            s = s_scr[h]
            o = jnp.dot(a, vp, preferred_element_type=jnp.float32)
            o = o + jnp.dot(q * jnp.exp(b), s, preferred_element_type=jnp.float32)
            o_ref[sl, hs] = o
            kd = jnp.concatenate([k * jnp.exp(bl - b), zpad], axis=0)
            ghi, glo = _split_bf16(g)
            blc = (lax.dot_general(ghi, ones_cd, tn_dims, preferred_element_type=jnp.float32)
                   + lax.dot_general(glo, ones_cd, tn_dims, preferred_element_type=jnp.float32))
            s_scr[h] = jnp.exp(blc) * s + lax.dot_general(kd, vp, tn_dims, preferred_element_type=jnp.float32)

    @pl.when(c == pl.num_programs(1) - 1)
    def _():
        so_ref[...] = s_scr[...]


def hgrn_scan(pa, B, L, tl, chunk, lb_logits, s0, layer_j, valid_len=None):
    H, dk = s0.shape[1], s0.shape[2]
    d_a = H * dk
    nt = L // tl
    kern = functools.partial(_hgrn_kernel, chunk=chunk, layer_j=layer_j, valid_len=valid_len)

    def col(off):
        return pl.BlockSpec((tl, d_a), lambda b, c: (b * nt + c, off))

    st = pl.BlockSpec((None, H, dk, dk), lambda b, c: (b, 0, 0, 0))
    return pl.pallas_call(
        kern,
        grid=(B, nt),
        in_specs=[col(0), col(1), col(2), pl.BlockSpec(lb_logits.shape, lambda b, c: (0, 0)), st],
        out_specs=[pl.BlockSpec((tl, d_a), lambda b, c: (b * nt + c, 0)), st],
        out_shape=[jax.ShapeDtypeStruct((B * L, d_a), jnp.float32),
                   jax.ShapeDtypeStruct(s0.shape, jnp.float32)],
        scratch_shapes=[pltpu.VMEM((H, dk, dk), jnp.float32)],
        compiler_params=_cparams("parallel", "arbitrary"),
        name="hgrn_scan",
    )(pa, pa, pa, lb_logits, s0)


def _seg_sum(x, bones):
    outs = []
    for j in range(x.shape[1] // LANES):
        hi, lo = _split_bf16(x[:, j * LANES:(j + 1) * LANES])
        outs.append(jnp.dot(hi, bones, preferred_element_type=jnp.float32)
                    + jnp.dot(lo, bones, preferred_element_type=jnp.float32))
    return jnp.concatenate(outs, axis=1)


def _rwkv_prep_kernel(u_ref, halo_ref, e_ref, pos_ref, mu_ref, w0_ref, a0_ref, kk_ref, ka_ref, rk_ref,
                      w2_ref, a2_ref, g2_ref, bones_ref,
                      kk_o, wr_o, w_o, kka_o, km_o, vhi_o, vlo_o, g_o, c1_o, c2_o, bon_o,
                      *, d_b, first_state_tile):
    u = u_ref[...]
    has_state = pl.program_id(0) >= first_state_tile
    u_prev = _prev_in_seq(u, halo_ref[...], 1, pos_ref[...], jnp.where(has_state, e_ref[...], 0.0))
    us = u + (u_prev - u) * mu_ref[...]
    r = us[:, 0:d_b]
    k = us[:, d_b:2 * d_b]
    v = us[:, 2 * d_b:3 * d_b]
    wa = us[:, 3 * d_b:3 * d_b + LANES]
    gd = us[:, 3 * d_b + LANES:3 * d_b + 2 * LANES]
    bones = bones_ref[...]
    z = w0_ref[...] + jnp.dot(jnp.tanh(wa), w2_ref[...], preferred_element_type=jnp.float32)
    nz = -z
    softplus = jnp.maximum(nz, 0.0) + jnp.log(1.0 + jnp.exp(-jnp.abs(nz)))
    w_log = -softplus - 0.5
    decay = jnp.exp(-jnp.exp(w_log))
    a = jax.nn.sigmoid(a0_ref[...] + jnp.dot(wa, a2_ref[...], preferred_element_type=jnp.float32))
    g = jnp.dot(jax.nn.sigmoid(gd), g2_ref[...], preferred_element_type=jnp.float32)
    kk = k * kk_ref[...]
    kk = kk / jnp.maximum(jnp.sqrt(_seg_sum(kk * kk, bones)), 1e-12)
    kmod = k * (1.0 + (a - 1.0) * ka_ref[...])
    kka = kk * a
    vhi, vlo = _split_bf16(v)
    kk_o[...] = kk
    wr_o[...] = decay * r
    w_o[...] = decay
    kka_o[...] = kka
    km_o[...] = kmod
    vhi_o[...] = vhi
    vlo_o[...] = vlo
    g_o[...] = g
    c1_o[...] = _seg_sum(kka * r, bones)
    c2_o[...] = _seg_sum(kmod * r, bones)
    bon_o[...] = _seg_sum(r * kmod * rk_ref[...], bones)


def rwkv_prep(u, pos, e1, tm, n_zero_rows, mu, w0, a0, k_k, k_a, r_k, w2p, a2p, g2, bones, d_b):
    T, DU = u.shape
    row = lambda n: pl.BlockSpec((1, n), lambda i: (0, 0))
    full = lambda a: pl.BlockSpec(a.shape, lambda i: (0, 0))
    first, specs = _seq_tiles(tm, n_zero_rows)
    tok_u, halo_u, start_u = specs(DU)
    tok = pl.BlockSpec((tm, d_b), lambda i: (i, 0))
    return pl.pallas_call(
        functools.partial(_rwkv_prep_kernel, d_b=d_b, first_state_tile=first),
        grid=(T // tm,),
        in_specs=[tok_u, halo_u, start_u, pl.BlockSpec((tm, 1), lambda i: (i, 0)),
                  row(DU), row(d_b), row(d_b), row(d_b), row(d_b), row(d_b),
                  full(w2p), full(a2p), full(g2), full(bones)],
        out_specs=[tok] * 11,
        out_shape=[jax.ShapeDtypeStruct((T, d_b), jnp.float32)] * 11,
        compiler_params=_cparams("parallel"),
        name="rwkv_prep",
    )(u, u, e1, pos, mu.reshape(1, DU), w0.reshape(1, d_b), a0.reshape(1, d_b), k_k.reshape(1, d_b),
      k_a.reshape(1, d_b), r_k.reshape(1, d_b), w2p, a2p, g2, bones)


SCAN_GROUP = 64


def _rwkv_scan_kernel(kk_ref, wr_ref, w_ref, kka_ref, k_ref, vhi_ref, vlo_ref, c1_ref, c2_ref,
                      s0_ref, rhs_ref, md_ref, y_ref, so_ref, s_scr, r2_o, sa_o, *, n_pair):
    tb = kk_ref.shape[-2]
    n = HEAD_B

    @pl.when(pl.program_id(1) == 0)
    def _():
        s_scr[...] = s0_ref[...]

    r2_o[...] = jnp.zeros_like(r2_o)
    sa_o[...] = jnp.zeros_like(sa_o)
    rhs = rhs_ref[...]
    md = md_ref[...]
    lane = lax.broadcasted_iota(jnp.int32, (n, LANES), 1) & (n - 1)

    def token_step(row, sel):
        parts = []
        for p in range(n_pair):
            sl = slice(p * LANES, (p + 1) * LANES)
            s = s_scr[p]
            p1 = s * row(kk_ref, sl)
            p2 = s * row(wr_ref, sl)
            dh = md * row(vhi_ref, sl)
            dl = md * row(vlo_ref, sl)
            parts.append(jnp.concatenate([p1, dh], axis=1))
            parts.append(jnp.concatenate([p2, dl], axis=1))
        out = jnp.dot(jnp.concatenate(parts, axis=0), rhs, preferred_element_type=jnp.float32)
        for p in range(n_pair):
            sl = slice(p * LANES, (p + 1) * LANES)
            top = out[2 * p * n:(2 * p + 1) * n]
            bot = out[(2 * p + 1) * n:(2 * p + 2) * n]
            sa_b = top[:, 0:LANES]
            r2_b = bot[:, 0:LANES]
            v_b = top[:, LANES:2 * LANES] + bot[:, LANES:2 * LANES]
            s = s_scr[p]
            s_scr[p] = s * row(w_ref, sl) - sa_b * row(kka_ref, sl) + v_b * row(k_ref, sl)
            r2_o[p] = jnp.where(sel, r2_b, r2_o[p])
            sa_o[p] = jnp.where(sel, sa_b, sa_o[p])

    if tb % SCAN_GROUP == 0:
        def group(gi, carry):
            for sub in range(SCAN_GROUP // SUBLANES):
                t0 = gi * SCAN_GROUP + sub * SUBLANES
                base = pl.multiple_of(t0, SUBLANES)
                for jj in range(SUBLANES):
                    token_step(lambda ref, sl, jj=jj, base=base: ref[pl.ds(base, SUBLANES), sl][jj:jj + 1, :],
                               lane == t0 + jj)
            return carry

        lax.fori_loop(0, tb // SCAN_GROUP, group, 0)
    else:
        for tt in range(tb):
            token_step(lambda ref, sl, tt=tt: ref[tt:tt + 1, sl], lane == tt)

    for p in range(n_pair):
        sl = slice(p * LANES, (p + 1) * LANES)
        zt = jnp.concatenate([r2_o[p], sa_o[p]], axis=0).T
        r2 = jnp.concatenate([zt[0:tb, 0:n], zt[n:n + tb, 0:n]], axis=1)
        sa = jnp.concatenate([zt[0:tb, n:2 * n], zt[n:n + tb, n:2 * n]], axis=1)
        v = vhi_ref[:, sl] + vlo_ref[:, sl]
        y_ref[:, sl] = r2 - sa * c1_ref[:, sl] + v * c2_ref[:, sl]

    @pl.when(pl.program_id(1) == pl.num_programs(1) - 1)
    def _():
        so_ref[...] = s_scr[...]


def rwkv_scan(vecs, tok, y_spec, y_shape, B, nt, s0_pair, rhs, md):
    n_pair = s0_pair.shape[1]
    st = pl.BlockSpec((None, n_pair, HEAD_B, LANES), lambda b, t: (b, 0, 0, 0))
    acc = pltpu.VMEM((n_pair, HEAD_B, LANES), jnp.float32)
    return pl.pallas_call(
        functools.partial(_rwkv_scan_kernel, n_pair=n_pair),
        grid=(B, nt),
        in_specs=[tok] * 9 + [st, pl.BlockSpec(rhs.shape, lambda b, t: (0, 0)),
                              pl.BlockSpec(md.shape, lambda b, t: (0, 0))],
        out_specs=[y_spec, st],
        out_shape=[y_shape, jax.ShapeDtypeStruct(s0_pair.shape, jnp.float32)],
        scratch_shapes=[acc, acc, acc],
        compiler_params=_cparams("parallel", "arbitrary"),
        name="rwkv_scan",
    )(*vecs, s0_pair, rhs, md)


def _mix_out_kernel(oa1_ref, oa2_ref, go_ref, hg_ref, y1_ref, y2_ref, vhi_ref, vlo_ref, g_ref, bon_ref,
                    lnw_ref, lnb_ref, bones_ref, o_ref, *, d_a, n_first_tiles):
    first = pl.program_id(0) < n_first_tiles
    oa = jnp.where(first, oa1_ref[...], oa2_ref[...])
    outs = []
    for h in range(d_a // DK_A):
        x = oa[:, h * DK_A:(h + 1) * DK_A]
        outs.append(x * lax.rsqrt(jnp.mean(x * x, axis=-1, keepdims=True) + NORM_EPS))
    o_a = jnp.concatenate(outs, axis=1) * hg_ref[...] * _silu(go_ref[...])
    bones = bones_ref[...]
    v = vhi_ref[...] + vlo_ref[...]
    y = jnp.where(first, y1_ref[...], y2_ref[...])
    inv_n = 1.0 / HEAD_B
    mean = _seg_sum(y, bones) * inv_n
    d = y - mean
    var = _seg_sum(d * d, bones) * inv_n
    yn = d * lax.rsqrt(var + RWKV_GN_EPS) * lnw_ref[...] + lnb_ref[...]
    o_b = (yn + bon_ref[...] * v) * g_ref[...]
    o_ref[:, 0:d_a] = o_a.astype(o_ref.dtype)
    o_ref[:, d_a:] = o_b.astype(o_ref.dtype)


def mix_out(oa_pair, pa, hg_row, y_pair, vhi, vlo, g, bon, ln_w, ln_b, bones):
    n_first, d_a = oa_pair[0].shape
    n_rest = oa_pair[1].shape[0]
    T = n_first + n_rest
    d_b = y_pair[0].shape[1]
    tm = _tile(math.gcd(n_first, n_rest), 256, 2 * SUBLANES)
    nf = n_first // tm
    tok = lambda n: pl.BlockSpec((tm, n), lambda i: (i, 0))
    row = lambda n: pl.BlockSpec((1, n), lambda i: (0, 0))
    lead = lambda n: pl.BlockSpec((tm, n), lambda i: (jnp.minimum(i, nf - 1), 0))
    rest = lambda n: pl.BlockSpec((tm, n), lambda i: (jnp.maximum(i - nf, 0), 0))
    return pl.pallas_call(
        functools.partial(_mix_out_kernel, d_a=d_a, n_first_tiles=nf),
        grid=(T // tm,),
        in_specs=[lead(d_a), rest(d_a), pl.BlockSpec((tm, d_a), lambda i: (i, 3)), row(d_a),
                  lead(d_b), rest(d_b)]
                 + [tok(d_b)] * 4 + [row(d_b), row(d_b), pl.BlockSpec(bones.shape, lambda i: (0, 0))],
        out_specs=tok(d_a + d_b),
        out_shape=jax.ShapeDtypeStruct((T, d_a + d_b), jnp.bfloat16),
        compiler_params=_cparams("parallel"),
        name="mix_out",
    )(*oa_pair, pa, hg_row, *y_pair, vhi, vlo, g, bon, ln_w.reshape(1, d_b), ln_b.reshape(1, d_b), bones)


def _conv_in_kernel(x_ref, g_ref, wb_ref, wc_ref, wh_ref, gb_ref, u_ref, xn_ref):
    @pl.when(pl.program_id(1) == 0)
    def _():
        xn_ref[...] = _rms(x_ref[...], g_ref[...]).astype(xn_ref.dtype)

    xn = xn_ref[...]
    gb_ref[...] = _mxu_dot(xn, wb_ref[...])
    u_ref[...] = _mxu_dot(xn, wc_ref[...]) * _mxu_dot(xn, wh_ref[...])


def conv_in(x, gain, w, tn=256):
    T, K = x.shape
    d = w.shape[1] // 3
    tm = _tile(T, 1088)
    nb = d // tn
    out = pl.BlockSpec((tm, tn), lambda i, j: (i, j))
    return pl.pallas_call(
        _conv_in_kernel,
        grid=(T // tm, nb),
        in_specs=[pl.BlockSpec((tm, K), lambda i, j: (i, 0)),
                  pl.BlockSpec((1, K), lambda i, j: (0, 0)),
                  pl.BlockSpec((K, tn), lambda i, j: (0, j)),
                  pl.BlockSpec((K, tn), lambda i, j: (0, nb + j)),
                  pl.BlockSpec((K, tn), lambda i, j: (0, 2 * nb + j))],
        out_specs=[out, out],
        out_shape=[jax.ShapeDtypeStruct((T, d), jnp.float32)] * 2,
        scratch_shapes=[pltpu.VMEM((tm, K), jnp.bfloat16)],
        compiler_params=_cparams("parallel", "arbitrary"),
        name="conv_in",
    )(x, gain.reshape(1, K), w, w, w)


def _prev_rows(tile, halo, k):
    rolled = pltpu.roll(tile, k, 0)
    row = lax.broadcasted_iota(jnp.int32, halo.shape, 0)
    head = jnp.where(row < k, pltpu.roll(halo, k, 0), rolled[0:SUBLANES])
    return jnp.concatenate([head, rolled[SUBLANES:]], axis=0)


def _prev_in_seq(tile, halo, k, pos, start_rows):
    return jnp.where(pos < k, start_rows, _prev_rows(tile, halo, k))


def _conv_mm_kernel(gb_ref, u_ref, halo_ref, pos_ref, e1_ref, e2_ref, ck_ref, w_ref, r_ref, o_ref,
                    *, first_state_tile):
    u = u_ref[...]
    pos = pos_ref[...]
    has_state = pl.program_id(0) >= first_state_tile
    u1 = _prev_in_seq(u, halo_ref[...], 1, pos, jnp.where(has_state, e1_ref[...], 0.0))
    u2 = _prev_in_seq(u, halo_ref[...], 2, pos, jnp.where(has_state, e2_ref[...], 0.0))
    y = ck_ref[0:1, :] * u2
    y = y + ck_ref[1:2, :] * u1
    y = y + ck_ref[2:3, :] * u
    o_ref[...] = r_ref[...] + _mxu_dot(gb_ref[...] * y, w_ref[...])


def _seq_tiles(tm, n_zero_rows):
    first = n_zero_rows // tm
    hb = tm // SUBLANES

    def specs(n):
        return (pl.BlockSpec((tm, n), lambda i, *_: (i, 0)),
                pl.BlockSpec((SUBLANES, n), lambda i, *_: (jnp.maximum(i * hb - 1, 0), 0)),
                pl.BlockSpec((tm, n), lambda i, *_: (jnp.maximum(i - first, 0), 0)))

    return first, specs


def conv_mm(gb, u, pos, e1, e2, conv_k, w, res, tm, n_zero_rows):
    T, d = u.shape
    N = w.shape[1]
    assert conv_k.shape[0] == 3
    first, specs = _seq_tiles(tm, n_zero_rows)
    tok, halo, start = specs(d)
    one = pl.Buffered(1)
    start = pl.BlockSpec(start.block_shape, start.index_map, pipeline_mode=one)
    return pl.pallas_call(
        functools.partial(_conv_mm_kernel, first_state_tile=first),
        grid=(T // tm,),
        in_specs=[tok, tok, halo, pl.BlockSpec((tm, 1), lambda i: (i, 0)), start, start,
                  pl.BlockSpec(conv_k.shape, lambda i: (0, 0)),
                  pl.BlockSpec((d, N), lambda i: (0, 0), pipeline_mode=one),
                  pl.BlockSpec((tm, N), lambda i: (i, 0))],
        out_specs=pl.BlockSpec((tm, N), lambda i: (i, 0)),
        out_shape=jax.ShapeDtypeStruct((T, N), jnp.float32),
        compiler_params=_cparams("parallel"),
        name="conv_mm",
    )(gb, u, u, pos, e1, e2, conv_k, w, res)


def _start_rows(state, L, k):
    B, ns, N = state.shape
    assert k <= min(L, ns)
    return jnp.zeros((B, L, N), state.dtype).at[:, :k].set(state[:, ns - k:]).reshape(B * L, N)


def _last_rows(x, groups, k):
    outs = []
    row = 0
    for B, L in groups:
        idx = [row + b * L + L - k + i for b in range(B) for i in range(k)]
        outs.append(jnp.take(x, jnp.asarray(idx, jnp.int32), axis=0).reshape(B, k, -1))
        row += B * L
    return outs


def _pair_state(s):
    B, H, n, _ = s.shape
    return s.reshape(B, H // 2, 2, n, n).transpose(0, 1, 3, 2, 4).reshape(B, H // 2, n, 2 * n)


def _unpair_state(s):
    B, hp, n, _ = s.shape
    return s.reshape(B, hp, n, 2, n).transpose(0, 1, 3, 2, 4).reshape(B, 2 * hp, n, n)


def kernel(x_prompt, x_sample, state_hgrn, state_rwkv, state_shift, state_conv, norm_mix, w_in_ab,
           hgrn_lb_logits, hgrn_norm, rwkv_mu, rwkv_w0, rwkv_w2, rwkv_a0, rwkv_a2, rwkv_g2, rwkv_k_k,
           rwkv_k_a, rwkv_r_k, rwkv_ln_w, rwkv_ln_b, w_out_ab, norm_ffn, ffn_w1, ffn_w3, ffn_w2,
           conv_w_in, conv_k, conv_w_out, moe_router, moe_w1, moe_w3, moe_w2, norm_final):
    f32 = jnp.float32
    Bp, Lp, D = x_prompt.shape
    Bs, Ls, _ = x_sample.shape
    Tp, Ts = Bp * Lp, Bs * Ls
    groups = ((Bp, Lp), (Bs, Ls))
    n_even = w_in_ab.shape[0]
    n_odd = conv_w_in.shape[0]
    depth = n_even + n_odd
    H_A, dk = state_hgrn.shape[2], state_hgrn.shape[3]
    d_a = H_A * dk
    H_B, hb = state_rwkv.shape[2], state_rwkv.shape[3]
    d_b = H_B * hb
    d_shift = state_shift.shape[-1]
    lora_w = rwkv_w2.shape[1]
    lora_a = rwkv_a2.shape[1]
    assert hb == HEAD_B and dk == DK_A and lora_w + lora_a == LANES and H_B % 2 == 0
    assert w_in_ab.shape[2] == 4 * d_a + d_shift

    seg = jnp.arange(LANES) // HEAD_B
    bones = (seg[:, None] == seg[None, :]).astype(f32)
    seg2 = jnp.arange(2 * LANES) // HEAD_B
    rhs_scan = (seg2[:, None] == seg2[None, :]).astype(f32)
    md = (jnp.arange(LANES)[None, :] % HEAD_B == jnp.arange(HEAD_B)[:, None]).astype(f32)

    x = jnp.concatenate([x_prompt.reshape(Tp, D), x_sample.reshape(Ts, D)], axis=0)
    pos = jnp.concatenate([jnp.tile(jnp.arange(Lp, dtype=jnp.int32), Bp),
                           jnp.tile(jnp.arange(Ls, dtype=jnp.int32), Bs)]).reshape(Tp + Ts, 1)
    seq_tile = math.gcd(Tp, Ts)
    assert seq_tile % SUBLANES == 0
    new_h, new_r, new_s, new_c = ([], []), ([], []), ([], []), ([], [])

    for layer in range(depth):
        j = layer // 2
        if layer % 2 == 0:
            w_in = w_in_ab[j]
            tn_u = _tile(d_shift, 512, LANES)
            assert (4 * d_a) % tn_u == 0
            pa = rms_mm(x, norm_mix[layer], w_in, tn=1024, n_cols=4 * d_a)
            u = rms_mm(x, norm_mix[layer], w_in, tn=tn_u, col0=4 * d_a)
            zeros_h = jnp.zeros((Bp,) + state_hgrn.shape[2:], f32)
            tl = _tile(Lp, 128)
            oa_p, sh_p = hgrn_scan(pa, Bp, Lp, tl, min(32, tl), hgrn_lb_logits, zeros_h, j)
            lpad = -(-Ls // SUBLANES) * SUBLANES
            pa_s = jnp.pad(pa[Tp:, :3 * d_a].reshape(Bs, Ls, 3 * d_a), ((0, 0), (0, lpad - Ls), (0, 0)))
            oa_s, sh_s = hgrn_scan(pa_s.reshape(Bs * lpad, 3 * d_a), Bs, lpad, lpad, lpad, hgrn_lb_logits,
                                   state_hgrn[j], j, valid_len=Ls)
            oa = (oa_p, oa_s.reshape(Bs, lpad, d_a)[:, :Ls].reshape(Ts, d_a))
            new_h[0].append(sh_p)
            new_h[1].append(sh_s)
            e_shift = _start_rows(state_shift[j][:, None, :], Ls, 1)
            w2p = jnp.concatenate([rwkv_w2[j], jnp.zeros((lora_a, d_b), f32)], axis=0)
            a2p = jnp.concatenate([jnp.zeros((lora_w, d_b), f32), rwkv_a2[j]], axis=0)
            prep = rwkv_prep(u, pos, e_shift, _tile(seq_tile, 256), Tp, rwkv_mu[j], rwkv_w0[j], rwkv_a0[j],
                             rwkv_k_k[j], rwkv_k_a[j], rwkv_r_k[j].reshape(d_b), w2p, a2p, rwkv_g2[j],
                             bones, d_b)
            kk, wr, wdec, kka, kmod, vhi, vlo, gg, c1, c2, bon = prep
            scan_in = (kk, wr, wdec, kka, kmod, vhi, vlo, c1, c2)
            tb_p = _tile(Lp, HEAD_B)
            nt_p = Lp // tb_p
            zeros_r = jnp.zeros((Bp, H_B // 2, hb, 2 * hb), f32)
            tok_p = pl.BlockSpec((tb_p, d_b), lambda b, t: (b * nt_p + t, 0))
            y_p, sr_p = rwkv_scan(scan_in, tok_p, tok_p, jax.ShapeDtypeStruct((Tp, d_b), f32), Bp, nt_p,
                                  zeros_r, rhs_scan, md)
            tok_s = pl.BlockSpec((None, Ls, d_b), lambda b, t: (b, 0, 0))
            y_s, sr_s = rwkv_scan([a[Tp:].reshape(Bs, Ls, d_b) for a in scan_in], tok_s, tok_s,
                                  jax.ShapeDtypeStruct((Bs, Ls, d_b), f32), Bs, 1,
                                  _pair_state(state_rwkv[j]), rhs_scan, md)
            yb = (y_p, y_s.reshape(Ts, d_b))
            new_r[0].append(_unpair_state(sr_p))
            new_r[1].append(_unpair_state(sr_s))
            last_u = _last_rows(u, groups, 1)
            new_s[0].append(last_u[0][:, 0])
            new_s[1].append(last_u[1][:, 0])
            hg_row = jnp.tile(hgrn_norm[j], H_A).reshape(1, d_a)
            o = mix_out(oa, pa, hg_row, yb, vhi, vlo, gg, bon, rwkv_ln_w[j], rwkv_ln_b[j], bones)
            x = mm_res(o, w_out_ab[j], x)
            x = ffn(x, norm_ffn[layer], ffn_w1[j], ffn_w3[j], ffn_w2[j])
        else:
            d_c = state_conv.shape[-1]
            cw = state_conv.shape[2]
            gb, uc = conv_in(x, norm_mix[layer], conv_w_in[j])
            last_c = _last_rows(uc, groups, cw)
            new_c[0].append(last_c[0])
            new_c[1].append(last_c[1])
            x = conv_mm(gb, uc, pos, _start_rows(state_conv[j], Ls, 1), _start_rows(state_conv[j], Ls, 2),
                        conv_k[j], conv_w_out[j], x, _tile(seq_tile, 256), Tp)
            xp, xs = moe_top2(x, norm_ffn[layer], moe_router[j], moe_w1[j], moe_w3[j], moe_w2[j], Tp,
                              final_gain=norm_final if layer == depth - 1 else None)
            if layer < depth - 1:
                x = jnp.concatenate([xp, xs], axis=0)

    if depth % 2 == 1:
        y = rms_only(x, norm_final)
        xp, xs = y[:Tp], y[Tp:]
    outs = [xp.reshape(Bp, Lp, D), xs.reshape(Bs, Ls, D)]
    for g in (0, 1):
        outs += [jnp.stack(new_h[g]), jnp.stack(new_r[g]), jnp.stack(new_s[g]), jnp.stack(new_c[g])]
    return tuple(outs)
```
